```python
import jax, jax.numpy as jnp
from jax import lax
import numpy as np

D_MODEL = 1024
BATCH = 16
SEQ = 2048
DEPTH = 1
DEC_BATCH = 32
DEC_SEQ = 4
PAST_LEN = 16384
PAGE_SIZE = 128

HEAD_DIM = 64
ROPE_THETA = 10000.0
NORM_EPS = 1e-6
NEG_BIG = -1e30

A_HEADS = 8
A_BLOCK = 256
A_TOPK = 3
A_WIDTH = A_HEADS * HEAD_DIM
A_Q_CHUNK = 8

B_HEADS = 8
B_KV_HEADS = 2
B_GROUP = B_HEADS // B_KV_HEADS
B_WIDTH = B_HEADS * HEAD_DIM
B_KV_WIDTH = B_KV_HEADS * HEAD_DIM
CMP_LEN = 32
CMP_STRIDE = 16
CMP_HIDDEN = 64
SEL_BLOCK = 64
SEL_TOPN = 16
WINDOW = 512
FORCE_SCORE = 1e4
B_Q_CHUNK = 32

MEM_LEN = 256
C_HEADS = 4
C_HEAD_DIM = 128
C_WIDTH = C_HEADS * C_HEAD_DIM

P_HEADS = 8
P_NKEYS = 128
P_EXPERTS = P_NKEYS * P_NKEYS
P_QDIM = 256
P_TOPK = 16
P_TOKEN_BLOCK = 256

IN_SPLITS = (A_WIDTH, A_WIDTH, A_WIDTH, B_WIDTH) + (B_KV_WIDTH,) * 6 + (3 * B_HEADS, D_MODEL, D_MODEL)
N_IN = sum(IN_SPLITS)

kernel_name = 'moba_nsa_peer_hybrid_step'


def rms_norm(x, g):
    xf = x.astype(jnp.float32)
    y = xf * lax.rsqrt(jnp.mean(xf * xf, axis=-1, keepdims=True) + NORM_EPS)
    return (y * g.astype(jnp.float32)).astype(x.dtype)


def rotary(x, pos):
    half = x.shape[-1] // 2
    inv_freq = ROPE_THETA ** (-jnp.arange(half, dtype=jnp.float32) / half)
    ang = pos.astype(jnp.float32)[:, None] * inv_freq[None, :]
    cos = jnp.cos(ang)[:, None, :]
    sin = jnp.sin(ang)[:, None, :]
    xf = x.astype(jnp.float32)
    x1, x2 = xf[..., :half], xf[..., half:]
    return jnp.concatenate([x1 * cos - x2 * sin, x2 * cos + x1 * sin], axis=-1).astype(x.dtype)


def masked_softmax(s, mask):
    p = jax.nn.softmax(jnp.where(mask, s, NEG_BIG), axis=-1)
    return jnp.where(mask, p, 0.0)


def _query_chunks(fn, chunk, q_pos, *qs):
    t = q_pos.shape[0]
    c = min(chunk, t)
    n = -(-t // c)
    pad = n * c - t
    pos = jnp.pad(q_pos, (0, pad), mode='edge').reshape(n, c)

    def split(a):
        a = jnp.pad(a, [(0, 0), (0, pad)] + [(0, 0)] * (a.ndim - 2))
        return jnp.moveaxis(a.reshape(a.shape[0], n, c, *a.shape[2:]), 1, 0)

    out = lax.map(lambda args: fn(*args), (pos, *[split(a) for a in qs]))
    out = jnp.moveaxis(out, 0, 1)
    return out.reshape(out.shape[0], n * c, *out.shape[3:])[:, :t]


def moba_attention(q, kv, q_pos):
    b, l, _, h, dh = kv.shape
    nb = -(-l // A_BLOCK)
    kvb = jnp.pad(kv, ((0, 0), (0, nb * A_BLOCK - l), (0, 0), (0, 0), (0, 0)))
    kvb = kvb.reshape(b, nb, A_BLOCK, 2, h, dh).transpose(0, 4, 1, 2, 3, 5)
    k_mean = jnp.mean(kvb[..., 0, :], axis=3, dtype=jnp.float32)
    n_top = min(A_TOPK, nb)
    bi = jnp.arange(b)[:, None, None, None]
    hi = jnp.arange(h)[None, :, None, None]
    blk = jnp.arange(nb)
    off = jnp.arange(A_BLOCK)
    scale = dh ** -0.5

    def chunk(pos, qc):
        qh = qc.transpose(0, 2, 1, 3)
        cur = pos // A_BLOCK
        gate = jnp.einsum('bhqd,bhnd->bhqn', qh.astype(jnp.float32), k_mean)
        gate = jnp.where(blk[None, :] < cur[:, None], gate, -jnp.inf)
        _, top = lax.top_k(gate, n_top)
        own = jnp.broadcast_to(cur[:, None], top.shape[:-1] + (1,)).astype(top.dtype)
        idx = jnp.concatenate([top, own], axis=-1)
        ok = jnp.concatenate([top < cur[:, None], jnp.ones(own.shape, bool)], axis=-1)
        g = kvb[bi, hi, idx]
        kpos = idx[..., None] * A_BLOCK + off
        mask = (ok[..., None] & (kpos <= pos[:, None, None])).reshape(*idx.shape[:3], -1)
        s = jnp.einsum('bhqd,bhqnkd->bhqnk', qh, g[..., 0, :], preferred_element_type=jnp.float32) * scale
        p = masked_softmax(s.reshape(mask.shape), mask)
        o = jnp.einsum('bhqm,bhqmd->bhqd', p.astype(kv.dtype), g[..., 1, :].reshape(*mask.shape, dh))
        return o.transpose(0, 2, 1, 3)

    return _query_chunks(chunk, A_Q_CHUNK, q_pos, q)


def _compress(x, pe, w1, w2, tok):
    b, _, g, dh = x.shape
    blocks = x[:, tok] + pe[:, None, :]
    flat = blocks.transpose(0, 1, 3, 2, 4).reshape(b, tok.shape[0], g, CMP_LEN * dh)
    return jax.nn.gelu(flat @ w1) @ w2


def nsa_attention(q, q_rot, gates, kv4, win_kv, win_pos0, q_pos, pe_cmp, w_ck1, w_ck2, w_cv1, w_cv2):
    b, l, _, g, dh = kv4.shape
    scale = dh ** -0.5
    f32 = jnp.float32
    n_cmp = (l - CMP_LEN) // CMP_STRIDE + 1
    tok = np.arange(n_cmp)[:, None] * CMP_STRIDE + np.arange(CMP_LEN)[None, :]
    k_cmp = _compress(kv4[:, :, 0], pe_cmp, w_ck1, w_ck2, tok)
    v_cmp = _compress(kv4[:, :, 1], pe_cmp, w_cv1, w_cv2, tok)
    cmp_end = jnp.asarray(tok[:, -1], jnp.int32)
    n_sel = -(-l // SEL_BLOCK)
    c_start = np.arange(n_cmp)[:, None] * CMP_STRIDE
    s_start = np.arange(n_sel)[None, :] * SEL_BLOCK
    overlap = np.clip(np.minimum(c_start + CMP_LEN, s_start + SEL_BLOCK) - np.maximum(c_start, s_start), 0, None)
    overlap = jnp.asarray(overlap / CMP_STRIDE, f32)
    kvs = jnp.pad(kv4[:, :, 2:], ((0, 0), (0, n_sel * SEL_BLOCK - l), (0, 0), (0, 0), (0, 0)))
    kvs = kvs.reshape(b, n_sel, SEL_BLOCK, 2, g, dh).transpose(0, 4, 1, 2, 3, 5)
    n_top = min(SEL_TOPN, n_sel)
    bi = jnp.arange(b)[:, None, None, None]
    gi = jnp.arange(g)[None, :, None, None]
    blk = jnp.arange(n_sel)
    off = jnp.arange(SEL_BLOCK)
    wpad = jnp.pad(win_kv, ((0, 0), (WINDOW, B_Q_CHUNK), (0, 0), (0, 0), (0, 0)))

    def to_groups(a):
        return a.reshape(b, a.shape[1], g, B_GROUP, a.shape[-1]).transpose(0, 2, 3, 1, 4)

    def chunk(pos, qc, qrc, gc):
        c = pos.shape[0]
        qg = to_groups(qc)
        qrg = to_groups(qrc)
        gt = jax.nn.sigmoid(to_groups(gc).astype(f32))
        s = jnp.einsum('bgjqd,bngd->bgjqn', qg, k_cmp, preferred_element_type=f32) * scale
        p_cmp = masked_softmax(s, cmp_end[None, :] <= pos[:, None])
        o_cmp = jnp.einsum('bgjqn,bngd->bgjqd', p_cmp.astype(v_cmp.dtype), v_cmp)
        cur = pos // SEL_BLOCK
        imp = jnp.einsum('bgjqn,nm->bgqm', p_cmp, overlap)
        forced = (blk[None, :] == 0) | (blk[None, :] == cur[:, None]) | (blk[None, :] == cur[:, None] - 1)
        score = jnp.where(blk[None, :] <= cur[:, None], jnp.where(forced, FORCE_SCORE, imp), -jnp.inf)
        _, idx = lax.top_k(score, n_top)
        sg = kvs[bi, gi, idx]
        kpos = idx[..., None] * SEL_BLOCK + off
        smask = ((idx <= cur[:, None])[..., None] & (kpos <= pos[:, None, None])).reshape(b, g, 1, c, -1)
        s = jnp.einsum('bgjqd,bgqnkd->bgjqnk', qrg, sg[..., 0, :], preferred_element_type=f32)
        p = masked_softmax(s.reshape(b, g, B_GROUP, c, -1) * scale, smask)
        o_sel = jnp.einsum('bgjqm,bgqmd->bgjqd', p.astype(kv4.dtype), sg[..., 1, :].reshape(b, g, c, -1, dh))
        wlen = WINDOW + c - 1
        wk = lax.dynamic_slice_in_dim(wpad, pos[0] - win_pos0 + 1, wlen, axis=1)
        kpos_w = pos[0] + 1 - WINDOW + jnp.arange(wlen)
        wmask = ((kpos_w[None, :] <= pos[:, None]) & (kpos_w[None, :] > pos[:, None] - WINDOW)
                 & (kpos_w[None, :] >= win_pos0))
        s = jnp.einsum('bgjqd,bkgd->bgjqk', qrg, wk[:, :, 0], preferred_element_type=f32) * scale
        p = masked_softmax(s, wmask)
        o_win = jnp.einsum('bgjqk,bkgd->bgjqd', p.astype(win_kv.dtype), wk[:, :, 1])
        o = gt[..., 0:1] * o_cmp + gt[..., 1:2] * o_sel + gt[..., 2:3] * o_win
        return o.astype(q.dtype).transpose(0, 3, 1, 2, 4).reshape(b, c, B_HEADS, dh)

    return _query_chunks(chunk, B_Q_CHUNK, q_pos, q, q_rot, gates)


def memory_kv(mem, g_mem, w_ckv):
    b, m, _ = mem.shape
    return (rms_norm(mem, g_mem) @ w_ckv).reshape(b, m, 2, C_HEADS, C_HEAD_DIM)


def cross_attention(x, mem_kv, w_cq, w_co):
    b, t, _ = x.shape
    q = (x @ w_cq).reshape(b, t, C_HEADS, C_HEAD_DIM)
    s = jnp.einsum('bqhd,bmhd->bhqm', q, mem_kv[:, :, 0], preferred_element_type=jnp.float32) * C_HEAD_DIM ** -0.5
    p = jax.nn.softmax(s, axis=-1)
    o = jnp.einsum('bhqm,bmhd->bqhd', p.astype(x.dtype), mem_kv[:, :, 1])
    return o.reshape(b, t, C_WIDTH) @ w_co


def peer_ffn(z, w_pq, sub_k1, sub_k2, peer_u, peer_v):
    b, t, d = z.shape
    n = b * t
    nblk = -(-n // P_TOKEN_BLOCK)
    zb = jnp.pad(z.reshape(n, d), ((0, nblk * P_TOKEN_BLOCK - n), (0, 0))).reshape(nblk, P_TOKEN_BLOCK, d)
    half = P_QDIM // 2
    f32 = jnp.float32

    def block(zc):
        q = (zc @ w_pq).reshape(-1, P_HEADS, 2, half)
        s1 = jnp.einsum('phc,kc->phk', q[:, :, 0], sub_k1, preferred_element_type=f32)
        s2 = jnp.einsum('phc,kc->phk', q[:, :, 1], sub_k2, preferred_element_type=f32)
        v1, i1 = lax.top_k(s1, P_TOPK)
        v2, i2 = lax.top_k(s2, P_TOPK)
        cand = (v1[..., :, None] + v2[..., None, :]).reshape(*v1.shape[:2], -1)
        cid = (i1[..., :, None] * P_NKEYS + i2[..., None, :]).reshape(*i1.shape[:2], -1)
        sc, j = lax.top_k(cand, P_TOPK)
        eid = jnp.take_along_axis(cid, j, axis=-1)
        gate = jax.nn.softmax(sc, axis=-1)
        act = jax.nn.gelu(jnp.einsum('pd,phkd->phk', zc, peer_u[eid], preferred_element_type=f32))
        return jnp.einsum('phk,phkd->pd', (gate * act).astype(zc.dtype), peer_v[eid])

    return lax.map(block, zb).reshape(-1, d)[:n].reshape(b, t, d)


def _mixer_projection(x, pos, g_attn, w_in):
    b, t, _ = x.shape
    offs = [int(o) for o in np.cumsum(IN_SPLITS)[:-1]]
    parts = jnp.split(rms_norm(x, g_attn) @ w_in, offs, axis=-1)

    def heads(a, h):
        return a.reshape(b, t, h, HEAD_DIM)

    a_q = rotary(heads(parts[0], A_HEADS), pos)
    a_kv = jnp.stack([rotary(heads(parts[1], A_HEADS), pos), heads(parts[2], A_HEADS)], axis=2)
    b_q = heads(parts[3], B_HEADS)
    b_qr = rotary(b_q, pos)
    b_kv = jnp.stack([heads(parts[4], B_KV_HEADS), heads(parts[5], B_KV_HEADS),
                      rotary(heads(parts[6], B_KV_HEADS), pos), heads(parts[7], B_KV_HEADS)], axis=2)
    b_win = jnp.stack([rotary(heads(parts[8], B_KV_HEADS), pos), heads(parts[9], B_KV_HEADS)], axis=2)
    b_gate = parts[10].reshape(b, t, B_HEADS, 3)
    return a_q, a_kv, b_q, b_qr, b_kv, b_win, b_gate, parts[11], parts[12]


def _merge(o_a, o_b, gate_a, gate_b, p_a, p_b, w_o):
    b, t = o_a.shape[:2]
    y_a = o_a.reshape(b, t, A_WIDTH) @ p_a
    y_b = o_b.reshape(b, t, B_WIDTH) @ p_b
    return (jax.nn.sigmoid(gate_a) * y_a + jax.nn.sigmoid(gate_b) * y_b) @ w_o


def _channel_stages(h, mem_kv, g_cross, w_cq, w_co, g_ffn, w_pq, sub_k1, sub_k2, peer_u, peer_v):
    h = h + cross_attention(rms_norm(h, g_cross), mem_kv, w_cq, w_co)
    return h + peer_ffn(rms_norm(h, g_ffn), w_pq, sub_k1, sub_k2, peer_u, peer_v)


def prompt_layer(x, mem, g_attn, w_in, pe_cmp, w_ck1, w_ck2, w_cv1, w_cv2, p_a, p_b, w_o,
                 g_cross, g_mem, w_cq, w_ckv, w_co, g_ffn, w_pq, sub_k1, sub_k2, peer_u, peer_v):
    t = x.shape[1]
    pos = jnp.arange(t, dtype=jnp.int32)
    a_q, a_kv, b_q, b_qr, b_kv, b_win, b_gate, gate_a, gate_b = _mixer_projection(x, pos, g_attn, w_in)
    o_a = moba_attention(a_q, a_kv, pos)
    o_b = nsa_attention(b_q, b_qr, b_gate, b_kv, b_win, 0, pos, pe_cmp, w_ck1, w_ck2, w_cv1, w_cv2)
    h = x + _merge(o_a, o_b, gate_a, gate_b, p_a, p_b, w_o)
    mem_kv = memory_kv(mem, g_mem, w_ckv)
    h = _channel_stages(h, mem_kv, g_cross, w_cq, w_co, g_ffn, w_pq, sub_k1, sub_k2, peer_u, peer_v)
    return h, a_kv, b_kv, b_win[:, t - min(WINDOW, t):], mem_kv


def sample_layer(x, moba_pool, nsa_pool, win_state, mem_kv, page_table, g_attn, w_in, pe_cmp, w_ck1, w_ck2,
                 w_cv1, w_cv2, p_a, p_b, w_o, g_cross, w_cq, w_co, g_ffn, w_pq, sub_k1, sub_k2, peer_u, peer_v):
    b, t, _ = x.shape
    past = page_table.shape[1] * PAGE_SIZE
    pos = past + jnp.arange(t, dtype=jnp.int32)
    a_q, a_kv, b_q, b_qr, b_kv, b_win, b_gate, gate_a, gate_b = _mixer_projection(x, pos, g_attn, w_in)
    moba_past = moba_pool[page_table].reshape(b, past, 2, A_HEADS, HEAD_DIM)
    nsa_past = nsa_pool[page_table].reshape(b, past, 4, B_KV_HEADS, HEAD_DIM)
    o_a = moba_attention(a_q, jnp.concatenate([moba_past, a_kv], axis=1), pos)
    wb = win_state.shape[1]
    win = jnp.concatenate([win_state, b_win], axis=1)
    o_b = nsa_attention(b_q, b_qr, b_gate, jnp.concatenate([nsa_past, b_kv], axis=1), win, past - wb, pos,
                        pe_cmp, w_ck1, w_ck2, w_cv1, w_cv2)
    h = x + _merge(o_a, o_b, gate_a, gate_b, p_a, p_b, w_o)
    h = _channel_stages(h, mem_kv, g_cross, w_cq, w_co, g_ffn, w_pq, sub_k1, sub_k2, peer_u, peer_v)
    new_wb = min(WINDOW, wb + t)
    return h, a_kv, b_kv, win[:, wb + t - new_wb:]


def setup_inputs(seed: int = 0) -> dict:
    key = jax.random.key(seed)
    ks = jax.random.split(key, 40)
    f32 = jnp.float32

    def nrm(k, shape, scale):
        return jax.random.normal(k, shape, f32) * scale

    def gain(k, shape):
        return 1.0 + 0.05 * jax.random.normal(k, shape, f32)

    n_pages = PAST_LEN // PAGE_SIZE
    n_used = DEC_BATCH * n_pages
    n_pool = n_used + max(n_used // 4, 1)
    page_table = jax.random.permutation(ks[7], n_pool)[:n_used].reshape(DEC_BATCH, n_pages).astype(jnp.int32)
    win_len = min(WINDOW, PAST_LEN)
    L = DEPTH
    return {
        'x_prompt': nrm(ks[0], (BATCH, SEQ, D_MODEL), 1.0),
        'x_sample': nrm(ks[1], (DEC_BATCH, DEC_SEQ, D_MODEL), 1.0),
        'cache_moba_kv': nrm(ks[2], (L, n_pool, PAGE_SIZE, 2, A_HEADS, HEAD_DIM), 1.0),
        'cache_nsa_kv': nrm(ks[3], (L, n_pool, PAGE_SIZE, 4, B_KV_HEADS, HEAD_DIM), 1.0),
        'state_nsa_win': nrm(ks[4], (L, DEC_BATCH, win_len, 2, B_KV_HEADS, HEAD_DIM), 1.0),
        'cache_mem_kv': nrm(ks[5], (L, DEC_BATCH, MEM_LEN, 2, C_HEADS, C_HEAD_DIM), 1.0),
        'page_table': page_table,
        'mem_prompt': nrm(ks[6], (BATCH, MEM_LEN, D_MODEL), 1.0),
        'g_attn': gain(ks[8], (L, D_MODEL)),
        'w_in': nrm(ks[9], (L, D_MODEL, N_IN), D_MODEL ** -0.5),
        'pe_cmp': nrm(ks[10], (L, CMP_LEN, HEAD_DIM), 0.1),
        'w_ck1': nrm(ks[11], (L, CMP_LEN * HEAD_DIM, CMP_HIDDEN), (CMP_LEN * HEAD_DIM) ** -0.5),
        'w_ck2': nrm(ks[12], (L, CMP_HIDDEN, HEAD_DIM), CMP_HIDDEN ** -0.5),
        'w_cv1': nrm(ks[13], (L, CMP_LEN * HEAD_DIM, CMP_HIDDEN), (CMP_LEN * HEAD_DIM) ** -0.5),
        'w_cv2': nrm(ks[14], (L, CMP_HIDDEN, HEAD_DIM), CMP_HIDDEN ** -0.5),
        'p_a': nrm(ks[15], (L, A_WIDTH, D_MODEL), A_WIDTH ** -0.5),
        'p_b': nrm(ks[16], (L, B_WIDTH, D_MODEL), B_WIDTH ** -0.5),
        'w_o': nrm(ks[17], (L, D_MODEL, D_MODEL), D_MODEL ** -0.5),
        'g_cross': gain(ks[18], (L, D_MODEL)),
        'g_mem': gain(ks[19], (L, D_MODEL)),
        'w_cq': nrm(ks[20], (L, D_MODEL, C_WIDTH), D_MODEL ** -0.5),
        'w_ckv': nrm(ks[21], (L, D_MODEL, 2 * C_WIDTH), D_MODEL ** -0.5),
        'w_co': nrm(ks[22], (L, C_WIDTH, D_MODEL), C_WIDTH ** -0.5),
        'g_ffn': gain(ks[23], (L, D_MODEL)),
        'w_pq': nrm(ks[24], (L, D_MODEL, P_HEADS * P_QDIM), D_MODEL ** -0.5),
        'sub_k1': nrm(ks[25], (L, P_NKEYS, P_QDIM // 2), (P_QDIM // 2) ** -0.5),
        'sub_k2': nrm(ks[26], (L, P_NKEYS, P_QDIM // 2), (P_QDIM // 2) ** -0.5),
        'peer_u': nrm(ks[27], (L, P_EXPERTS, D_MODEL), D_MODEL ** -0.5),
        'peer_v': nrm(ks[28], (L, P_EXPERTS, D_MODEL), 0.3),
        'g_final': gain(ks[29], (D_MODEL,)),
    }


def reference(x_prompt, x_sample, cache_moba_kv, cache_nsa_kv, state_nsa_win, cache_mem_kv, page_table, mem_prompt,
              g_attn, w_in, pe_cmp, w_ck1, w_ck2, w_cv1, w_cv2, p_a, p_b, w_o, g_cross, g_mem, w_cq, w_ckv, w_co,
              g_ffn, w_pq, sub_k1, sub_k2, peer_u, peer_v, g_final):
    h_p, h_s = x_prompt, x_sample
    moba_p, moba_s, nsa_p, nsa_s, win_p, win_s, mem_p = [], [], [], [], [], [], []
    for layer in range(DEPTH):
        h_p, a_kv, b_kv, b_win, m_kv = prompt_layer(
            h_p, mem_prompt, g_attn[layer], w_in[layer], pe_cmp[layer], w_ck1[layer], w_ck2[layer], w_cv1[layer],
            w_cv2[layer], p_a[layer], p_b[layer], w_o[layer], g_cross[layer], g_mem[layer], w_cq[layer],
            w_ckv[layer], w_co[layer], g_ffn[layer], w_pq[layer], sub_k1[layer], sub_k2[layer], peer_u[layer],
            peer_v[layer])
        moba_p.append(a_kv)
        nsa_p.append(b_kv)
        win_p.append(b_win)
        mem_p.append(m_kv)
        h_s, a_kv, b_kv, b_win = sample_layer(
            h_s, cache_moba_kv[layer], cache_nsa_kv[layer], state_nsa_win[layer], cache_mem_kv[layer], page_table,
            g_attn[layer], w_in[layer], pe_cmp[layer], w_ck1[layer], w_ck2[layer], w_cv1[layer], w_cv2[layer],
            p_a[layer], p_b[layer], w_o[layer], g_cross[layer], w_cq[layer], w_co[layer], g_ffn[layer],
            w_pq[layer], sub_k1[layer], sub_k2[layer], peer_u[layer], peer_v[layer])
        moba_s.append(a_kv)
        nsa_s.append(b_kv)
        win_s.append(b_win)
    y_prompt = rms_norm(h_p, g_final)
    y_sample = rms_norm(h_s, g_final)
    return (y_prompt, y_sample, jnp.stack(moba_p), jnp.stack(moba_s), jnp.stack(nsa_p), jnp.stack(nsa_s),
            jnp.stack(win_p), jnp.stack(win_s), jnp.stack(mem_p))
```

```python
import functools

import numpy as np
import jax
import jax.numpy as jnp
from jax import lax
from jax.experimental import pallas as pl
from jax.experimental.pallas import tpu as pltpu

F32 = jnp.float32
BF16 = jnp.bfloat16
HIGHEST = lax.Precision.HIGHEST

LANES = 128
SUBLANES = 8
VMEM_LIMIT_BYTES = 56 * 1024 * 1024

D_MODEL = 1024
HEAD_DIM = 64
ROPE_THETA = 10000.0
NORM_EPS = 1e-6
NEG_BIG = -1e30
PAGE_SIZE = 128

A_HEADS = 8
A_BLOCK = 256
A_TOPK = 3
A_WIDTH = A_HEADS * HEAD_DIM

B_HEADS = 8
B_KV_HEADS = 2
B_GROUP = B_HEADS // B_KV_HEADS
B_WIDTH = B_HEADS * HEAD_DIM
B_KV_WIDTH = B_KV_HEADS * HEAD_DIM
CMP_LEN = 32
CMP_STRIDE = 16
CMP_HIDDEN = 64
SEL_BLOCK = 64
SEL_TOPN = 16
WINDOW = 512
FORCE_SCORE = 1e4

C_HEADS = 4
C_HEAD_DIM = 128
C_WIDTH = C_HEADS * C_HEAD_DIM

P_HEADS = 8
P_NKEYS = 128
P_QDIM = 256
P_TOPK = 16
P_PICKS = P_HEADS * P_TOPK

Q_BLOCK = 256
N_MAIN = 3 * A_WIDTH + B_WIDTH + 6 * B_KV_WIDTH
N_GATES = 3 * B_HEADS
N_PROJ = N_MAIN + LANES + 2 * D_MODEL


def _params(semantics):
    return pltpu.CompilerParams(dimension_semantics=semantics, vmem_limit_bytes=VMEM_LIMIT_BYTES)


def _lane_iota(shape, dtype=jnp.int32):
    return lax.broadcasted_iota(dtype, shape, len(shape) - 1)


def _row_iota(shape, dtype=jnp.int32):
    return lax.broadcasted_iota(dtype, shape, len(shape) - 2)


def _rms(x, g):
    return x * lax.rsqrt(jnp.mean(x * x, axis=-1, keepdims=True) + NORM_EPS) * g


def _sigmoid(x):
    return 1.0 / (1.0 + jnp.exp(-x))


def _dot_t(a, b, precision=None):
    return lax.dot_general(a, b, (((1,), (1,)), ((), ())), precision=precision, preferred_element_type=F32)


def _topk_mask(x, k, lane_f):
    sel = jnp.zeros(x.shape, F32)
    for _ in range(k):
        mx = jnp.max(x, axis=-1, keepdims=True)
        first = jnp.min(jnp.where(x == mx, lane_f, float(x.shape[-1])), axis=-1, keepdims=True)
        hit = lane_f == first
        sel = jnp.where(hit, 1.0, sel)
        x = jnp.where(hit, -jnp.inf, x)
    return sel


def _softmax_rows(s, mask):
    s = jnp.where(mask, s, NEG_BIG)
    m = jnp.max(s, axis=-1, keepdims=True)
    p = jnp.where(mask, jnp.exp(s - m), 0.0)
    return p, jnp.sum(p, axis=-1, keepdims=True)


def _safe_inv(l):
    return jnp.where(l > 0.0, 1.0 / jnp.where(l > 0.0, l, 1.0), 0.0)


def _proj_kernel(x_ref, g_ref, w_ref, cs_ref, sn_ref, aq_ref, akv_ref, bq_ref, bqr_ref, bkv_ref, bwin_ref,
                 gates_ref, ga_ref, gb_ref):
    ub = _rms(x_ref[...], g_ref[...]).astype(BF16)
    cs = cs_ref[...]
    sn = sn_ref[...]
    first_half = (_lane_iota((1, LANES)) % HEAD_DIM) < (HEAD_DIM // 2)

    def cols(c0, n):
        return jnp.dot(ub, w_ref[:, c0:c0 + n], preferred_element_type=F32)

    def rot(p):
        swapped = jnp.where(first_half, pltpu.roll(p, LANES - HEAD_DIM // 2, 1), pltpu.roll(p, HEAD_DIM // 2, 1))
        return p * cs + swapped * sn

    def rot_wide(p):
        return jnp.concatenate([rot(p[:, c:c + LANES]) for c in range(0, p.shape[1], LANES)], axis=1)

    aq_ref[...] = rot_wide(cols(0, A_WIDTH))
    akv_ref[:, 0:A_WIDTH] = rot_wide(cols(A_WIDTH, A_WIDTH))
    akv_ref[:, A_WIDTH:2 * A_WIDTH] = cols(2 * A_WIDTH, A_WIDTH)
    bq = cols(3 * A_WIDTH, B_WIDTH)
    bq_ref[...] = bq
    bqr_ref[...] = rot_wide(bq)
    c0 = 3 * A_WIDTH + B_WIDTH
    bkv = cols(c0, 4 * B_KV_WIDTH)
    bkv_ref[:, 0:2 * LANES] = bkv[:, 0:2 * LANES]
    bkv_ref[:, 2 * LANES:3 * LANES] = rot(bkv[:, 2 * LANES:3 * LANES])
    bkv_ref[:, 3 * LANES:4 * LANES] = bkv[:, 3 * LANES:4 * LANES]
    bwin = cols(c0 + 4 * B_KV_WIDTH, 2 * B_KV_WIDTH)
    bwin_ref[:, 0:LANES] = rot(bwin[:, 0:LANES])
    bwin_ref[:, LANES:2 * LANES] = bwin[:, LANES:2 * LANES]
    gates_ref[...] = cols(N_MAIN, LANES)
    ga_ref[...] = cols(N_MAIN + LANES, D_MODEL)
    gb_ref[...] = cols(N_MAIN + LANES + D_MODEL, D_MODEL)


def _rope_tables(pos):
    half = HEAD_DIM // 2
    inv_freq = ROPE_THETA ** (-jnp.arange(half, dtype=F32) / half)
    ang = pos.astype(F32)[:, None] * inv_freq[None, :]
    cos, sin = jnp.cos(ang), jnp.sin(ang)
    reps = LANES // HEAD_DIM
    return jnp.tile(jnp.concatenate([cos, cos], axis=1), (1, reps)), jnp.tile(jnp.concatenate([-sin, sin], axis=1), (1, reps))


def _projection(x2d, pos, seq, g_attn, w_proj):
    n = x2d.shape[0]
    tm = min(Q_BLOCK, n)
    cs, sn = _rope_tables(pos)
    if seq >= tm:
        tab_map = lambda i: (i % (seq // tm), 0)
    else:
        cs, sn = jnp.tile(cs, (tm // seq, 1)), jnp.tile(sn, (tm // seq, 1))
        tab_map = lambda i: (0, 0)
    widths = (A_WIDTH, 2 * A_WIDTH, B_WIDTH, B_WIDTH, 4 * B_KV_WIDTH, 2 * B_KV_WIDTH, LANES, D_MODEL, D_MODEL)
    row = lambda i: (i, 0)
    fixed = lambda i: (0, 0)
    return pl.pallas_call(
        _proj_kernel,
        grid=(n // tm,),
        in_specs=[pl.BlockSpec((tm, D_MODEL), row), pl.BlockSpec((1, D_MODEL), fixed),
                  pl.BlockSpec((D_MODEL, N_PROJ), fixed), pl.BlockSpec((tm, LANES), tab_map),
                  pl.BlockSpec((tm, LANES), tab_map)],
        out_specs=[pl.BlockSpec((tm, w), row) for w in widths],
        out_shape=[jax.ShapeDtypeStruct((n, w), F32) for w in widths],
        compiler_params=_params(("parallel",)),
    )(x2d, g_attn.reshape(1, D_MODEL), w_proj, cs, sn)


def _rms_matmul_kernel(x_ref, g_ref, w_ref, o_ref):
    o_ref[...] = jnp.dot(_rms(x_ref[...], g_ref[...]).astype(BF16), w_ref[...], preferred_element_type=F32)


def _rms_matmul(x2d, g, w_bf16):
    n, d = x2d.shape
    m = w_bf16.shape[1]
    tm = min(Q_BLOCK, n)
    return pl.pallas_call(
        _rms_matmul_kernel,
        grid=(n // tm,),
        in_specs=[pl.BlockSpec((tm, d), lambda i: (i, 0)), pl.BlockSpec((1, d), lambda i: (0, 0)),
                  pl.BlockSpec((d, m), lambda i: (0, 0))],
        out_specs=pl.BlockSpec((tm, m), lambda i: (i, 0)),
        out_shape=jax.ShapeDtypeStruct((n, m), F32),
        compiler_params=_params(("parallel",)),
    )(x2d, g.reshape(1, d), w_bf16)


def _moba_prompt_kernel(q_ref, k_ref, v_ref, o_ref, *, seq):
    nb = seq // A_BLOCK
    k = k_ref[0]
    kb = k.astype(BF16)
    vb = v_ref[0].astype(BF16)
    lane = _lane_iota((1, LANES))
    lane_f = lane.astype(F32)
    kmean = jnp.concatenate(
        [jnp.mean(k[j * A_BLOCK:(j + 1) * A_BLOCK], axis=0, keepdims=True) for j in range(nb)]
        + [jnp.zeros((LANES - nb, LANES), F32)], axis=0)
    expand = jnp.where(_row_iota((LANES, seq)) == _lane_iota((LANES, seq)) // A_BLOCK, 1.0, 0.0).astype(BF16)
    scale = HEAD_DIM ** -0.5
    for i in range(nb):
        qi = q_ref[0, i * Q_BLOCK:(i + 1) * Q_BLOCK, :]
        outs = []
        for hh in range(LANES // HEAD_DIM):
            head = (lane // HEAD_DIM) == hh
            qh = jnp.where(head, qi, 0.0)
            gate = _dot_t(qh, kmean, HIGHEST)
            gate = jnp.where(lane < i, gate, -jnp.inf)
            sel = _topk_mask(gate, min(A_TOPK, nb), lane_f)
            sel = jnp.where(lane < i, sel, 0.0)
            s = _dot_t((qh * scale).astype(BF16), kb[0:(i + 1) * A_BLOCK])
            chosen = jnp.dot(sel.astype(BF16), expand[:, 0:(i + 1) * A_BLOCK], preferred_element_type=F32) > 0.5
            key = _lane_iota((Q_BLOCK, (i + 1) * A_BLOCK)) - i * A_BLOCK
            own = (key >= 0) & (key <= _row_iota((Q_BLOCK, (i + 1) * A_BLOCK)))
            p, l = _softmax_rows(s, chosen | own)
            o = jnp.dot(p.astype(BF16), vb[0:(i + 1) * A_BLOCK], preferred_element_type=F32)
            outs.append(o * _safe_inv(l))
        o_ref[0, i * Q_BLOCK:(i + 1) * Q_BLOCK, :] = jnp.where((lane // HEAD_DIM) == 0, outs[0], outs[1])


def _moba_prompt(aq, akv):
    b, seq, _ = aq.shape
    hp = A_WIDTH // LANES
    return pl.pallas_call(
        functools.partial(_moba_prompt_kernel, seq=seq),
        grid=(b, hp),
        in_specs=[pl.BlockSpec((1, seq, LANES), lambda i, j: (i, 0, j)),
                  pl.BlockSpec((1, seq, LANES), lambda i, j: (i, 0, j)),
                  pl.BlockSpec((1, seq, LANES), lambda i, j: (i, 0, hp + j))],
        out_specs=pl.BlockSpec((1, seq, LANES), lambda i, j: (i, 0, j)),
        out_shape=jax.ShapeDtypeStruct((b, seq, A_WIDTH), F32),
        compiler_params=_params(("parallel", "parallel")),
    )(aq, akv, akv)


def _compress_rows(xk_ref, xv_ref, n_rows, pe_ref, wa_ref, wb_ref, w2_ref):
    acc_a = jnp.zeros((n_rows, 2 * LANES), F32)
    acc_b = jnp.zeros((n_rows, 2 * LANES), F32)
    for tt in range(CMP_STRIDE):
        rows_tt = pl.ds(tt, n_rows, stride=CMP_STRIDE)
        xt = jnp.concatenate([xk_ref[rows_tt, :], xv_ref[rows_tt, :]], axis=1)
        acc_a += jnp.dot((xt + pe_ref[tt:tt + 1, :]).astype(BF16), wa_ref[tt], preferred_element_type=F32)
        acc_b += jnp.dot((xt + pe_ref[CMP_STRIDE + tt:CMP_STRIDE + tt + 1, :]).astype(BF16), wb_ref[tt],
                         preferred_element_type=F32)
    hidden = acc_a + pltpu.roll(acc_b, n_rows - 1, 0)
    out = jnp.dot(jax.nn.gelu(hidden).astype(BF16), w2_ref[...], preferred_element_type=F32)
    return jnp.where(_row_iota(out.shape) < n_rows - 1, out, 0.0)


def _compress_prompt_kernel(xk_ref, xv_ref, pe_ref, wa_ref, wb_ref, w2_ref, o_ref, *, seq):
    o_ref[0] = _compress_rows(xk_ref.at[0], xv_ref.at[0], seq // CMP_STRIDE, pe_ref, wa_ref, wb_ref, w2_ref)


def _compress_weights(pe_cmp, w_ck1, w_ck2, w_cv1, w_cv2):
    def diag(mk, mv):
        z = jnp.zeros_like(mk)
        rows = [[mk, z, z, z], [z, mk, z, z], [z, z, mv, z], [z, z, z, mv]]
        return jnp.concatenate([jnp.concatenate(r, axis=-1) for r in rows], axis=-2)

    k1 = w_ck1.reshape(CMP_LEN, HEAD_DIM, CMP_HIDDEN)
    v1 = w_cv1.reshape(CMP_LEN, HEAD_DIM, CMP_HIDDEN)
    w1 = diag(k1, v1).astype(BF16)
    return jnp.tile(pe_cmp, (1, 4)), w1[:CMP_STRIDE], w1[CMP_STRIDE:], diag(w_ck2, w_cv2).astype(BF16)


def _compress_prompt(bkv, cw):
    b, seq, _ = bkv.shape
    n_rows = seq // CMP_STRIDE
    pe, wa, wb, w2 = cw
    full = lambda *s: pl.BlockSpec(s, lambda i: (0,) * len(s))
    return pl.pallas_call(
        functools.partial(_compress_prompt_kernel, seq=seq),
        grid=(b,),
        in_specs=[pl.BlockSpec((1, seq, LANES), lambda i: (i, 0, 0)), pl.BlockSpec((1, seq, LANES), lambda i: (i, 0, 1)),
                  full(CMP_LEN, 2 * LANES), full(CMP_STRIDE, 2 * LANES, 2 * LANES),
                  full(CMP_STRIDE, 2 * LANES, 2 * LANES), full(2 * LANES, 2 * LANES)],
        out_specs=pl.BlockSpec((1, n_rows, 2 * LANES), lambda i: (i, 0, 0)),
        out_shape=jax.ShapeDtypeStruct((b, n_rows, 2 * LANES), F32),
        compiler_params=_params(("parallel",)),
    )(bkv, bkv, pe, wa, wb, w2)


def _overlap_matrix(n_cmp, n_sel, rows, cols):
    c_start = np.arange(n_cmp)[:, None] * CMP_STRIDE
    s_start = np.arange(n_sel)[None, :] * SEL_BLOCK
    ov = np.clip(np.minimum(c_start + CMP_LEN, s_start + SEL_BLOCK) - np.maximum(c_start, s_start), 0, None)
    out = np.zeros((rows, cols), np.float32)
    out[:n_cmp, :n_sel] = ov / CMP_STRIDE
    return jnp.asarray(out)


def _nsa_prompt_kernel(q_ref, qr_ref, gates_ref, ksel_ref, vsel_ref, kwin_ref, vwin_ref, cmp_ref, ov_ref, o_ref,
                       sel_ref, m_ref, l_ref, acc_ref, *, seq):
    g = pl.program_id(1)
    i = pl.program_id(2)
    n_cmp = (seq - CMP_LEN) // CMP_STRIDE + 1
    rows = B_GROUP * Q_BLOCK
    scale = HEAD_DIM ** -0.5
    lane = _lane_iota((1, LANES))
    lane_f = lane.astype(F32)
    in_group = (lane // HEAD_DIM) == g

    def both_halves(x):
        xg = jnp.where(in_group, x, 0.0)
        return xg + pltpu.roll(xg, HEAD_DIM, 1)

    def stack_heads(ref):
        parts = []
        for j in range(B_GROUP):
            x = ref[0, pl.ds(pl.multiple_of(i * Q_BLOCK, Q_BLOCK), Q_BLOCK), (j // 2) * LANES:(j // 2 + 1) * LANES]
            parts.append(jnp.where((lane // HEAD_DIM) == (j % 2), x, 0.0) * scale)
        return jnp.concatenate(parts, axis=0).astype(BF16)

    q4 = stack_heads(q_ref)
    qr4 = stack_heads(qr_ref)
    qpos = i * Q_BLOCK + _row_iota((Q_BLOCK, 1))

    kc = both_halves(cmp_ref[0, :, 0:LANES]).astype(BF16)
    vc = both_halves(cmp_ref[0, :, LANES:2 * LANES]).astype(BF16)
    n_tok = kc.shape[0]
    tok = _lane_iota((1, n_tok))
    cmask = (tok < n_cmp) & (tok * CMP_STRIDE + (CMP_LEN - 1) <= qpos)
    s = _dot_t(q4, kc).reshape(B_GROUP, Q_BLOCK, n_tok)
    p, l = _softmax_rows(s, cmask[None])
    p_cmp = p * _safe_inv(l)
    o_cmp = jnp.dot(p_cmp.reshape(rows, n_tok).astype(BF16), vc, preferred_element_type=F32)
    imp = jnp.dot(jnp.sum(p_cmp, axis=0), ov_ref[...], precision=HIGHEST, preferred_element_type=F32)
    cur = qpos // SEL_BLOCK
    forced = (lane == 0) | (lane == cur) | (lane == cur - 1)
    score = jnp.where(lane <= cur, jnp.where(forced, FORCE_SCORE, imp), -jnp.inf)
    n_sel = -(-seq // SEL_BLOCK)
    sel = _topk_mask(score, min(SEL_TOPN, n_sel), lane_f)
    sel_ref[...] = jnp.where(lane <= cur, sel, 0.0)

    m_ref[...] = jnp.full(m_ref.shape, NEG_BIG, F32)
    l_ref[...] = jnp.zeros(l_ref.shape, F32)
    acc_ref[...] = jnp.zeros(acc_ref.shape, F32)
    blocks_per_tile = Q_BLOCK // SEL_BLOCK

    def sel_step(j, carry):
        start = pl.multiple_of(j * Q_BLOCK, Q_BLOCK)
        kj = both_halves(ksel_ref[0, pl.ds(start, Q_BLOCK), :]).astype(BF16)
        vj = both_halves(vsel_ref[0, pl.ds(start, Q_BLOCK), :]).astype(BF16)
        kpos = j * Q_BLOCK + _lane_iota((1, Q_BLOCK))
        blk = j * blocks_per_tile + _lane_iota((LANES, Q_BLOCK)) // SEL_BLOCK
        expand = jnp.where(_row_iota((LANES, Q_BLOCK)) == blk, 1.0, 0.0).astype(BF16)
        chosen = jnp.dot(sel_ref[...].astype(BF16), expand, preferred_element_type=F32) > 0.5
        mask = chosen & (kpos <= qpos)
        sj = jnp.where(mask[None], _dot_t(qr4, kj).reshape(B_GROUP, Q_BLOCK, Q_BLOCK), NEG_BIG)
        m_old = m_ref[...]
        m_new = jnp.maximum(m_old, jnp.max(sj, axis=-1, keepdims=True))
        pj = jnp.where(mask[None], jnp.exp(sj - m_new), 0.0)
        alpha = jnp.exp(m_old - m_new)
        l_ref[...] = alpha * l_ref[...] + jnp.sum(pj, axis=-1, keepdims=True)
        pv = jnp.dot(pj.reshape(rows, Q_BLOCK).astype(BF16), vj, preferred_element_type=F32)
        acc_ref[...] = alpha * acc_ref[...] + pv.reshape(B_GROUP, Q_BLOCK, LANES)
        m_ref[...] = m_new
        return carry

    lax.fori_loop(0, i + 1, sel_step, 0)
    o_sel = (acc_ref[...] * _safe_inv(l_ref[...])).reshape(rows, LANES)

    span = WINDOW + Q_BLOCK
    w0 = jnp.maximum(i * Q_BLOCK - WINDOW, 0)
    start = pl.multiple_of(w0, Q_BLOCK)
    kw = both_halves(kwin_ref[0, pl.ds(start, span), :]).astype(BF16)
    vw = both_halves(vwin_ref[0, pl.ds(start, span), :]).astype(BF16)
    kpos = w0 + _lane_iota((1, span))
    wmask = (kpos <= qpos) & (kpos > qpos - WINDOW)
    p, l = _softmax_rows(_dot_t(qr4, kw).reshape(B_GROUP, Q_BLOCK, span), wmask[None])
    o_win = jnp.dot(p.reshape(rows, span).astype(BF16), vw, preferred_element_type=F32)
    o_win = o_win * _safe_inv(l).reshape(rows, 1)

    gates = gates_ref[0]
    outs = []
    for j in range(B_GROUP):
        r = slice(j * Q_BLOCK, (j + 1) * Q_BLOCK)
        gt = [_sigmoid(jnp.sum(jnp.where(lane == (g * B_GROUP + j) * 3 + c, gates, 0.0), axis=-1, keepdims=True))
              for c in range(3)]
        outs.append(gt[0] * o_cmp[r] + gt[1] * o_sel[r] + gt[2] * o_win[r])
    left = (lane // HEAD_DIM) == 0
    o_ref[0] = jnp.concatenate([jnp.where(left, outs[0], outs[1]), jnp.where(left, outs[2], outs[3])], axis=1)


def _nsa_prompt(bq, bqr, gates, bkv, bwin, cmp_tok):
    b, seq, _ = bq.shape
    nq = seq // Q_BLOCK
    n_tok = cmp_tok.shape[1]
    n_cmp = (seq - CMP_LEN) // CMP_STRIDE + 1
    n_sel = -(-seq // SEL_BLOCK)
    assert n_sel <= LANES and seq >= WINDOW + Q_BLOCK
    overlap = _overlap_matrix(n_cmp, n_sel, n_tok, LANES)
    qspec = pl.BlockSpec((1, seq, 2 * LANES), lambda i, g, t: (i, 0, g))
    col = lambda c: pl.BlockSpec((1, seq, LANES), lambda i, g, t: (i, 0, c))
    return pl.pallas_call(
        functools.partial(_nsa_prompt_kernel, seq=seq),
        grid=(b, B_KV_HEADS, nq),
        in_specs=[qspec, qspec, pl.BlockSpec((1, Q_BLOCK, LANES), lambda i, g, t: (i, t, 0)),
                  col(2), col(3), col(0), col(1),
                  pl.BlockSpec((1, n_tok, 2 * LANES), lambda i, g, t: (i, 0, 0)),
                  pl.BlockSpec((n_tok, LANES), lambda i, g, t: (0, 0))],
        out_specs=pl.BlockSpec((1, Q_BLOCK, 2 * LANES), lambda i, g, t: (i, t, g)),
        out_shape=jax.ShapeDtypeStruct((b, seq, B_WIDTH), F32),
        scratch_shapes=[pltpu.VMEM((Q_BLOCK, LANES), F32), pltpu.VMEM((B_GROUP, Q_BLOCK, 1), F32),
                        pltpu.VMEM((B_GROUP, Q_BLOCK, 1), F32), pltpu.VMEM((B_GROUP, Q_BLOCK, LANES), F32)],
        compiler_params=_params(("parallel", "parallel", "arbitrary")),
    )(bq, bqr, gates, bkv, bkv, bwin, bwin, cmp_tok, overlap)


def _merge_kernel(x_ref, oa_ref, ob_ref, ga_ref, gb_ref, pa_ref, pb_ref, wo_ref, h_ref):
    ya = jnp.dot(oa_ref[...].astype(BF16), pa_ref[...], preferred_element_type=F32)
    yb = jnp.dot(ob_ref[...].astype(BF16), pb_ref[...], preferred_element_type=F32)
    mixed = _sigmoid(ga_ref[...]) * ya + _sigmoid(gb_ref[...]) * yb
    h_ref[...] = x_ref[...] + jnp.dot(mixed.astype(BF16), wo_ref[...], preferred_element_type=F32)


def _merge(x2d, oa, ob, ga, gb, pa, pb, wo):
    n = x2d.shape[0]
    tm = min(Q_BLOCK, n)
    row = lambda w: pl.BlockSpec((tm, w), lambda i: (i, 0))
    full = lambda a: pl.BlockSpec(a.shape, lambda i: (0, 0))
    return pl.pallas_call(
        _merge_kernel,
        grid=(n // tm,),
        in_specs=[row(D_MODEL), row(A_WIDTH), row(B_WIDTH), row(D_MODEL), row(D_MODEL), full(pa), full(pb), full(wo)],
        out_specs=row(D_MODEL),
        out_shape=jax.ShapeDtypeStruct((n, D_MODEL), F32),
        compiler_params=_params(("parallel",)),
    )(x2d, oa, ob, ga, gb, pa, pb, wo)


def _cross_kernel(h_ref, mem_ref, g_ref, wq_ref, wo_ref, o_ref):
    h = h_ref[0]
    q = jnp.dot(_rms(h, g_ref[...]).astype(BF16), wq_ref[...], preferred_element_type=F32)
    scale = C_HEAD_DIM ** -0.5
    outs = []
    for hd in range(C_HEADS):
        c = slice(hd * C_HEAD_DIM, (hd + 1) * C_HEAD_DIM)
        kh = mem_ref[0, :, c].astype(BF16)
        vh = mem_ref[0, :, C_WIDTH + hd * C_HEAD_DIM:C_WIDTH + (hd + 1) * C_HEAD_DIM].astype(BF16)
        s = _dot_t(q[:, c].astype(BF16), kh) * scale
        m = jnp.max(s, axis=-1, keepdims=True)
        p = jnp.exp(s - m)
        o = jnp.dot(p.astype(BF16), vh, preferred_element_type=F32)
        outs.append(o / jnp.sum(p, axis=-1, keepdims=True))
    att = jnp.concatenate(outs, axis=1).astype(BF16)
    o_ref[0] = h + jnp.dot(att, wo_ref[...], preferred_element_type=F32)


def _cross(h3, mem_kv, g_cross, wq, wo):
    b, t, _ = h3.shape
    tq = min(Q_BLOCK, t)
    mlen = mem_kv.shape[1]
    return pl.pallas_call(
        _cross_kernel,
        grid=(b, t // tq),
        in_specs=[pl.BlockSpec((1, tq, D_MODEL), lambda i, j: (i, j, 0)),
                  pl.BlockSpec((1, mlen, 2 * C_WIDTH), lambda i, j: (i, 0, 0)),
                  pl.BlockSpec((1, D_MODEL), lambda i, j: (0, 0)),
                  pl.BlockSpec(wq.shape, lambda i, j: (0, 0)), pl.BlockSpec(wo.shape, lambda i, j: (0, 0))],
        out_specs=pl.BlockSpec((1, tq, D_MODEL), lambda i, j: (i, j, 0)),
        out_shape=jax.ShapeDtypeStruct((b, t, D_MODEL), F32),
        compiler_params=_params(("parallel", "parallel")),
    )(h3, mem_kv, g_cross.reshape(1, D_MODEL), wq, wo)


PEER_ROWS = 32


def _peer_select_kernel(h_ref, g_ref, wq_ref, k1_ref, k2_ref, z_ref, eid_ref, gate_ref, s_ref, e_scr, w_scr, *, tm):
    z = _rms(h_ref[...], g_ref[...])
    z_ref[...] = z
    zb = z.astype(BF16)
    half = P_QDIM // 2
    k1 = k1_ref[...]
    k2 = k2_ref[...]
    for hd in range(P_HEADS):
        q = jnp.dot(zb, wq_ref[:, hd * P_QDIM:(hd + 1) * P_QDIM], preferred_element_type=F32)
        s_ref[2 * hd] = _dot_t(q[:, 0:half], k1, HIGHEST)
        s_ref[2 * hd + 1] = _dot_t(q[:, half:P_QDIM], k2, HIGHEST)

    lane = _lane_iota((1, LANES))
    lane_f = lane.astype(F32)
    lane2 = _lane_iota((1, P_TOPK * P_TOPK))
    lane2_f = lane2.astype(F32)
    chunks = tm // PEER_ROWS

    def step(it, carry):
        hd = it // chunks
        r0 = pl.multiple_of((it % chunks) * PEER_ROWS, PEER_ROWS)
        x1 = s_ref[2 * hd, pl.ds(r0, PEER_ROWS), :]
        x2 = s_ref[2 * hd + 1, pl.ds(r0, PEER_ROWS), :]
        shape2 = (PEER_ROWS, P_TOPK * P_TOPK)
        v1e = jnp.zeros(shape2, F32)
        i1e = jnp.zeros(shape2, F32)
        v2e = jnp.zeros(shape2, F32)
        i2e = jnp.zeros(shape2, F32)
        for k in range(P_TOPK):
            m1 = jnp.max(x1, axis=-1, keepdims=True)
            a1 = jnp.min(jnp.where(x1 == m1, lane_f, float(LANES)), axis=-1, keepdims=True)
            x1 = jnp.where(lane_f == a1, -jnp.inf, x1)
            m2 = jnp.max(x2, axis=-1, keepdims=True)
            a2 = jnp.min(jnp.where(x2 == m2, lane_f, float(LANES)), axis=-1, keepdims=True)
            x2 = jnp.where(lane_f == a2, -jnp.inf, x2)
            row_k = (lane2 // P_TOPK) == k
            col_k = (lane2 % P_TOPK) == k
            v1e = jnp.where(row_k, m1, v1e)
            i1e = jnp.where(row_k, a1, i1e)
            v2e = jnp.where(col_k, m2, v2e)
            i2e = jnp.where(col_k, a2, i2e)
        cand = v1e + v2e
        cid = i1e * float(P_NKEYS) + i2e
        e16 = jnp.zeros((PEER_ROWS, LANES), F32)
        s16 = jnp.zeros((PEER_ROWS, LANES), F32)
        top = None
        for k in range(P_TOPK):
            mx = jnp.max(cand, axis=-1, keepdims=True)
            pos = jnp.min(jnp.where(cand == mx, lane2_f, float(P_TOPK * P_TOPK)), axis=-1, keepdims=True)
            hit = lane2_f == pos
            e = jnp.max(jnp.where(hit, cid, -1.0), axis=-1, keepdims=True)
            cand = jnp.where(hit, -jnp.inf, cand)
            slot = (lane % P_TOPK) == k
            e16 = jnp.where(slot, e, e16)
            s16 = jnp.where(slot, mx, s16)
            if k == 0:
                top = mx
        ex = jnp.exp(s16 - top)
        denom = jnp.sum(jnp.where(lane < P_TOPK, ex, 0.0), axis=-1, keepdims=True)
        e_scr[hd, pl.ds(r0, PEER_ROWS), :] = e16
        w_scr[hd, pl.ds(r0, PEER_ROWS), :] = ex / denom
        return carry

    lax.fori_loop(0, P_HEADS * chunks, step, 0)
    eid = jnp.zeros((tm, LANES), F32)
    gate = jnp.zeros((tm, LANES), F32)
    for hd in range(P_HEADS):
        mine = (lane // P_TOPK) == hd
        eid = jnp.where(mine, e_scr[hd], eid)
        gate = jnp.where(mine, w_scr[hd], gate)
    eid_ref[...] = eid.astype(jnp.int32)
    gate_ref[...] = gate


def _peer_select(h2d, g_ffn, wq, k1, k2):
    n = h2d.shape[0]
    tm = min(Q_BLOCK, n)
    assert tm % PEER_ROWS == 0 and n % tm == 0
    row = lambda: pl.BlockSpec((tm, D_MODEL), lambda i: (i, 0))
    pick = lambda: pl.BlockSpec((tm, LANES), lambda i: (i, 0))
    full = lambda a: pl.BlockSpec(a.shape, lambda i: (0, 0))
    return pl.pallas_call(
        functools.partial(_peer_select_kernel, tm=tm),
        grid=(n // tm,),
        in_specs=[row(), pl.BlockSpec((1, D_MODEL), lambda i: (0, 0)), full(wq), full(k1), full(k2)],
        out_specs=[row(), pick(), pick()],
        out_shape=[jax.ShapeDtypeStruct((n, D_MODEL), F32), jax.ShapeDtypeStruct((n, LANES), jnp.int32),
                   jax.ShapeDtypeStruct((n, LANES), F32)],
        scratch_shapes=[pltpu.VMEM((2 * P_HEADS, tm, LANES), F32), pltpu.VMEM((P_HEADS, tm, LANES), F32),
                        pltpu.VMEM((P_HEADS, tm, LANES), F32)],
        compiler_params=_params(("parallel",)),
    )(h2d, g_ffn.reshape(1, D_MODEL), wq, k1, k2)


PEER_TOKENS = 8
ROW_TILES = D_MODEL // LANES


def _peer_gather_kernel(eid_ref, eid_next_ref, h_ref, z_ref, gate_ref, gfin_ref, uv_hbm, y_ref, buf, sem):
    i = pl.program_id(0)
    n_steps = pl.num_programs(0)
    slot = i % 2
    rows = PEER_TOKENS * P_PICKS

    tiles = 2 * ROW_TILES

    def issue(ids_ref, to_slot):
        def body(r, carry):
            e = ids_ref[r // P_PICKS, r % P_PICKS]
            pltpu.make_async_copy(uv_hbm.at[pl.ds(e * tiles, tiles)], buf.at[to_slot, pl.ds(r * tiles, tiles)],
                                  sem.at[to_slot]).start()
            return carry
        lax.fori_loop(0, rows, body, 0, unroll=8)

    @pl.when(i == 0)
    def _():
        issue(eid_ref, 0)

    @pl.when(i + 1 < n_steps)
    def _():
        issue(eid_next_ref, 1 - slot)

    pltpu.make_async_copy(uv_hbm.at[pl.ds(0, rows * tiles)], buf.at[slot], sem.at[slot]).wait()

    flat = buf.at[slot]
    gate_t = jnp.transpose(jnp.concatenate(
        [gate_ref[...], jnp.zeros((LANES - PEER_TOKENS, P_PICKS), F32)], axis=0))
    for p in range(PEER_TOKENS):
        base = p * P_PICKS * 2 * ROW_TILES
        acc = jnp.zeros((P_PICKS, LANES), F32)
        for s in range(ROW_TILES):
            u_s = flat[pl.ds(base + s, P_PICKS, stride=2 * ROW_TILES), :]
            acc += u_s * z_ref[p:p + 1, s * LANES:(s + 1) * LANES]
        act = jax.nn.gelu(jnp.sum(acc, axis=-1, keepdims=True))
        coef = jnp.broadcast_to(gate_t[:, p:p + 1] * act, (P_PICKS, LANES))
        outs = []
        for s in range(ROW_TILES):
            v_s = flat[pl.ds(base + ROW_TILES + s, P_PICKS, stride=2 * ROW_TILES), :]
            outs.append(jnp.sum(coef * v_s, axis=0, keepdims=True))
        y_ref[p:p + 1, :] = h_ref[p:p + 1, :] + jnp.concatenate(outs, axis=1)
    y_ref[...] = _rms(y_ref[...], gfin_ref[...])


def _peer_gather(eid, h2d, z, gate, g_final, uv):
    n = h2d.shape[0]
    steps = n // PEER_TOKENS
    smem = lambda f: pl.BlockSpec((PEER_TOKENS, P_PICKS), f, memory_space=pltpu.SMEM)
    row = lambda w: pl.BlockSpec((PEER_TOKENS, w), lambda i: (i, 0))
    return pl.pallas_call(
        _peer_gather_kernel,
        grid=(steps,),
        in_specs=[smem(lambda i: (i, 0)), smem(lambda i: (jnp.minimum(i + 1, steps - 1), 0)),
                  row(D_MODEL), row(D_MODEL), row(P_PICKS), pl.BlockSpec((1, D_MODEL), lambda i: (0, 0)),
                  pl.BlockSpec(memory_space=pl.ANY)],
        out_specs=row(D_MODEL),
        out_shape=jax.ShapeDtypeStruct((n, D_MODEL), F32),
        scratch_shapes=[pltpu.VMEM((2, PEER_TOKENS * P_PICKS * 2 * ROW_TILES, LANES), F32),
                        pltpu.SemaphoreType.DMA((2,))],
        compiler_params=_params(("arbitrary",)),
    )(eid, eid, h2d, z, gate, g_final.reshape(1, D_MODEL), uv)


PAGES_PER_STEP = 16
KEY_CHUNK = 2048


def _page_specs(width, col_block):
    def spec(k):
        return pl.BlockSpec((1, PAGE_SIZE, width), lambda b, c, pt: (pt[b, c * PAGES_PER_STEP + k], 0, col_block))
    return [spec(k) for k in range(PAGES_PER_STEP)]


def _online_update(s, mask, v, m_ref, l_ref, acc_ref):
    s = jnp.where(mask, s, NEG_BIG)
    m_old = m_ref[...]
    m_new = jnp.maximum(m_old, jnp.max(s, axis=-1, keepdims=True))
    p = jnp.where(mask, jnp.exp(s - m_new), 0.0)
    alpha = jnp.exp(m_old - m_new)
    l_ref[...] = alpha * l_ref[...] + jnp.sum(p, axis=-1, keepdims=True)
    acc_ref[...] = alpha * acc_ref[...] + jnp.dot(p.astype(BF16), v, preferred_element_type=F32)
    m_ref[...] = m_new


def _moba_kmean_kernel(pt_ref, *refs):
    pages, o_ref = refs[:PAGES_PER_STEP], refs[PAGES_PER_STEP]
    c = pl.program_id(1)

    @pl.when(c == 0)
    def _():
        o_ref[...] = jnp.zeros(o_ref.shape, F32)

    per_block = A_BLOCK // PAGE_SIZE
    means = []
    for j in range(PAGES_PER_STEP // per_block):
        tot = sum(jnp.sum(pages[j * per_block + t][0], axis=0, keepdims=True) for t in range(per_block))
        means.append(tot / A_BLOCK)
    n = len(means)
    o_ref[0, pl.ds(pl.multiple_of(c * n, n), n), :] = jnp.concatenate(means, axis=0)


def _moba_kmean(pool, page_table, n_rows):
    b, n_pages = page_table.shape
    return pl.pallas_call(
        _moba_kmean_kernel,
        grid_spec=pltpu.PrefetchScalarGridSpec(
            num_scalar_prefetch=1, grid=(b, n_pages // PAGES_PER_STEP),
            in_specs=_page_specs(A_WIDTH, 0),
            out_specs=pl.BlockSpec((1, n_rows, A_WIDTH), lambda i, c, pt: (i, 0, 0))),
        out_shape=jax.ShapeDtypeStruct((b, n_rows, A_WIDTH), F32),
        compiler_params=_params(("parallel", "arbitrary")),
    )(page_table, *([pool] * PAGES_PER_STEP))


def _moba_sample_kernel(pt_ref, *refs, past, t_new):
    pages = refs[:PAGES_PER_STEP]
    kmean_ref, q_ref, new_ref, o_ref, sel_ref, m_ref, l_ref, acc_ref = refs[PAGES_PER_STEP:]
    c = pl.program_id(1)
    rows = A_HEADS * t_new
    n_lanes = sel_ref.shape[1]
    cur = past // A_BLOCK
    q = q_ref[0]
    qb = (q * HEAD_DIM ** -0.5).astype(BF16)

    @pl.when(c == 0)
    def _():
        lane = _lane_iota((1, n_lanes))
        gate = jnp.where(lane < cur, _dot_t(q, kmean_ref[0], HIGHEST), -jnp.inf)
        sel = _topk_mask(gate, min(A_TOPK, cur + 1), lane.astype(F32))
        sel_ref[...] = jnp.where(lane < cur, sel, 0.0)
        m_ref[...] = jnp.full(m_ref.shape, NEG_BIG, F32)
        l_ref[...] = jnp.zeros(l_ref.shape, F32)
        acc_ref[...] = jnp.zeros(acc_ref.shape, F32)

    n_keys = PAGES_PER_STEP * PAGE_SIZE
    k = jnp.concatenate([p[0, :, 0:A_WIDTH] for p in pages], axis=0).astype(BF16)
    v = jnp.concatenate([p[0, :, A_WIDTH:2 * A_WIDTH] for p in pages], axis=0).astype(BF16)
    blk = c * (n_keys // A_BLOCK) + _lane_iota((n_lanes, n_keys)) // A_BLOCK
    expand = jnp.where(_row_iota((n_lanes, n_keys)) == blk, 1.0, 0.0).astype(BF16)
    chosen = jnp.dot(sel_ref[...].astype(BF16), expand, preferred_element_type=F32) > 0.5
    _online_update(_dot_t(qb, k), chosen, v, m_ref, l_ref, acc_ref)

    @pl.when(c == pl.num_programs(1) - 1)
    def _():
        kn = new_ref[0, :, 0:A_WIDTH].astype(BF16)
        vn = new_ref[0, :, A_WIDTH:2 * A_WIDTH].astype(BF16)
        t_key = _lane_iota((rows, kn.shape[0]))
        t_row = _row_iota((rows, kn.shape[0])) % t_new
        _online_update(_dot_t(qb, kn), (t_key <= t_row) & (t_key < t_new), vn, m_ref, l_ref, acc_ref)
        own = (_lane_iota((rows, A_WIDTH)) // HEAD_DIM) == (_row_iota((rows, A_WIDTH)) // t_new)
        o_ref[0] = jnp.where(own, acc_ref[...] * _safe_inv(l_ref[...]), 0.0)


def _moba_sample(pool, page_table, kmean, q_rows, new_kv, past, t_new):
    b, n_pages = page_table.shape
    rows = A_HEADS * t_new
    per_b = lambda s: pl.BlockSpec((1,) + s, lambda i, c, pt: (i, 0, 0))
    return pl.pallas_call(
        functools.partial(_moba_sample_kernel, past=past, t_new=t_new),
        grid_spec=pltpu.PrefetchScalarGridSpec(
            num_scalar_prefetch=1, grid=(b, n_pages // PAGES_PER_STEP),
            in_specs=_page_specs(2 * A_WIDTH, 0) + [per_b(kmean.shape[1:]), per_b(q_rows.shape[1:]),
                                                     per_b(new_kv.shape[1:])],
            out_specs=per_b((rows, A_WIDTH)),
            scratch_shapes=[pltpu.VMEM((rows, kmean.shape[1]), F32), pltpu.VMEM((rows, 1), F32),
                            pltpu.VMEM((rows, 1), F32), pltpu.VMEM((rows, A_WIDTH), F32)]),
        out_shape=jax.ShapeDtypeStruct((b, rows, A_WIDTH), F32),
        compiler_params=_params(("parallel", "arbitrary")),
    )(page_table, *([pool] * PAGES_PER_STEP), kmean, q_rows, new_kv)


def _compress_sample_kernel(pt_ref, *refs, past):
    pages = refs[:PAGES_PER_STEP]
    pe_ref, wa_ref, wb_ref, w2_ref, o_ref, xk_ref, xv_ref = refs[PAGES_PER_STEP:]
    c = pl.program_id(1)
    for k, page in enumerate(pages):
        start = pl.multiple_of((c * PAGES_PER_STEP + k) * PAGE_SIZE, PAGE_SIZE)
        xk_ref[pl.ds(start, PAGE_SIZE), :] = page[0, :, 0:LANES]
        xv_ref[pl.ds(start, PAGE_SIZE), :] = page[0, :, LANES:2 * LANES]

    @pl.when(c == pl.num_programs(1) - 1)
    def _():
        o_ref[0] = _compress_rows(xk_ref, xv_ref, past // CMP_STRIDE, pe_ref, wa_ref, wb_ref, w2_ref)


def _compress_sample(pool, page_table, cw, past):
    b, n_pages = page_table.shape
    n_rows = past // CMP_STRIDE
    pe, wa, wb, w2 = cw
    full = lambda a: pl.BlockSpec(a.shape, lambda i, c, pt: (0,) * a.ndim)
    return pl.pallas_call(
        functools.partial(_compress_sample_kernel, past=past),
        grid_spec=pltpu.PrefetchScalarGridSpec(
            num_scalar_prefetch=1, grid=(b, n_pages // PAGES_PER_STEP),
            in_specs=_page_specs(2 * LANES, 0) + [full(pe), full(wa), full(wb), full(w2)],
            out_specs=pl.BlockSpec((1, n_rows, 2 * LANES), lambda i, c, pt: (i, 0, 0)),
            scratch_shapes=[pltpu.VMEM((past, LANES), F32), pltpu.VMEM((past, LANES), F32)]),
        out_shape=jax.ShapeDtypeStruct((b, n_rows, 2 * LANES), F32),
        compiler_params=_params(("parallel", "arbitrary")),
    )(page_table, *([pool] * PAGES_PER_STEP), pe, wa, wb, w2)


def _nsa_sample_kernel(pt_ref, *refs, past, t_new):
    pages = refs[:PAGES_PER_STEP]
    (cmp_ref, q_ref, qr_ref, gates_ref, newsel_ref, win_ref, newwin_ref, ov_ref, o_ref,
     kv_ref, m_ref, l_ref, acc_ref) = refs[PAGES_PER_STEP:]
    c = pl.program_id(1)
    for k, page in enumerate(pages):
        start = pl.multiple_of((c * PAGES_PER_STEP + k) * PAGE_SIZE, PAGE_SIZE)
        kv_ref[pl.ds(start, PAGE_SIZE), :] = page[0]

    @pl.when(c == pl.num_programs(1) - 1)
    def _():
        pad = newsel_ref.shape[1]
        kv_ref[past:past + PAGE_SIZE, :] = jnp.zeros((PAGE_SIZE, 2 * LANES), F32)
        kv_ref[past:past + pad, :] = newsel_ref[0]
        rows = B_HEADS * t_new
        slab = B_KV_HEADS * t_new
        scale = HEAD_DIM ** -0.5
        qb = (q_ref[0] * scale).astype(BF16)
        qrb = (qr_ref[0] * scale).astype(BF16)
        pos = past + _row_iota((rows, 1)) % t_new

        n_tok = cmp_ref.shape[1]
        n_cmp = n_tok - 1
        kc = cmp_ref[0, :, 0:LANES].astype(BF16)
        vc = cmp_ref[0, :, LANES:2 * LANES].astype(BF16)
        tok = _lane_iota((1, n_tok))
        p, l = _softmax_rows(_dot_t(qb, kc), (tok < n_cmp) & (tok * CMP_STRIDE + (CMP_LEN - 1) <= pos))
        p_cmp = p * _safe_inv(l)
        o_cmp = jnp.dot(p_cmp.astype(BF16), vc, preferred_element_type=F32)
        p_sum = sum(p_cmp[j * slab:(j + 1) * slab] for j in range(B_GROUP))
        imp = jnp.dot(p_sum, ov_ref[...], precision=HIGHEST, preferred_element_type=F32)
        n_lanes = ov_ref.shape[1]
        lane = _lane_iota((1, n_lanes))
        cur = past // SEL_BLOCK
        forced = (lane == 0) | (lane == cur) | (lane == cur - 1)
        score = jnp.where(lane <= cur, jnp.where(forced, FORCE_SCORE, imp), -jnp.inf)
        sel = _topk_mask(score, min(SEL_TOPN, cur + 1), lane.astype(F32))
        sel = jnp.where(lane <= cur, sel, 0.0).astype(BF16)

        m_ref[...] = jnp.full(m_ref.shape, NEG_BIG, F32)
        l_ref[...] = jnp.zeros(l_ref.shape, F32)
        acc_ref[...] = jnp.zeros(acc_ref.shape, F32)
        total = past + PAGE_SIZE
        for k0 in range(0, total, KEY_CHUNK):
            n_keys = min(KEY_CHUNK, total - k0)
            ks = kv_ref[k0:k0 + n_keys, 0:LANES].astype(BF16)
            vs = kv_ref[k0:k0 + n_keys, LANES:2 * LANES].astype(BF16)
            kpos = k0 + _lane_iota((1, n_keys))
            blk = (k0 + _lane_iota((n_lanes, n_keys))) // SEL_BLOCK
            expand = jnp.where(_row_iota((n_lanes, n_keys)) == blk, 1.0, 0.0).astype(BF16)
            chosen = jnp.dot(sel, expand, preferred_element_type=F32) > 0.5
            mask = jnp.concatenate([chosen] * B_GROUP, axis=0) & (kpos <= pos)
            _online_update(_dot_t(qrb, ks), mask, vs, m_ref, l_ref, acc_ref)
        o_sel = acc_ref[...] * _safe_inv(l_ref[...])

        n_win = win_ref.shape[1]
        kw = jnp.concatenate([win_ref[0, :, 0:LANES], newwin_ref[0, :, 0:LANES]], axis=0).astype(BF16)
        vw = jnp.concatenate([win_ref[0, :, LANES:2 * LANES], newwin_ref[0, :, LANES:2 * LANES]], axis=0).astype(BF16)
        wpos = past - n_win + _lane_iota((1, kw.shape[0]))
        p, l = _softmax_rows(_dot_t(qrb, kw), (wpos <= pos) & (wpos > pos - WINDOW))
        o_win = jnp.dot(p.astype(BF16), vw, preferred_element_type=F32) * _safe_inv(l)

        gt = _sigmoid(gates_ref[0])
        o = gt[:, 0:1] * o_cmp + gt[:, 1:2] * o_sel + gt[:, 2:3] * o_win
        own = (_lane_iota((rows, LANES)) // HEAD_DIM) == ((_row_iota((rows, LANES)) // t_new) % B_KV_HEADS)
        o_ref[0] = jnp.where(own, o, 0.0)


def _nsa_sample(pool, page_table, cmp_tok, q_rows, qr_rows, gate_rows, new_sel, win_state, new_win, past, t_new):
    b, n_pages = page_table.shape
    rows = B_HEADS * t_new
    n_tok = cmp_tok.shape[1]
    n_sel = past // SEL_BLOCK + 1
    n_lanes = -(-n_sel // LANES) * LANES
    overlap = _overlap_matrix(n_tok - 1, n_sel, n_tok, n_lanes)
    per_b = lambda a: pl.BlockSpec((1,) + a.shape[1:], lambda i, c, pt: (i, 0, 0))
    return pl.pallas_call(
        functools.partial(_nsa_sample_kernel, past=past, t_new=t_new),
        grid_spec=pltpu.PrefetchScalarGridSpec(
            num_scalar_prefetch=1, grid=(b, n_pages // PAGES_PER_STEP),
            in_specs=_page_specs(2 * LANES, 1) + [per_b(cmp_tok), per_b(q_rows), per_b(qr_rows), per_b(gate_rows),
                                                  per_b(new_sel), per_b(win_state), per_b(new_win),
                                                  pl.BlockSpec(overlap.shape, lambda i, c, pt: (0, 0))],
            out_specs=pl.BlockSpec((1, rows, LANES), lambda i, c, pt: (i, 0, 0)),
            scratch_shapes=[pltpu.VMEM((past + PAGE_SIZE, 2 * LANES), F32), pltpu.VMEM((rows, 1), F32),
                            pltpu.VMEM((rows, 1), F32), pltpu.VMEM((rows, LANES), F32)]),
        out_shape=jax.ShapeDtypeStruct((b, rows, LANES), F32),
        compiler_params=_params(("parallel", "arbitrary")),
    )(page_table, *([pool] * PAGES_PER_STEP), cmp_tok, q_rows, qr_rows, gate_rows, new_sel, win_state, new_win,
      overlap)


def _layer_weights(w_in, pe_cmp, w_ck1, w_ck2, w_cv1, w_cv2, p_a, p_b, w_o, w_cq, w_ckv, w_co, w_pq, peer_u, peer_v):
    w_proj = jnp.concatenate(
        [w_in[:, :N_MAIN], jnp.pad(w_in[:, N_MAIN:N_MAIN + N_GATES], ((0, 0), (0, LANES - N_GATES))),
         w_in[:, N_MAIN + N_GATES:]], axis=1).astype(BF16)
    n_exp = peer_u.shape[0]
    uv = jnp.concatenate([peer_u.reshape(n_exp, ROW_TILES, LANES), peer_v.reshape(n_exp, ROW_TILES, LANES)], axis=1)
    return dict(w_proj=w_proj, cw=_compress_weights(pe_cmp, w_ck1, w_ck2, w_cv1, w_cv2),
                p_a=p_a.astype(BF16), p_b=p_b.astype(BF16), w_o=w_o.astype(BF16), w_cq=w_cq.astype(BF16),
                w_ckv=w_ckv.astype(BF16), w_co=w_co.astype(BF16), w_pq=w_pq.astype(BF16),
                uv=uv.reshape(n_exp * 2 * ROW_TILES, LANES))


def _channel_and_norm(h2d, batch, mem_kv, w, g_cross, g_ffn, sub_k1, sub_k2, g_final):
    h3 = _cross(h2d.reshape(batch, -1, D_MODEL), mem_kv, g_cross, w["w_cq"], w["w_co"])
    h2 = h3.reshape(-1, D_MODEL)
    z, eid, gate = _peer_select(h2, g_ffn, w["w_pq"], sub_k1, sub_k2)
    return _peer_gather(eid, h2, z, gate, g_final, w["uv"])


def _prompt_group(x, mem, w, g_attn, g_cross, g_mem, g_ffn, sub_k1, sub_k2, g_final):
    b, seq, _ = x.shape
    x2d = x.reshape(b * seq, D_MODEL)
    aq, akv, bq, bqr, bkv, bwin, gates, ga, gb = _projection(x2d, jnp.arange(seq, dtype=jnp.int32), seq, g_attn,
                                                             w["w_proj"])
    r3 = lambda a: a.reshape(b, seq, a.shape[-1])
    oa = _moba_prompt(r3(aq), r3(akv))
    cmp_tok = _compress_prompt(r3(bkv), w["cw"])
    ob = _nsa_prompt(r3(bq), r3(bqr), r3(gates), r3(bkv), r3(bwin), cmp_tok)
    h = _merge(x2d, oa.reshape(-1, A_WIDTH), ob.reshape(-1, B_WIDTH), ga, gb, w["p_a"], w["p_b"], w["w_o"])
    mlen = mem.shape[1]
    mem_kv = _rms_matmul(mem.reshape(b * mlen, D_MODEL), g_mem, w["w_ckv"]).reshape(b, mlen, 2 * C_WIDTH)
    y = _channel_and_norm(h, b, mem_kv, w, g_cross, g_ffn, sub_k1, sub_k2, g_final)
    win = r3(bwin)[:, seq - min(WINDOW, seq):]
    return (y.reshape(b, seq, D_MODEL), r3(akv).reshape(b, seq, 2, A_HEADS, HEAD_DIM),
            r3(bkv).reshape(b, seq, 4, B_KV_HEADS, HEAD_DIM), win.reshape(b, -1, 2, B_KV_HEADS, HEAD_DIM),
            mem_kv.reshape(b, mlen, 2, C_HEADS, C_HEAD_DIM))


def _pad_rows(a, rows):
    return jnp.pad(a, ((0, 0), (0, rows - a.shape[1]), (0, 0)))


def _sample_group(x, moba_pool, nsa_pool, win_state, mem_kv, page_table, w, g_attn, g_cross, g_ffn, sub_k1, sub_k2,
                  g_final):
    b, t, _ = x.shape
    past = page_table.shape[1] * PAGE_SIZE
    assert t * B_KV_HEADS == SUBLANES and past % (PAGES_PER_STEP * PAGE_SIZE) == 0 and past % KEY_CHUNK == 0
    x2d = x.reshape(b * t, D_MODEL)
    pos = past + jnp.arange(t, dtype=jnp.int32)
    aq, akv, bq, bqr, bkv, bwin, gates, ga, gb = _projection(x2d, pos, t, g_attn, w["w_proj"])
    r3 = lambda a: a.reshape(b, t, a.shape[-1])

    n_pool = moba_pool.shape[0]
    pool_a = moba_pool.reshape(n_pool, PAGE_SIZE, 2 * A_WIDTH)
    n_blocks = past // A_BLOCK
    kmean = _moba_kmean(pool_a, page_table, -(-n_blocks // LANES) * LANES)
    qa = aq.reshape(b, t, A_HEADS, HEAD_DIM).transpose(0, 2, 1, 3)
    qa_rows = (qa[:, :, :, None, :] * jnp.eye(A_HEADS, dtype=F32)[None, :, None, :, None]).reshape(b, A_HEADS * t, A_WIDTH)
    oa_rows = _moba_sample(pool_a, page_table, kmean, qa_rows, _pad_rows(r3(akv), SUBLANES), past, t)
    oa = oa_rows.reshape(b, A_HEADS, t, A_WIDTH).sum(axis=1)

    pool_b = nsa_pool.reshape(n_pool, PAGE_SIZE, 4 * B_KV_WIDTH)
    cmp_tok = _compress_sample(pool_b, page_table, w["cw"], past)
    eye_g = jnp.eye(B_KV_HEADS, dtype=F32)[None, None, :, None, :, None]

    def group_rows(a):
        a = a.reshape(b, t, B_KV_HEADS, B_GROUP, HEAD_DIM).transpose(0, 3, 2, 1, 4)
        return (a[:, :, :, :, None, :] * eye_g).reshape(b, B_HEADS * t, LANES)

    gate_rows = gates[:, :N_GATES].reshape(b, t, B_KV_HEADS, B_GROUP, 3).transpose(0, 3, 2, 1, 4)
    gate_rows = jnp.pad(gate_rows.reshape(b, B_HEADS * t, 3), ((0, 0), (0, 0), (0, LANES - 3)))
    win_rows = win_state.reshape(b, win_state.shape[1], 2 * B_KV_WIDTH)
    ob_rows = _nsa_sample(pool_b, page_table, cmp_tok, group_rows(bq), group_rows(bqr), gate_rows,
                          _pad_rows(r3(bkv)[:, :, 2 * B_KV_WIDTH:], SUBLANES), win_rows,
                          _pad_rows(r3(bwin), SUBLANES), past, t)
    ob = ob_rows.reshape(b, B_GROUP, B_KV_HEADS, t, B_KV_HEADS, HEAD_DIM).sum(axis=4)
    ob = ob.transpose(0, 3, 2, 1, 4).reshape(b * t, B_WIDTH)

    h = _merge(x2d, oa.reshape(-1, A_WIDTH), ob, ga, gb, w["p_a"], w["p_b"], w["w_o"])
    mem_rows = mem_kv.reshape(b, mem_kv.shape[1], 2 * C_WIDTH)
    y = _channel_and_norm(h, b, mem_rows, w, g_cross, g_ffn, sub_k1, sub_k2, g_final)
    win = jnp.concatenate([win_rows, r3(bwin)], axis=1)
    win = win[:, win.shape[1] - min(WINDOW, win.shape[1]):]
    return (y.reshape(b, t, D_MODEL), r3(akv).reshape(b, t, 2, A_HEADS, HEAD_DIM),
            r3(bkv).reshape(b, t, 4, B_KV_HEADS, HEAD_DIM), win.reshape(b, -1, 2, B_KV_HEADS, HEAD_DIM))


def kernel(x_prompt, x_sample, cache_moba_kv, cache_nsa_kv, state_nsa_win, cache_mem_kv, page_table, mem_prompt, g_attn, w_in, pe_cmp, w_ck1, w_ck2, w_cv1, w_cv2, p_a, p_b, w_o, g_cross, g_mem, w_cq, w_ckv, w_co, g_ffn, w_pq, sub_k1, sub_k2, peer_u, peer_v, g_final):
    assert g_attn.shape[0] == 1, "the final norm is fused into the last PEER step of a single layer"
    w = _layer_weights(w_in[0], pe_cmp[0], w_ck1[0], w_ck2[0], w_cv1[0], w_cv2[0], p_a[0], p_b[0], w_o[0], w_cq[0],
                       w_ckv[0], w_co[0], w_pq[0], peer_u[0], peer_v[0])
    y_p, moba_p, nsa_p, win_p, mem_p = _prompt_group(x_prompt, mem_prompt, w, g_attn[0], g_cross[0], g_mem[0],
                                                    g_ffn[0], sub_k1[0], sub_k2[0], g_final)
    y_s, moba_s, nsa_s, win_s = _sample_group(x_sample, cache_moba_kv[0], cache_nsa_kv[0], state_nsa_win[0],
                                              cache_mem_kv[0], page_table, w, g_attn[0], g_cross[0], g_ffn[0],
                                              sub_k1[0], sub_k2[0], g_final)
    return (y_p, y_s, moba_p[None], moba_s[None], nsa_p[None], nsa_s[None], win_p[None], win_s[None], mem_p[None])
```

```python
import functools

import numpy as np
import jax
import jax.numpy as jnp
from jax import lax
from jax.experimental import pallas as pl
from jax.experimental.pallas import tpu as pltpu

F32 = jnp.float32
BF16 = jnp.bfloat16
HIGHEST = lax.Precision.HIGHEST

LANES = 128
SUBLANES = 8
VMEM_LIMIT_BYTES = 56 * 1024 * 1024

D_MODEL = 1024
HEAD_DIM = 64
ROPE_THETA = 10000.0
NORM_EPS = 1e-6
NEG_BIG = -1e30
PAGE_SIZE = 128

A_HEADS = 8
A_BLOCK = 256
A_TOPK = 3
A_WIDTH = A_HEADS * HEAD_DIM

B_HEADS = 8
B_KV_HEADS = 2
B_GROUP = B_HEADS // B_KV_HEADS
B_WIDTH = B_HEADS * HEAD_DIM
B_KV_WIDTH = B_KV_HEADS * HEAD_DIM
CMP_LEN = 32
CMP_STRIDE = 16
CMP_HIDDEN = 64
SEL_BLOCK = 64
SEL_TOPN = 16
WINDOW = 512
FORCE_SCORE = 1e4

C_HEADS = 4
C_HEAD_DIM = 128
C_WIDTH = C_HEADS * C_HEAD_DIM

P_HEADS = 8
P_NKEYS = 128
P_QDIM = 256
P_TOPK = 16
P_PICKS = P_HEADS * P_TOPK

Q_BLOCK = 256
N_MAIN = 3 * A_WIDTH + B_WIDTH + 6 * B_KV_WIDTH
N_GATES = 3 * B_HEADS
N_PROJ = N_MAIN + LANES + 2 * D_MODEL


def _params(semantics):
    return pltpu.CompilerParams(dimension_semantics=semantics, vmem_limit_bytes=VMEM_LIMIT_BYTES)


def _lane_iota(shape, dtype=jnp.int32):
    return lax.broadcasted_iota(dtype, shape, len(shape) - 1)


def _row_iota(shape, dtype=jnp.int32):
    return lax.broadcasted_iota(dtype, shape, len(shape) - 2)


def _rms(x, g):
    return x * lax.rsqrt(jnp.mean(x * x, axis=-1, keepdims=True) + NORM_EPS) * g


def _sigmoid(x):
    return 1.0 / (1.0 + jnp.exp(-x))


def _dot_t(a, b, precision=None):
    return lax.dot_general(a, b, (((1,), (1,)), ((), ())), precision=precision, preferred_element_type=F32)


def _topk_mask(x, k, lane_f):
    sel = jnp.zeros(x.shape, F32)
    for _ in range(k):
        mx = jnp.max(x, axis=-1, keepdims=True)
        first = jnp.min(jnp.where(x == mx, lane_f, float(x.shape[-1])), axis=-1, keepdims=True)
        hit = lane_f == first
        sel = jnp.where(hit, 1.0, sel)
        x = jnp.where(hit, -jnp.inf, x)
    return sel


def _softmax_rows(s, mask):
    s = jnp.where(mask, s, NEG_BIG)
    m = jnp.max(s, axis=-1, keepdims=True)
    p = jnp.where(mask, jnp.exp(s - m), 0.0)
    return p, jnp.sum(p, axis=-1, keepdims=True)


def _safe_inv(l):
    return jnp.where(l > 0.0, 1.0 / jnp.where(l > 0.0, l, 1.0), 0.0)


def _proj_kernel(x_ref, g_ref, w_ref, cs_ref, sn_ref, aq_ref, akv_ref, bq_ref, bqr_ref, bkv_ref, bwin_ref,
                 gates_ref, ga_ref, gb_ref):
    ub = _rms(x_ref[...], g_ref[...]).astype(BF16)
    cs = cs_ref[...]
    sn = sn_ref[...]
    first_half = (_lane_iota((1, LANES)) % HEAD_DIM) < (HEAD_DIM // 2)

    def cols(c0, n):
        return jnp.dot(ub, w_ref[:, c0:c0 + n], preferred_element_type=F32)

    def rot(p):
        swapped = jnp.where(first_half, pltpu.roll(p, LANES - HEAD_DIM // 2, 1), pltpu.roll(p, HEAD_DIM // 2, 1))
        return p * cs + swapped * sn

    def rot_wide(p):
        return jnp.concatenate([rot(p[:, c:c + LANES]) for c in range(0, p.shape[1], LANES)], axis=1)

    aq_ref[...] = rot_wide(cols(0, A_WIDTH))
    akv_ref[:, 0:A_WIDTH] = rot_wide(cols(A_WIDTH, A_WIDTH))
    akv_ref[:, A_WIDTH:2 * A_WIDTH] = cols(2 * A_WIDTH, A_WIDTH)
    bq = cols(3 * A_WIDTH, B_WIDTH)
    bq_ref[...] = bq
    bqr_ref[...] = rot_wide(bq)
    c0 = 3 * A_WIDTH + B_WIDTH
    bkv = cols(c0, 4 * B_KV_WIDTH)
    bkv_ref[:, 0:2 * LANES] = bkv[:, 0:2 * LANES]
    bkv_ref[:, 2 * LANES:3 * LANES] = rot(bkv[:, 2 * LANES:3 * LANES])
    bkv_ref[:, 3 * LANES:4 * LANES] = bkv[:, 3 * LANES:4 * LANES]
    bwin = cols(c0 + 4 * B_KV_WIDTH, 2 * B_KV_WIDTH)
    bwin_ref[:, 0:LANES] = rot(bwin[:, 0:LANES])
    bwin_ref[:, LANES:2 * LANES] = bwin[:, LANES:2 * LANES]
    gates_ref[...] = cols(N_MAIN, LANES)
    ga_ref[...] = cols(N_MAIN + LANES, D_MODEL)
    gb_ref[...] = cols(N_MAIN + LANES + D_MODEL, D_MODEL)


def _rope_tables(pos):
    half = HEAD_DIM // 2
    inv_freq = ROPE_THETA ** (-jnp.arange(half, dtype=F32) / half)
    ang = pos.astype(F32)[:, None] * inv_freq[None, :]
    cos, sin = jnp.cos(ang), jnp.sin(ang)
    reps = LANES // HEAD_DIM
    return jnp.tile(jnp.concatenate([cos, cos], axis=1), (1, reps)), jnp.tile(jnp.concatenate([-sin, sin], axis=1), (1, reps))


def _projection(x2d, pos, seq, g_attn, w_proj):
    n = x2d.shape[0]
    tm = min(Q_BLOCK, n)
    cs, sn = _rope_tables(pos)
    if seq >= tm:
        tab_map = lambda i: (i % (seq // tm), 0)
    else:
        cs, sn = jnp.tile(cs, (tm // seq, 1)), jnp.tile(sn, (tm // seq, 1))
        tab_map = lambda i: (0, 0)
    widths = (A_WIDTH, 2 * A_WIDTH, B_WIDTH, B_WIDTH, 4 * B_KV_WIDTH, 2 * B_KV_WIDTH, LANES, D_MODEL, D_MODEL)
    row = lambda i: (i, 0)
    fixed = lambda i: (0, 0)
    return pl.pallas_call(
        _proj_kernel,
        grid=(n // tm,),
        in_specs=[pl.BlockSpec((tm, D_MODEL), row), pl.BlockSpec((1, D_MODEL), fixed),
                  pl.BlockSpec((D_MODEL, N_PROJ), fixed), pl.BlockSpec((tm, LANES), tab_map),
                  pl.BlockSpec((tm, LANES), tab_map)],
        out_specs=[pl.BlockSpec((tm, w), row) for w in widths],
        out_shape=[jax.ShapeDtypeStruct((n, w), F32) for w in widths],
        compiler_params=_params(("parallel",)),
    )(x2d, g_attn.reshape(1, D_MODEL), w_proj, cs, sn)


def _rms_matmul_kernel(x_ref, g_ref, w_ref, o_ref):
    o_ref[...] = jnp.dot(_rms(x_ref[...], g_ref[...]).astype(BF16), w_ref[...], preferred_element_type=F32)


def _rms_matmul(x2d, g, w_bf16):
    n, d = x2d.shape
    m = w_bf16.shape[1]
    tm = min(Q_BLOCK, n)
    return pl.pallas_call(
        _rms_matmul_kernel,
        grid=(n // tm,),
        in_specs=[pl.BlockSpec((tm, d), lambda i: (i, 0)), pl.BlockSpec((1, d), lambda i: (0, 0)),
                  pl.BlockSpec((d, m), lambda i: (0, 0))],
        out_specs=pl.BlockSpec((tm, m), lambda i: (i, 0)),
        out_shape=jax.ShapeDtypeStruct((n, m), F32),
        compiler_params=_params(("parallel",)),
    )(x2d, g.reshape(1, d), w_bf16)


def _moba_prompt_kernel(q_ref, k_ref, v_ref, o_ref, *, seq):
    nb = seq // A_BLOCK
    k = k_ref[0]
    kb = k.astype(BF16)
    vb = v_ref[0].astype(BF16)
    lane = _lane_iota((1, LANES))
    lane_f = lane.astype(F32)
    kmean = jnp.concatenate(
        [jnp.mean(k[j * A_BLOCK:(j + 1) * A_BLOCK], axis=0, keepdims=True) for j in range(nb)]
        + [jnp.zeros((LANES - nb, LANES), F32)], axis=0)
    expand = jnp.where(_row_iota((LANES, seq)) == _lane_iota((LANES, seq)) // A_BLOCK, 1.0, 0.0).astype(BF16)
    scale = HEAD_DIM ** -0.5
    for i in range(nb):
        qi = q_ref[0, i * Q_BLOCK:(i + 1) * Q_BLOCK, :]
        outs = []
        for hh in range(LANES // HEAD_DIM):
            head = (lane // HEAD_DIM) == hh
            qh = jnp.where(head, qi, 0.0)
            gate = _dot_t(qh, kmean, HIGHEST)
            gate = jnp.where(lane < i, gate, -jnp.inf)
            sel = _topk_mask(gate, min(A_TOPK, nb), lane_f)
            sel = jnp.where(lane < i, sel, 0.0)
            s = _dot_t((qh * scale).astype(BF16), kb[0:(i + 1) * A_BLOCK])
            chosen = jnp.dot(sel.astype(BF16), expand[:, 0:(i + 1) * A_BLOCK], preferred_element_type=F32) > 0.5
            key = _lane_iota((Q_BLOCK, (i + 1) * A_BLOCK)) - i * A_BLOCK
            own = (key >= 0) & (key <= _row_iota((Q_BLOCK, (i + 1) * A_BLOCK)))
            p, l = _softmax_rows(s, chosen | own)
            o = jnp.dot(p.astype(BF16), vb[0:(i + 1) * A_BLOCK], preferred_element_type=F32)
            outs.append(o * _safe_inv(l))
        o_ref[0, i * Q_BLOCK:(i + 1) * Q_BLOCK, :] = jnp.where((lane // HEAD_DIM) == 0, outs[0], outs[1])


def _moba_prompt(aq, akv):
    b, seq, _ = aq.shape
    hp = A_WIDTH // LANES
    return pl.pallas_call(
        functools.partial(_moba_prompt_kernel, seq=seq),
        grid=(b, hp),
        in_specs=[pl.BlockSpec((1, seq, LANES), lambda i, j: (i, 0, j)),
                  pl.BlockSpec((1, seq, LANES), lambda i, j: (i, 0, j)),
                  pl.BlockSpec((1, seq, LANES), lambda i, j: (i, 0, hp + j))],
        out_specs=pl.BlockSpec((1, seq, LANES), lambda i, j: (i, 0, j)),
        out_shape=jax.ShapeDtypeStruct((b, seq, A_WIDTH), F32),
        compiler_params=_params(("parallel", "parallel")),
    )(aq, akv, akv)


def _compress_rows(xk_ref, xv_ref, n_rows, pe_ref, wa_ref, wb_ref, w2_ref):
    acc_a = jnp.zeros((n_rows, 2 * LANES), F32)
    acc_b = jnp.zeros((n_rows, 2 * LANES), F32)
    for tt in range(CMP_STRIDE):
        rows_tt = pl.ds(tt, n_rows, stride=CMP_STRIDE)
        xt = jnp.concatenate([xk_ref[rows_tt, :], xv_ref[rows_tt, :]], axis=1)
        acc_a += jnp.dot((xt + pe_ref[tt:tt + 1, :]).astype(BF16), wa_ref[tt], preferred_element_type=F32)
        acc_b += jnp.dot((xt + pe_ref[CMP_STRIDE + tt:CMP_STRIDE + tt + 1, :]).astype(BF16), wb_ref[tt],
                         preferred_element_type=F32)
    hidden = acc_a + pltpu.roll(acc_b, n_rows - 1, 0)
    out = jnp.dot(jax.nn.gelu(hidden).astype(BF16), w2_ref[...], preferred_element_type=F32)
    return jnp.where(_row_iota(out.shape) < n_rows - 1, out, 0.0)


def _compress_prompt_kernel(xk_ref, xv_ref, pe_ref, wa_ref, wb_ref, w2_ref, o_ref, *, seq):
    o_ref[0] = _compress_rows(xk_ref.at[0], xv_ref.at[0], seq // CMP_STRIDE, pe_ref, wa_ref, wb_ref, w2_ref)


def _compress_weights(pe_cmp, w_ck1, w_ck2, w_cv1, w_cv2):
    def diag(mk, mv):
        z = jnp.zeros_like(mk)
        rows = [[mk, z, z, z], [z, mk, z, z], [z, z, mv, z], [z, z, z, mv]]
        return jnp.concatenate([jnp.concatenate(r, axis=-1) for r in rows], axis=-2)

    k1 = w_ck1.reshape(CMP_LEN, HEAD_DIM, CMP_HIDDEN)
    v1 = w_cv1.reshape(CMP_LEN, HEAD_DIM, CMP_HIDDEN)
    w1 = diag(k1, v1).astype(BF16)
    return jnp.tile(pe_cmp, (1, 4)), w1[:CMP_STRIDE], w1[CMP_STRIDE:], diag(w_ck2, w_cv2).astype(BF16)


def _compress_prompt(bkv, cw):
    b, seq, _ = bkv.shape
    n_rows = seq // CMP_STRIDE
    pe, wa, wb, w2 = cw
    full = lambda *s: pl.BlockSpec(s, lambda i: (0,) * len(s))
    return pl.pallas_call(
        functools.partial(_compress_prompt_kernel, seq=seq),
        grid=(b,),
        in_specs=[pl.BlockSpec((1, seq, LANES), lambda i: (i, 0, 0)), pl.BlockSpec((1, seq, LANES), lambda i: (i, 0, 1)),
                  full(CMP_LEN, 2 * LANES), full(CMP_STRIDE, 2 * LANES, 2 * LANES),
                  full(CMP_STRIDE, 2 * LANES, 2 * LANES), full(2 * LANES, 2 * LANES)],
        out_specs=pl.BlockSpec((1, n_rows, 2 * LANES), lambda i: (i, 0, 0)),
        out_shape=jax.ShapeDtypeStruct((b, n_rows, 2 * LANES), F32),
        compiler_params=_params(("parallel",)),
    )(bkv, bkv, pe, wa, wb, w2)


def _overlap_matrix(n_cmp, n_sel, rows, cols):
    c_start = np.arange(n_cmp)[:, None] * CMP_STRIDE
    s_start = np.arange(n_sel)[None, :] * SEL_BLOCK
    ov = np.clip(np.minimum(c_start + CMP_LEN, s_start + SEL_BLOCK) - np.maximum(c_start, s_start), 0, None)
    out = np.zeros((rows, cols), np.float32)
    out[:n_cmp, :n_sel] = ov / CMP_STRIDE
    return jnp.asarray(out)


def _nsa_prompt_kernel(q_ref, qr_ref, gates_ref, ksel_ref, vsel_ref, kwin_ref, vwin_ref, cmp_ref, ov_ref, o_ref,
                       sel_ref, m_ref, l_ref, acc_ref, *, seq):
    g = pl.program_id(1)
    i = pl.program_id(2)
    n_cmp = (seq - CMP_LEN) // CMP_STRIDE + 1
    rows = B_GROUP * Q_BLOCK
    scale = HEAD_DIM ** -0.5
    lane = _lane_iota((1, LANES))
    lane_f = lane.astype(F32)
    in_group = (lane // HEAD_DIM) == g

    def both_halves(x):
        xg = jnp.where(in_group, x, 0.0)
        return xg + pltpu.roll(xg, HEAD_DIM, 1)

    def stack_heads(ref):
        parts = []
        for j in range(B_GROUP):
            x = ref[0, pl.ds(pl.multiple_of(i * Q_BLOCK, Q_BLOCK), Q_BLOCK), (j // 2) * LANES:(j // 2 + 1) * LANES]
            parts.append(jnp.where((lane // HEAD_DIM) == (j % 2), x, 0.0) * scale)
        return jnp.concatenate(parts, axis=0).astype(BF16)

    q4 = stack_heads(q_ref)
    qr4 = stack_heads(qr_ref)
    qpos = i * Q_BLOCK + _row_iota((Q_BLOCK, 1))

    kc = both_halves(cmp_ref[0, :, 0:LANES]).astype(BF16)
    vc = both_halves(cmp_ref[0, :, LANES:2 * LANES]).astype(BF16)
    n_tok = kc.shape[0]
    tok = _lane_iota((1, n_tok))
    cmask = (tok < n_cmp) & (tok * CMP_STRIDE + (CMP_LEN - 1) <= qpos)
    s = _dot_t(q4, kc).reshape(B_GROUP, Q_BLOCK, n_tok)
    p, l = _softmax_rows(s, cmask[None])
    p_cmp = p * _safe_inv(l)
    o_cmp = jnp.dot(p_cmp.reshape(rows, n_tok).astype(BF16), vc, preferred_element_type=F32)
    imp = jnp.dot(jnp.sum(p_cmp, axis=0), ov_ref[...], precision=HIGHEST, preferred_element_type=F32)
    cur = qpos // SEL_BLOCK
    forced = (lane == 0) | (lane == cur) | (lane == cur - 1)
    score = jnp.where(lane <= cur, jnp.where(forced, FORCE_SCORE, imp), -jnp.inf)
    n_sel = -(-seq // SEL_BLOCK)
    sel = _topk_mask(score, min(SEL_TOPN, n_sel), lane_f)
    sel_ref[...] = jnp.where(lane <= cur, sel, 0.0)

    m_ref[...] = jnp.full(m_ref.shape, NEG_BIG, F32)
    l_ref[...] = jnp.zeros(l_ref.shape, F32)
    acc_ref[...] = jnp.zeros(acc_ref.shape, F32)
    blocks_per_tile = Q_BLOCK // SEL_BLOCK

    def sel_step(j, carry):
        start = pl.multiple_of(j * Q_BLOCK, Q_BLOCK)
        kj = both_halves(ksel_ref[0, pl.ds(start, Q_BLOCK), :]).astype(BF16)
        vj = both_halves(vsel_ref[0, pl.ds(start, Q_BLOCK), :]).astype(BF16)
        kpos = j * Q_BLOCK + _lane_iota((1, Q_BLOCK))
        blk = j * blocks_per_tile + _lane_iota((LANES, Q_BLOCK)) // SEL_BLOCK
        expand = jnp.where(_row_iota((LANES, Q_BLOCK)) == blk, 1.0, 0.0).astype(BF16)
        chosen = jnp.dot(sel_ref[...].astype(BF16), expand, preferred_element_type=F32) > 0.5
        mask = chosen & (kpos <= qpos)
        sj = jnp.where(mask[None], _dot_t(qr4, kj).reshape(B_GROUP, Q_BLOCK, Q_BLOCK), NEG_BIG)
        m_old = m_ref[...]
        m_new = jnp.maximum(m_old, jnp.max(sj, axis=-1, keepdims=True))
        pj = jnp.where(mask[None], jnp.exp(sj - m_new), 0.0)
        alpha = jnp.exp(m_old - m_new)
        l_ref[...] = alpha * l_ref[...] + jnp.sum(pj, axis=-1, keepdims=True)
        pv = jnp.dot(pj.reshape(rows, Q_BLOCK).astype(BF16), vj, preferred_element_type=F32)
        acc_ref[...] = alpha * acc_ref[...] + pv.reshape(B_GROUP, Q_BLOCK, LANES)
        m_ref[...] = m_new
        return carry

    lax.fori_loop(0, i + 1, sel_step, 0)
    o_sel = (acc_ref[...] * _safe_inv(l_ref[...])).reshape(rows, LANES)

    span = WINDOW + Q_BLOCK
    w0 = jnp.maximum(i * Q_BLOCK - WINDOW, 0)
    start = pl.multiple_of(w0, Q_BLOCK)
    kw = both_halves(kwin_ref[0, pl.ds(start, span), :]).astype(BF16)
    vw = both_halves(vwin_ref[0, pl.ds(start, span), :]).astype(BF16)
    kpos = w0 + _lane_iota((1, span))
    wmask = (kpos <= qpos) & (kpos > qpos - WINDOW)
    p, l = _softmax_rows(_dot_t(qr4, kw).reshape(B_GROUP, Q_BLOCK, span), wmask[None])
    o_win = jnp.dot(p.reshape(rows, span).astype(BF16), vw, preferred_element_type=F32)
    o_win = o_win * _safe_inv(l).reshape(rows, 1)

    gates = gates_ref[0]
    outs = []
    for j in range(B_GROUP):
        r = slice(j * Q_BLOCK, (j + 1) * Q_BLOCK)
        gt = [_sigmoid(jnp.sum(jnp.where(lane == (g * B_GROUP + j) * 3 + c, gates, 0.0), axis=-1, keepdims=True))
              for c in range(3)]
        outs.append(gt[0] * o_cmp[r] + gt[1] * o_sel[r] + gt[2] * o_win[r])
    left = (lane // HEAD_DIM) == 0
    o_ref[0] = jnp.concatenate([jnp.where(left, outs[0], outs[1]), jnp.where(left, outs[2], outs[3])], axis=1)


def _nsa_prompt(bq, bqr, gates, bkv, bwin, cmp_tok):
    b, seq, _ = bq.shape
    nq = seq // Q_BLOCK
    n_tok = cmp_tok.shape[1]
    n_cmp = (seq - CMP_LEN) // CMP_STRIDE + 1
    n_sel = -(-seq // SEL_BLOCK)
    assert n_sel <= LANES and seq >= WINDOW + Q_BLOCK
    overlap = _overlap_matrix(n_cmp, n_sel, n_tok, LANES)
    qspec = pl.BlockSpec((1, seq, 2 * LANES), lambda i, g, t: (i, 0, g))
    col = lambda c: pl.BlockSpec((1, seq, LANES), lambda i, g, t: (i, 0, c))
    return pl.pallas_call(
        functools.partial(_nsa_prompt_kernel, seq=seq),
        grid=(b, B_KV_HEADS, nq),
        in_specs=[qspec, qspec, pl.BlockSpec((1, Q_BLOCK, LANES), lambda i, g, t: (i, t, 0)),
                  col(2), col(3), col(0), col(1),
                  pl.BlockSpec((1, n_tok, 2 * LANES), lambda i, g, t: (i, 0, 0)),
                  pl.BlockSpec((n_tok, LANES), lambda i, g, t: (0, 0))],
        out_specs=pl.BlockSpec((1, Q_BLOCK, 2 * LANES), lambda i, g, t: (i, t, g)),
        out_shape=jax.ShapeDtypeStruct((b, seq, B_WIDTH), F32),
        scratch_shapes=[pltpu.VMEM((Q_BLOCK, LANES), F32), pltpu.VMEM((B_GROUP, Q_BLOCK, 1), F32),
                        pltpu.VMEM((B_GROUP, Q_BLOCK, 1), F32), pltpu.VMEM((B_GROUP, Q_BLOCK, LANES), F32)],
        compiler_params=_params(("parallel", "parallel", "arbitrary")),
    )(bq, bqr, gates, bkv, bkv, bwin, bwin, cmp_tok, overlap)


def _merge_kernel(x_ref, oa_ref, ob_ref, ga_ref, gb_ref, pa_ref, pb_ref, wo_ref, h_ref):
    ya = jnp.dot(oa_ref[...].astype(BF16), pa_ref[...], preferred_element_type=F32)
    yb = jnp.dot(ob_ref[...].astype(BF16), pb_ref[...], preferred_element_type=F32)
    mixed = _sigmoid(ga_ref[...]) * ya + _sigmoid(gb_ref[...]) * yb
    h_ref[...] = x_ref[...] + jnp.dot(mixed.astype(BF16), wo_ref[...], preferred_element_type=F32)


def _merge(x2d, oa, ob, ga, gb, pa, pb, wo):
    n = x2d.shape[0]
    tm = min(Q_BLOCK, n)
    row = lambda w: pl.BlockSpec((tm, w), lambda i: (i, 0))
    full = lambda a: pl.BlockSpec(a.shape, lambda i: (0, 0))
    return pl.pallas_call(
        _merge_kernel,
        grid=(n // tm,),
        in_specs=[row(D_MODEL), row(A_WIDTH), row(B_WIDTH), row(D_MODEL), row(D_MODEL), full(pa), full(pb), full(wo)],
        out_specs=row(D_MODEL),
        out_shape=jax.ShapeDtypeStruct((n, D_MODEL), F32),
        compiler_params=_params(("parallel",)),
    )(x2d, oa, ob, ga, gb, pa, pb, wo)


def _cross_kernel(h_ref, mem_ref, g_ref, wq_ref, wo_ref, o_ref):
    h = h_ref[0]
    q = jnp.dot(_rms(h, g_ref[...]).astype(BF16), wq_ref[...], preferred_element_type=F32)
    scale = C_HEAD_DIM ** -0.5
    outs = []
    for hd in range(C_HEADS):
        c = slice(hd * C_HEAD_DIM, (hd + 1) * C_HEAD_DIM)
        kh = mem_ref[0, :, c].astype(BF16)
        vh = mem_ref[0, :, C_WIDTH + hd * C_HEAD_DIM:C_WIDTH + (hd + 1) * C_HEAD_DIM].astype(BF16)
        s = _dot_t(q[:, c].astype(BF16), kh) * scale
        m = jnp.max(s, axis=-1, keepdims=True)
        p = jnp.exp(s - m)
        o = jnp.dot(p.astype(BF16), vh, preferred_element_type=F32)
        outs.append(o / jnp.sum(p, axis=-1, keepdims=True))
    att = jnp.concatenate(outs, axis=1).astype(BF16)
    o_ref[0] = h + jnp.dot(att, wo_ref[...], preferred_element_type=F32)


def _cross(h3, mem_kv, g_cross, wq, wo):
    b, t, _ = h3.shape
    tq = min(Q_BLOCK, t)
    mlen = mem_kv.shape[1]
    return pl.pallas_call(
        _cross_kernel,
        grid=(b, t // tq),
        in_specs=[pl.BlockSpec((1, tq, D_MODEL), lambda i, j: (i, j, 0)),
                  pl.BlockSpec((1, mlen, 2 * C_WIDTH), lambda i, j: (i, 0, 0)),
                  pl.BlockSpec((1, D_MODEL), lambda i, j: (0, 0)),
                  pl.BlockSpec(wq.shape, lambda i, j: (0, 0)), pl.BlockSpec(wo.shape, lambda i, j: (0, 0))],
        out_specs=pl.BlockSpec((1, tq, D_MODEL), lambda i, j: (i, j, 0)),
        out_shape=jax.ShapeDtypeStruct((b, t, D_MODEL), F32),
        compiler_params=_params(("parallel", "parallel")),
    )(h3, mem_kv, g_cross.reshape(1, D_MODEL), wq, wo)


def _peer_select_kernel(h_ref, g_ref, wq_ref, k1_ref, k2_ref, z_ref, eid_ref, gate_ref, s_ref, e_scr, w_scr, *, tm):
    z = _rms(h_ref[...], g_ref[...])
    z_ref[...] = z
    zb = z.astype(BF16)
    half = P_QDIM // 2
    k1 = k1_ref[...]
    k2 = k2_ref[...]
    for hd in range(P_HEADS):
        q = jnp.dot(zb, wq_ref[:, hd * P_QDIM:(hd + 1) * P_QDIM], preferred_element_type=F32)
        s1 = _dot_t(k1, q[:, 0:half], HIGHEST)
        s2 = _dot_t(k2, q[:, half:P_QDIM], HIGHEST)
        for tb in range(tm // LANES):
            s_ref[hd, tb, 0] = s1[:, tb * LANES:(tb + 1) * LANES]
            s_ref[hd, tb, 1] = s2[:, tb * LANES:(tb + 1) * LANES]

    n_cand = P_TOPK * P_TOPK
    key_f = _row_iota((P_NKEYS, LANES)).astype(F32)
    cand_f = _row_iota((n_cand, LANES)).astype(F32)

    def take_top(x, ids_f, limit):
        mx = jnp.max(x, axis=0, keepdims=True)
        first = jnp.min(jnp.where(x == mx, ids_f, limit), axis=0, keepdims=True)
        hit = ids_f == first
        return mx, first, hit, jnp.where(hit, -jnp.inf, x)

    def select(x1, x2):
        v1, i1, v2, i2 = [], [], [], []
        for _ in range(P_TOPK):
            m, a, _, x1 = take_top(x1, key_f, float(P_NKEYS))
            v1.append(m)
            i1.append(a)
            m, a, _, x2 = take_top(x2, key_f, float(P_NKEYS))
            v2.append(m)
            i2.append(a)
        v2s = jnp.concatenate(v2, axis=0)
        i2s = jnp.concatenate(i2, axis=0)
        cand = jnp.concatenate([v1[a] + v2s for a in range(P_TOPK)], axis=0)
        cid = jnp.concatenate([i1[a] * float(P_NKEYS) + i2s for a in range(P_TOPK)], axis=0)
        es, ss = [], []
        for _ in range(P_TOPK):
            mx, _, hit, cand = take_top(cand, cand_f, float(n_cand))
            es.append(jnp.max(jnp.where(hit, cid, -1.0), axis=0, keepdims=True))
            ss.append(mx)
        ex = jnp.exp(jnp.concatenate(ss, axis=0) - ss[0])
        return jnp.concatenate(es, axis=0), ex / jnp.sum(ex, axis=0, keepdims=True)

    def step(hd, carry):
        rows = pl.ds(pl.multiple_of(hd * P_TOPK, P_TOPK), P_TOPK)
        for tb in range(tm // LANES):
            e16, w16 = select(s_ref[hd, tb, 0], s_ref[hd, tb, 1])
            e_scr[tb, rows, :] = e16
            w_scr[tb, rows, :] = w16
        return carry

    lax.fori_loop(0, P_HEADS, step, 0)
    for tb in range(tm // LANES):
        eid_ref[tb * LANES:(tb + 1) * LANES, :] = jnp.transpose(e_scr[tb]).astype(jnp.int32)
        gate_ref[tb * LANES:(tb + 1) * LANES, :] = jnp.transpose(w_scr[tb])


def _peer_select(h2d, g_ffn, wq, k1, k2):
    n = h2d.shape[0]
    tm = min(Q_BLOCK, n)
    assert tm % LANES == 0 and n % tm == 0
    row = lambda: pl.BlockSpec((tm, D_MODEL), lambda i: (i, 0))
    pick = lambda: pl.BlockSpec((tm, LANES), lambda i: (i, 0))
    full = lambda a: pl.BlockSpec(a.shape, lambda i: (0, 0))
    return pl.pallas_call(
        functools.partial(_peer_select_kernel, tm=tm),
        grid=(n // tm,),
        in_specs=[row(), pl.BlockSpec((1, D_MODEL), lambda i: (0, 0)), full(wq), full(k1), full(k2)],
        out_specs=[row(), pick(), pick()],
        out_shape=[jax.ShapeDtypeStruct((n, D_MODEL), F32), jax.ShapeDtypeStruct((n, LANES), jnp.int32),
                   jax.ShapeDtypeStruct((n, LANES), F32)],
        scratch_shapes=[pltpu.VMEM((P_HEADS, tm // LANES, 2, P_NKEYS, LANES), F32),
                        pltpu.VMEM((tm // LANES, P_PICKS, LANES), F32), pltpu.VMEM((tm // LANES, P_PICKS, LANES), F32)],
        compiler_params=_params(("parallel",)),
    )(h2d, g_ffn.reshape(1, D_MODEL), wq, k1, k2)


PEER_TOKENS = 8
ROW_TILES = D_MODEL // LANES


def _peer_gather_kernel(eid_ref, h_ref, z_ref, gate_ref, gfin_ref, uv_hbm, y_ref, buf, sem):
    j = pl.program_id(0)
    n_tiles = pl.num_programs(0) - 1
    rows = PEER_TOKENS * P_PICKS

    for to_slot in range(2):
        @pl.when((j < n_tiles) & (j % 2 == to_slot))
        def _(to_slot=to_slot):
            for r in range(rows):
                e = eid_ref[r // P_PICKS, r % P_PICKS]
                pltpu.make_async_copy(uv_hbm.at[e], buf.at[to_slot, pl.ds(r, 1)], sem.at[to_slot]).start()

    @pl.when(j > 0)
    def _():
        slot = (j - 1) % 2
        pltpu.make_async_copy(uv_hbm.at[pl.ds(0, rows), 0], buf.at[slot], sem.at[slot]).wait()
        gate_t = jnp.transpose(jnp.concatenate(
            [gate_ref[...], jnp.zeros((LANES - PEER_TOKENS, P_PICKS), F32)], axis=0))
        for p in range(PEER_TOKENS):
            picks = pl.ds(p * P_PICKS, P_PICKS)
            acc = jnp.zeros((P_PICKS, LANES), F32)
            for s in range(ROW_TILES):
                acc += buf[slot, picks, s * LANES:(s + 1) * LANES] * z_ref[p:p + 1, s * LANES:(s + 1) * LANES]
            act = jax.nn.gelu(jnp.sum(acc, axis=-1, keepdims=True))
            coef = jnp.broadcast_to(gate_t[:, p:p + 1] * act, (P_PICKS, LANES))
            outs = []
            for s in range(ROW_TILES):
                v_s = buf[slot, picks, D_MODEL + s * LANES:D_MODEL + (s + 1) * LANES]
                outs.append(jnp.sum(coef * v_s, axis=0, keepdims=True))
            y_ref[p:p + 1, :] = h_ref[p:p + 1, :] + jnp.concatenate(outs, axis=1)
        y_ref[...] = _rms(y_ref[...], gfin_ref[...])


def _peer_gather(eid, h2d, z, gate, g_final, uv):
    n = h2d.shape[0]
    tiles = n // PEER_TOKENS
    ahead = lambda j: (jnp.minimum(j, tiles - 1), 0)
    behind = lambda j: (jnp.maximum(j - 1, 0), 0)
    row = lambda w: pl.BlockSpec((PEER_TOKENS, w), behind)
    return pl.pallas_call(
        _peer_gather_kernel,
        grid=(tiles + 1,),
        in_specs=[pl.BlockSpec((PEER_TOKENS, P_PICKS), ahead, memory_space=pltpu.SMEM),
                  row(D_MODEL), row(D_MODEL), row(P_PICKS), pl.BlockSpec((1, D_MODEL), lambda j: (0, 0)),
                  pl.BlockSpec(memory_space=pl.ANY)],
        out_specs=row(D_MODEL),
        out_shape=jax.ShapeDtypeStruct((n, D_MODEL), F32),
        scratch_shapes=[pltpu.VMEM((2, PEER_TOKENS * P_PICKS, 2 * D_MODEL), F32),
                        pltpu.SemaphoreType.DMA((2,))],
        compiler_params=_params(("arbitrary",)),
    )(eid, h2d, z, gate, g_final.reshape(1, D_MODEL), uv)


PAGES_PER_STEP = 16
KEY_CHUNK = 2048


def _page_specs(width, col_block):
    def spec(k):
        return pl.BlockSpec((1, PAGE_SIZE, width), lambda b, c, pt: (pt[b, c * PAGES_PER_STEP + k], 0, col_block))
    return [spec(k) for k in range(PAGES_PER_STEP)]


def _online_update(s, mask, v, m_ref, l_ref, acc_ref):
    s = jnp.where(mask, s, NEG_BIG)
    m_old = m_ref[...]
    m_new = jnp.maximum(m_old, jnp.max(s, axis=-1, keepdims=True))
    p = jnp.where(mask, jnp.exp(s - m_new), 0.0)
    alpha = jnp.exp(m_old - m_new)
    l_ref[...] = alpha * l_ref[...] + jnp.sum(p, axis=-1, keepdims=True)
    acc_ref[...] = alpha * acc_ref[...] + jnp.dot(p.astype(BF16), v, preferred_element_type=F32)
    m_ref[...] = m_new


def _moba_kmean_kernel(pt_ref, *refs):
    pages, o_ref = refs[:PAGES_PER_STEP], refs[PAGES_PER_STEP]
    c = pl.program_id(1)

    @pl.when(c == 0)
    def _():
        o_ref[...] = jnp.zeros(o_ref.shape, F32)

    per_block = A_BLOCK // PAGE_SIZE
    means = []
    for j in range(PAGES_PER_STEP // per_block):
        tot = sum(jnp.sum(pages[j * per_block + t][0], axis=0, keepdims=True) for t in range(per_block))
        means.append(tot / A_BLOCK)
    n = len(means)
    o_ref[0, pl.ds(pl.multiple_of(c * n, n), n), :] = jnp.concatenate(means, axis=0)


def _moba_kmean(pool, page_table, n_rows):
    b, n_pages = page_table.shape
    return pl.pallas_call(
        _moba_kmean_kernel,
        grid_spec=pltpu.PrefetchScalarGridSpec(
            num_scalar_prefetch=1, grid=(b, n_pages // PAGES_PER_STEP),
            in_specs=_page_specs(A_WIDTH, 0),
            out_specs=pl.BlockSpec((1, n_rows, A_WIDTH), lambda i, c, pt: (i, 0, 0))),
        out_shape=jax.ShapeDtypeStruct((b, n_rows, A_WIDTH), F32),
        compiler_params=_params(("parallel", "arbitrary")),
    )(page_table, *([pool] * PAGES_PER_STEP))


def _moba_sample_kernel(pt_ref, *refs, past, t_new):
    pages = refs[:PAGES_PER_STEP]
    kmean_ref, q_ref, new_ref, o_ref, sel_ref, m_ref, l_ref, acc_ref = refs[PAGES_PER_STEP:]
    c = pl.program_id(1)
    rows = A_HEADS * t_new
    n_lanes = sel_ref.shape[1]
    cur = past // A_BLOCK
    q = q_ref[0]
    qb = (q * HEAD_DIM ** -0.5).astype(BF16)

    @pl.when(c == 0)
    def _():
        lane = _lane_iota((1, n_lanes))
        gate = jnp.where(lane < cur, _dot_t(q, kmean_ref[0], HIGHEST), -jnp.inf)
        sel = _topk_mask(gate, min(A_TOPK, cur + 1), lane.astype(F32))
        sel_ref[...] = jnp.where(lane < cur, sel, 0.0)
        m_ref[...] = jnp.full(m_ref.shape, NEG_BIG, F32)
        l_ref[...] = jnp.zeros(l_ref.shape, F32)
        acc_ref[...] = jnp.zeros(acc_ref.shape, F32)

    n_keys = PAGES_PER_STEP * PAGE_SIZE
    k = jnp.concatenate([p[0, :, 0:A_WIDTH] for p in pages], axis=0).astype(BF16)
    v = jnp.concatenate([p[0, :, A_WIDTH:2 * A_WIDTH] for p in pages], axis=0).astype(BF16)
    blk = c * (n_keys // A_BLOCK) + _lane_iota((n_lanes, n_keys)) // A_BLOCK
    expand = jnp.where(_row_iota((n_lanes, n_keys)) == blk, 1.0, 0.0).astype(BF16)
    chosen = jnp.dot(sel_ref[...].astype(BF16), expand, preferred_element_type=F32) > 0.5
    _online_update(_dot_t(qb, k), chosen, v, m_ref, l_ref, acc_ref)

    @pl.when(c == pl.num_programs(1) - 1)
    def _():
        kn = new_ref[0, :, 0:A_WIDTH].astype(BF16)
        vn = new_ref[0, :, A_WIDTH:2 * A_WIDTH].astype(BF16)
        t_key = _lane_iota((rows, kn.shape[0]))
        t_row = _row_iota((rows, kn.shape[0])) % t_new
        _online_update(_dot_t(qb, kn), (t_key <= t_row) & (t_key < t_new), vn, m_ref, l_ref, acc_ref)
        own = (_lane_iota((rows, A_WIDTH)) // HEAD_DIM) == (_row_iota((rows, A_WIDTH)) // t_new)
        o_ref[0] = jnp.where(own, acc_ref[...] * _safe_inv(l_ref[...]), 0.0)


def _moba_sample(pool, page_table, kmean, q_rows, new_kv, past, t_new):
    b, n_pages = page_table.shape
    rows = A_HEADS * t_new
    per_b = lambda s: pl.BlockSpec((1,) + s, lambda i, c, pt: (i, 0, 0))
    return pl.pallas_call(
        functools.partial(_moba_sample_kernel, past=past, t_new=t_new),
        grid_spec=pltpu.PrefetchScalarGridSpec(
            num_scalar_prefetch=1, grid=(b, n_pages // PAGES_PER_STEP),
            in_specs=_page_specs(2 * A_WIDTH, 0) + [per_b(kmean.shape[1:]), per_b(q_rows.shape[1:]),
                                                     per_b(new_kv.shape[1:])],
            out_specs=per_b((rows, A_WIDTH)),
            scratch_shapes=[pltpu.VMEM((rows, kmean.shape[1]), F32), pltpu.VMEM((rows, 1), F32),
                            pltpu.VMEM((rows, 1), F32), pltpu.VMEM((rows, A_WIDTH), F32)]),
        out_shape=jax.ShapeDtypeStruct((b, rows, A_WIDTH), F32),
        compiler_params=_params(("parallel", "arbitrary")),
    )(page_table, *([pool] * PAGES_PER_STEP), kmean, q_rows, new_kv)


def _compress_sample_kernel(pt_ref, *refs, past):
    pages = refs[:PAGES_PER_STEP]
    pe_ref, wa_ref, wb_ref, w2_ref, o_ref, xk_ref, xv_ref = refs[PAGES_PER_STEP:]
    c = pl.program_id(1)
    for k, page in enumerate(pages):
        start = pl.multiple_of((c * PAGES_PER_STEP + k) * PAGE_SIZE, PAGE_SIZE)
        xk_ref[pl.ds(start, PAGE_SIZE), :] = page[0, :, 0:LANES]
        xv_ref[pl.ds(start, PAGE_SIZE), :] = page[0, :, LANES:2 * LANES]

    @pl.when(c == pl.num_programs(1) - 1)
    def _():
        o_ref[0] = _compress_rows(xk_ref, xv_ref, past // CMP_STRIDE, pe_ref, wa_ref, wb_ref, w2_ref)


def _compress_sample(pool, page_table, cw, past):
    b, n_pages = page_table.shape
    n_rows = past // CMP_STRIDE
    pe, wa, wb, w2 = cw
    full = lambda a: pl.BlockSpec(a.shape, lambda i, c, pt: (0,) * a.ndim)
    return pl.pallas_call(
        functools.partial(_compress_sample_kernel, past=past),
        grid_spec=pltpu.PrefetchScalarGridSpec(
            num_scalar_prefetch=1, grid=(b, n_pages // PAGES_PER_STEP),
            in_specs=_page_specs(2 * LANES, 0) + [full(pe), full(wa), full(wb), full(w2)],
            out_specs=pl.BlockSpec((1, n_rows, 2 * LANES), lambda i, c, pt: (i, 0, 0)),
            scratch_shapes=[pltpu.VMEM((past, LANES), F32), pltpu.VMEM((past, LANES), F32)]),
        out_shape=jax.ShapeDtypeStruct((b, n_rows, 2 * LANES), F32),
        compiler_params=_params(("parallel", "arbitrary")),
    )(page_table, *([pool] * PAGES_PER_STEP), pe, wa, wb, w2)


def _nsa_sample_kernel(pt_ref, *refs, past, t_new):
    pages = refs[:PAGES_PER_STEP]
    (cmp_ref, q_ref, qr_ref, gates_ref, newsel_ref, win_ref, newwin_ref, ov_ref, o_ref,
     kv_ref, m_ref, l_ref, acc_ref) = refs[PAGES_PER_STEP:]
    c = pl.program_id(1)
    for k, page in enumerate(pages):
        start = pl.multiple_of((c * PAGES_PER_STEP + k) * PAGE_SIZE, PAGE_SIZE)
        kv_ref[pl.ds(start, PAGE_SIZE), :] = page[0]

    @pl.when(c == pl.num_programs(1) - 1)
    def _():
        pad = newsel_ref.shape[1]
        kv_ref[past:past + PAGE_SIZE, :] = jnp.zeros((PAGE_SIZE, 2 * LANES), F32)
        kv_ref[past:past + pad, :] = newsel_ref[0]
        rows = B_HEADS * t_new
        slab = B_KV_HEADS * t_new
        scale = HEAD_DIM ** -0.5
        qb = (q_ref[0] * scale).astype(BF16)
        qrb = (qr_ref[0] * scale).astype(BF16)
        pos = past + _row_iota((rows, 1)) % t_new

        n_tok = cmp_ref.shape[1]
        n_cmp = n_tok - 1
        kc = cmp_ref[0, :, 0:LANES].astype(BF16)
        vc = cmp_ref[0, :, LANES:2 * LANES].astype(BF16)
        tok = _lane_iota((1, n_tok))
        p, l = _softmax_rows(_dot_t(qb, kc), (tok < n_cmp) & (tok * CMP_STRIDE + (CMP_LEN - 1) <= pos))
        p_cmp = p * _safe_inv(l)
        o_cmp = jnp.dot(p_cmp.astype(BF16), vc, preferred_element_type=F32)
        p_sum = sum(p_cmp[j * slab:(j + 1) * slab] for j in range(B_GROUP))
        imp = jnp.dot(p_sum, ov_ref[...], precision=HIGHEST, preferred_element_type=F32)
        n_lanes = ov_ref.shape[1]
        lane = _lane_iota((1, n_lanes))
        cur = past // SEL_BLOCK
        forced = (lane == 0) | (lane == cur) | (lane == cur - 1)
        score = jnp.where(lane <= cur, jnp.where(forced, FORCE_SCORE, imp), -jnp.inf)
        sel = _topk_mask(score, min(SEL_TOPN, cur + 1), lane.astype(F32))
        sel = jnp.where(lane <= cur, sel, 0.0).astype(BF16)

        m_ref[...] = jnp.full(m_ref.shape, NEG_BIG, F32)
        l_ref[...] = jnp.zeros(l_ref.shape, F32)
        acc_ref[...] = jnp.zeros(acc_ref.shape, F32)
        total = past + PAGE_SIZE
        for k0 in range(0, total, KEY_CHUNK):
            n_keys = min(KEY_CHUNK, total - k0)
            ks = kv_ref[k0:k0 + n_keys, 0:LANES].astype(BF16)
            vs = kv_ref[k0:k0 + n_keys, LANES:2 * LANES].astype(BF16)
            kpos = k0 + _lane_iota((1, n_keys))
            blk = (k0 + _lane_iota((n_lanes, n_keys))) // SEL_BLOCK
            expand = jnp.where(_row_iota((n_lanes, n_keys)) == blk, 1.0, 0.0).astype(BF16)
            chosen = jnp.dot(sel, expand, preferred_element_type=F32) > 0.5
            mask = jnp.concatenate([chosen] * B_GROUP, axis=0) & (kpos <= pos)
            _online_update(_dot_t(qrb, ks), mask, vs, m_ref, l_ref, acc_ref)
        o_sel = acc_ref[...] * _safe_inv(l_ref[...])

        n_win = win_ref.shape[1]
        kw = jnp.concatenate([win_ref[0, :, 0:LANES], newwin_ref[0, :, 0:LANES]], axis=0).astype(BF16)
        vw = jnp.concatenate([win_ref[0, :, LANES:2 * LANES], newwin_ref[0, :, LANES:2 * LANES]], axis=0).astype(BF16)
        wpos = past - n_win + _lane_iota((1, kw.shape[0]))
        p, l = _softmax_rows(_dot_t(qrb, kw), (wpos <= pos) & (wpos > pos - WINDOW))
        o_win = jnp.dot(p.astype(BF16), vw, preferred_element_type=F32) * _safe_inv(l)

        gt = _sigmoid(gates_ref[0])
        o = gt[:, 0:1] * o_cmp + gt[:, 1:2] * o_sel + gt[:, 2:3] * o_win
        own = (_lane_iota((rows, LANES)) // HEAD_DIM) == ((_row_iota((rows, LANES)) // t_new) % B_KV_HEADS)
        o_ref[0] = jnp.where(own, o, 0.0)


def _nsa_sample(pool, page_table, cmp_tok, q_rows, qr_rows, gate_rows, new_sel, win_state, new_win, past, t_new):
    b, n_pages = page_table.shape
    rows = B_HEADS * t_new
    n_tok = cmp_tok.shape[1]
    n_sel = past // SEL_BLOCK + 1
    n_lanes = -(-n_sel // LANES) * LANES
    overlap = _overlap_matrix(n_tok - 1, n_sel, n_tok, n_lanes)
    per_b = lambda a: pl.BlockSpec((1,) + a.shape[1:], lambda i, c, pt: (i, 0, 0))
    return pl.pallas_call(
        functools.partial(_nsa_sample_kernel, past=past, t_new=t_new),
        grid_spec=pltpu.PrefetchScalarGridSpec(
            num_scalar_prefetch=1, grid=(b, n_pages // PAGES_PER_STEP),
            in_specs=_page_specs(2 * LANES, 1) + [per_b(cmp_tok), per_b(q_rows), per_b(qr_rows), per_b(gate_rows),
                                                  per_b(new_sel), per_b(win_state), per_b(new_win),
                                                  pl.BlockSpec(overlap.shape, lambda i, c, pt: (0, 0))],
            out_specs=pl.BlockSpec((1, rows, LANES), lambda i, c, pt: (i, 0, 0)),
            scratch_shapes=[pltpu.VMEM((past + PAGE_SIZE, 2 * LANES), F32), pltpu.VMEM((rows, 1), F32),
                            pltpu.VMEM((rows, 1), F32), pltpu.VMEM((rows, LANES), F32)]),
        out_shape=jax.ShapeDtypeStruct((b, rows, LANES), F32),
        compiler_params=_params(("parallel", "arbitrary")),
    )(page_table, *([pool] * PAGES_PER_STEP), cmp_tok, q_rows, qr_rows, gate_rows, new_sel, win_state, new_win,
      overlap)


def _layer_weights(w_in, pe_cmp, w_ck1, w_ck2, w_cv1, w_cv2, p_a, p_b, w_o, w_cq, w_ckv, w_co, w_pq, peer_u, peer_v):
    w_proj = jnp.concatenate(
        [w_in[:, :N_MAIN], jnp.pad(w_in[:, N_MAIN:N_MAIN + N_GATES], ((0, 0), (0, LANES - N_GATES))),
         w_in[:, N_MAIN + N_GATES:]], axis=1).astype(BF16)
    uv = jnp.concatenate([peer_u, peer_v], axis=1)[:, None, :]
    return dict(w_proj=w_proj, cw=_compress_weights(pe_cmp, w_ck1, w_ck2, w_cv1, w_cv2),
                p_a=p_a.astype(BF16), p_b=p_b.astype(BF16), w_o=w_o.astype(BF16), w_cq=w_cq.astype(BF16),
                w_ckv=w_ckv.astype(BF16), w_co=w_co.astype(BF16), w_pq=w_pq.astype(BF16),
                uv=uv)


def _channel_and_norm(h2d, batch, mem_kv, w, g_cross, g_ffn, sub_k1, sub_k2, g_final):
    h3 = _cross(h2d.reshape(batch, -1, D_MODEL), mem_kv, g_cross, w["w_cq"], w["w_co"])
    h2 = h3.reshape(-1, D_MODEL)
    z, eid, gate = _peer_select(h2, g_ffn, w["w_pq"], sub_k1, sub_k2)
    return _peer_gather(eid, h2, z, gate, g_final, w["uv"])


def _prompt_group(x, mem, w, g_attn, g_cross, g_mem, g_ffn, sub_k1, sub_k2, g_final):
    b, seq, _ = x.shape
    x2d = x.reshape(b * seq, D_MODEL)
    aq, akv, bq, bqr, bkv, bwin, gates, ga, gb = _projection(x2d, jnp.arange(seq, dtype=jnp.int32), seq, g_attn,
                                                             w["w_proj"])
    r3 = lambda a: a.reshape(b, seq, a.shape[-1])
    oa = _moba_prompt(r3(aq), r3(akv))
    cmp_tok = _compress_prompt(r3(bkv), w["cw"])
    ob = _nsa_prompt(r3(bq), r3(bqr), r3(gates), r3(bkv), r3(bwin), cmp_tok)
    h = _merge(x2d, oa.reshape(-1, A_WIDTH), ob.reshape(-1, B_WIDTH), ga, gb, w["p_a"], w["p_b"], w["w_o"])
    mlen = mem.shape[1]
    mem_kv = _rms_matmul(mem.reshape(b * mlen, D_MODEL), g_mem, w["w_ckv"]).reshape(b, mlen, 2 * C_WIDTH)
    y = _channel_and_norm(h, b, mem_kv, w, g_cross, g_ffn, sub_k1, sub_k2, g_final)
    win = r3(bwin)[:, seq - min(WINDOW, seq):]
    return (y.reshape(b, seq, D_MODEL), r3(akv).reshape(b, seq, 2, A_HEADS, HEAD_DIM),
            r3(bkv).reshape(b, seq, 4, B_KV_HEADS, HEAD_DIM), win.reshape(b, -1, 2, B_KV_HEADS, HEAD_DIM),
            mem_kv.reshape(b, mlen, 2, C_HEADS, C_HEAD_DIM))


def _pad_rows(a, rows):
    return jnp.pad(a, ((0, 0), (0, rows - a.shape[1]), (0, 0)))


def _sample_group(x, moba_pool, nsa_pool, win_state, mem_kv, page_table, w, g_attn, g_cross, g_ffn, sub_k1, sub_k2,
                  g_final):
    b, t, _ = x.shape
    past = page_table.shape[1] * PAGE_SIZE
    assert t * B_KV_HEADS == SUBLANES and past % (PAGES_PER_STEP * PAGE_SIZE) == 0 and past % KEY_CHUNK == 0
    x2d = x.reshape(b * t, D_MODEL)
    pos = past + jnp.arange(t, dtype=jnp.int32)
    aq, akv, bq, bqr, bkv, bwin, gates, ga, gb = _projection(x2d, pos, t, g_attn, w["w_proj"])
    r3 = lambda a: a.reshape(b, t, a.shape[-1])

    n_pool = moba_pool.shape[0]
    pool_a = moba_pool.reshape(n_pool, PAGE_SIZE, 2 * A_WIDTH)
    n_blocks = past // A_BLOCK
    kmean = _moba_kmean(pool_a, page_table, -(-n_blocks // LANES) * LANES)
    qa = aq.reshape(b, t, A_HEADS, HEAD_DIM).transpose(0, 2, 1, 3)
    qa_rows = (qa[:, :, :, None, :] * jnp.eye(A_HEADS, dtype=F32)[None, :, None, :, None]).reshape(b, A_HEADS * t, A_WIDTH)
    oa_rows = _moba_sample(pool_a, page_table, kmean, qa_rows, _pad_rows(r3(akv), SUBLANES), past, t)
    oa = oa_rows.reshape(b, A_HEADS, t, A_WIDTH).sum(axis=1)

    pool_b = nsa_pool.reshape(n_pool, PAGE_SIZE, 4 * B_KV_WIDTH)
    cmp_tok = _compress_sample(pool_b, page_table, w["cw"], past)
    eye_g = jnp.eye(B_KV_HEADS, dtype=F32)[None, None, :, None, :, None]

    def group_rows(a):
        a = a.reshape(b, t, B_KV_HEADS, B_GROUP, HEAD_DIM).transpose(0, 3, 2, 1, 4)
        return (a[:, :, :, :, None, :] * eye_g).reshape(b, B_HEADS * t, LANES)

    gate_rows = gates[:, :N_GATES].reshape(b, t, B_KV_HEADS, B_GROUP, 3).transpose(0, 3, 2, 1, 4)
    gate_rows = jnp.pad(gate_rows.reshape(b, B_HEADS * t, 3), ((0, 0), (0, 0), (0, LANES - 3)))
    win_rows = win_state.reshape(b, win_state.shape[1], 2 * B_KV_WIDTH)
    ob_rows = _nsa_sample(pool_b, page_table, cmp_tok, group_rows(bq), group_rows(bqr), gate_rows,
                          _pad_rows(r3(bkv)[:, :, 2 * B_KV_WIDTH:], SUBLANES), win_rows,
                          _pad_rows(r3(bwin), SUBLANES), past, t)
    ob = ob_rows.reshape(b, B_GROUP, B_KV_HEADS, t, B_KV_HEADS, HEAD_DIM).sum(axis=4)
    ob = ob.transpose(0, 3, 2, 1, 4).reshape(b * t, B_WIDTH)

    h = _merge(x2d, oa.reshape(-1, A_WIDTH), ob, ga, gb, w["p_a"], w["p_b"], w["w_o"])
    mem_rows = mem_kv.reshape(b, mem_kv.shape[1], 2 * C_WIDTH)
    y = _channel_and_norm(h, b, mem_rows, w, g_cross, g_ffn, sub_k1, sub_k2, g_final)
    win = jnp.concatenate([win_rows, r3(bwin)], axis=1)
    win = win[:, win.shape[1] - min(WINDOW, win.shape[1]):]
    return (y.reshape(b, t, D_MODEL), r3(akv).reshape(b, t, 2, A_HEADS, HEAD_DIM),
            r3(bkv).reshape(b, t, 4, B_KV_HEADS, HEAD_DIM), win.reshape(b, -1, 2, B_KV_HEADS, HEAD_DIM))


def kernel(x_prompt, x_sample, cache_moba_kv, cache_nsa_kv, state_nsa_win, cache_mem_kv, page_table, mem_prompt, g_attn, w_in, pe_cmp, w_ck1, w_ck2, w_cv1, w_cv2, p_a, p_b, w_o, g_cross, g_mem, w_cq, w_ckv, w_co, g_ffn, w_pq, sub_k1, sub_k2, peer_u, peer_v, g_final):
    assert g_attn.shape[0] == 1, "the final norm is fused into the last PEER step of a single layer"
    w = _layer_weights(w_in[0], pe_cmp[0], w_ck1[0], w_ck2[0], w_cv1[0], w_cv2[0], p_a[0], p_b[0], w_o[0], w_cq[0],
                       w_ckv[0], w_co[0], w_pq[0], peer_u[0], peer_v[0])
    y_p, moba_p, nsa_p, win_p, mem_p = _prompt_group(x_prompt, mem_prompt, w, g_attn[0], g_cross[0], g_mem[0],
                                                    g_ffn[0], sub_k1[0], sub_k2[0], g_final)
    y_s, moba_s, nsa_s, win_s = _sample_group(x_sample, cache_moba_kv[0], cache_nsa_kv[0], state_nsa_win[0],
                                              cache_mem_kv[0], page_table, w, g_attn[0], g_cross[0], g_ffn[0],
                                              sub_k1[0], sub_k2[0], g_final)
    return (y_p, y_s, moba_p[None], moba_s[None], nsa_p[None], nsa_s[None], win_p[None], win_s[None], mem_p[None])
```

```python
import functools

import numpy as np
import jax
import jax.numpy as jnp
from jax import lax
from jax.experimental import pallas as pl
from jax.experimental.pallas import tpu as pltpu

F32 = jnp.float32
BF16 = jnp.bfloat16
HIGHEST = lax.Precision.HIGHEST

LANES = 128
SUBLANES = 8
VMEM_LIMIT_BYTES = 56 * 1024 * 1024

D_MODEL = 1024
HEAD_DIM = 64
ROPE_THETA = 10000.0
NORM_EPS = 1e-6
NEG_BIG = -1e30
PAGE_SIZE = 128

A_HEADS = 8
A_BLOCK = 256
A_TOPK = 3
A_WIDTH = A_HEADS * HEAD_DIM

B_HEADS = 8
B_KV_HEADS = 2
B_GROUP = B_HEADS // B_KV_HEADS
B_WIDTH = B_HEADS * HEAD_DIM
B_KV_WIDTH = B_KV_HEADS * HEAD_DIM
CMP_LEN = 32
CMP_STRIDE = 16
CMP_HIDDEN = 64
SEL_BLOCK = 64
SEL_TOPN = 16
WINDOW = 512
FORCE_SCORE = 1e4

C_HEADS = 4
C_HEAD_DIM = 128
C_WIDTH = C_HEADS * C_HEAD_DIM

P_HEADS = 8
P_NKEYS = 128
P_QDIM = 256
P_TOPK = 16
P_PICKS = P_HEADS * P_TOPK

Q_BLOCK = 256
N_MAIN = 3 * A_WIDTH + B_WIDTH + 6 * B_KV_WIDTH
N_GATES = 3 * B_HEADS
N_PROJ = N_MAIN + LANES + 2 * D_MODEL


def _params(semantics):
    return pltpu.CompilerParams(dimension_semantics=semantics, vmem_limit_bytes=VMEM_LIMIT_BYTES)


def _lane_iota(shape, dtype=jnp.int32):
    return lax.broadcasted_iota(dtype, shape, len(shape) - 1)


def _row_iota(shape, dtype=jnp.int32):
    return lax.broadcasted_iota(dtype, shape, len(shape) - 2)


def _rms(x, g):
    return x * lax.rsqrt(jnp.mean(x * x, axis=-1, keepdims=True) + NORM_EPS) * g


def _sigmoid(x):
    return 1.0 / (1.0 + jnp.exp(-x))


def _dot_t(a, b, precision=None):
    return lax.dot_general(a, b, (((1,), (1,)), ((), ())), precision=precision, preferred_element_type=F32)


def _topk_mask(x, k, lane_f):
    sel = jnp.zeros(x.shape, F32)
    for _ in range(k):
        mx = jnp.max(x, axis=-1, keepdims=True)
        first = jnp.min(jnp.where(x == mx, lane_f, float(x.shape[-1])), axis=-1, keepdims=True)
        hit = lane_f == first
        sel = jnp.where(hit, 1.0, sel)
        x = jnp.where(hit, -jnp.inf, x)
    return sel


def _softmax_rows(s, mask):
    s = jnp.where(mask, s, NEG_BIG)
    m = jnp.max(s, axis=-1, keepdims=True)
    p = jnp.where(mask, jnp.exp(s - m), 0.0)
    return p, jnp.sum(p, axis=-1, keepdims=True)


def _safe_inv(l):
    return jnp.where(l > 0.0, 1.0 / jnp.where(l > 0.0, l, 1.0), 0.0)


def _proj_kernel(x_ref, g_ref, w_ref, cs_ref, sn_ref, aq_ref, akv_ref, bq_ref, bqr_ref, bkv_ref, bwin_ref,
                 gates_ref, ga_ref, gb_ref):
    ub = _rms(x_ref[...], g_ref[...]).astype(BF16)
    cs = cs_ref[...]
    sn = sn_ref[...]
    first_half = (_lane_iota((1, LANES)) % HEAD_DIM) < (HEAD_DIM // 2)

    def cols(c0, n):
        return jnp.dot(ub, w_ref[:, c0:c0 + n], preferred_element_type=F32)

    def rot(p):
        swapped = jnp.where(first_half, pltpu.roll(p, LANES - HEAD_DIM // 2, 1), pltpu.roll(p, HEAD_DIM // 2, 1))
        return p * cs + swapped * sn

    def rot_wide(p):
        return jnp.concatenate([rot(p[:, c:c + LANES]) for c in range(0, p.shape[1], LANES)], axis=1)

    aq_ref[...] = rot_wide(cols(0, A_WIDTH))
    akv_ref[:, 0:A_WIDTH] = rot_wide(cols(A_WIDTH, A_WIDTH))
    akv_ref[:, A_WIDTH:2 * A_WIDTH] = cols(2 * A_WIDTH, A_WIDTH)
    bq = cols(3 * A_WIDTH, B_WIDTH)
    bq_ref[...] = bq
    bqr_ref[...] = rot_wide(bq)
    c0 = 3 * A_WIDTH + B_WIDTH
    bkv = cols(c0, 4 * B_KV_WIDTH)
    bkv_ref[:, 0:2 * LANES] = bkv[:, 0:2 * LANES]
    bkv_ref[:, 2 * LANES:3 * LANES] = rot(bkv[:, 2 * LANES:3 * LANES])
    bkv_ref[:, 3 * LANES:4 * LANES] = bkv[:, 3 * LANES:4 * LANES]
    bwin = cols(c0 + 4 * B_KV_WIDTH, 2 * B_KV_WIDTH)
    bwin_ref[:, 0:LANES] = rot(bwin[:, 0:LANES])
    bwin_ref[:, LANES:2 * LANES] = bwin[:, LANES:2 * LANES]
    gates_ref[...] = cols(N_MAIN, LANES)
    ga_ref[...] = cols(N_MAIN + LANES, D_MODEL)
    gb_ref[...] = cols(N_MAIN + LANES + D_MODEL, D_MODEL)


def _rope_tables(pos):
    half = HEAD_DIM // 2
    inv_freq = ROPE_THETA ** (-jnp.arange(half, dtype=F32) / half)
    ang = pos.astype(F32)[:, None] * inv_freq[None, :]
    cos, sin = jnp.cos(ang), jnp.sin(ang)
    reps = LANES // HEAD_DIM
    return jnp.tile(jnp.concatenate([cos, cos], axis=1), (1, reps)), jnp.tile(jnp.concatenate([-sin, sin], axis=1), (1, reps))


def _projection(x2d, pos, seq, g_attn, w_proj):
    n = x2d.shape[0]
    tm = min(Q_BLOCK, n)
    cs, sn = _rope_tables(pos)
    if seq >= tm:
        tab_map = lambda i: (i % (seq // tm), 0)
    else:
        cs, sn = jnp.tile(cs, (tm // seq, 1)), jnp.tile(sn, (tm // seq, 1))
        tab_map = lambda i: (0, 0)
    widths = (A_WIDTH, 2 * A_WIDTH, B_WIDTH, B_WIDTH, 4 * B_KV_WIDTH, 2 * B_KV_WIDTH, LANES, D_MODEL, D_MODEL)
    row = lambda i: (i, 0)
    fixed = lambda i: (0, 0)
    return pl.pallas_call(
        _proj_kernel,
        grid=(n // tm,),
        in_specs=[pl.BlockSpec((tm, D_MODEL), row), pl.BlockSpec((1, D_MODEL), fixed),
                  pl.BlockSpec((D_MODEL, N_PROJ), fixed), pl.BlockSpec((tm, LANES), tab_map),
                  pl.BlockSpec((tm, LANES), tab_map)],
        out_specs=[pl.BlockSpec((tm, w), row) for w in widths],
        out_shape=[jax.ShapeDtypeStruct((n, w), F32) for w in widths],
        compiler_params=_params(("parallel",)),
    )(x2d, g_attn.reshape(1, D_MODEL), w_proj, cs, sn)


def _rms_matmul_kernel(x_ref, g_ref, w_ref, o_ref):
    o_ref[...] = jnp.dot(_rms(x_ref[...], g_ref[...]).astype(BF16), w_ref[...], preferred_element_type=F32)


def _rms_matmul(x2d, g, w_bf16):
    n, d = x2d.shape
    m = w_bf16.shape[1]
    tm = min(Q_BLOCK, n)
    return pl.pallas_call(
        _rms_matmul_kernel,
        grid=(n // tm,),
        in_specs=[pl.BlockSpec((tm, d), lambda i: (i, 0)), pl.BlockSpec((1, d), lambda i: (0, 0)),
                  pl.BlockSpec((d, m), lambda i: (0, 0))],
        out_specs=pl.BlockSpec((tm, m), lambda i: (i, 0)),
        out_shape=jax.ShapeDtypeStruct((n, m), F32),
        compiler_params=_params(("parallel",)),
    )(x2d, g.reshape(1, d), w_bf16)


def _moba_prompt_kernel(q_ref, k_ref, v_ref, o_ref, *, seq):
    nb = seq // A_BLOCK
    k = k_ref[0]
    kb = k.astype(BF16)
    vb = v_ref[0].astype(BF16)
    lane = _lane_iota((1, LANES))
    lane_f = lane.astype(F32)
    kmean = jnp.concatenate(
        [jnp.mean(k[j * A_BLOCK:(j + 1) * A_BLOCK], axis=0, keepdims=True) for j in range(nb)]
        + [jnp.zeros((LANES - nb, LANES), F32)], axis=0)
    expand = jnp.where(_row_iota((LANES, seq)) == _lane_iota((LANES, seq)) // A_BLOCK, 1.0, 0.0).astype(BF16)
    scale = HEAD_DIM ** -0.5
    for i in range(nb):
        qi = q_ref[0, i * Q_BLOCK:(i + 1) * Q_BLOCK, :]
        outs = []
        for hh in range(LANES // HEAD_DIM):
            head = (lane // HEAD_DIM) == hh
            qh = jnp.where(head, qi, 0.0)
            gate = _dot_t(qh, kmean, HIGHEST)
            gate = jnp.where(lane < i, gate, -jnp.inf)
            sel = _topk_mask(gate, min(A_TOPK, nb), lane_f)
            sel = jnp.where(lane < i, sel, 0.0)
            s = _dot_t((qh * scale).astype(BF16), kb[0:(i + 1) * A_BLOCK])
            chosen = jnp.dot(sel.astype(BF16), expand[:, 0:(i + 1) * A_BLOCK], preferred_element_type=F32) > 0.5
            key = _lane_iota((Q_BLOCK, (i + 1) * A_BLOCK)) - i * A_BLOCK
            own = (key >= 0) & (key <= _row_iota((Q_BLOCK, (i + 1) * A_BLOCK)))
            p, l = _softmax_rows(s, chosen | own)
            o = jnp.dot(p.astype(BF16), vb[0:(i + 1) * A_BLOCK], preferred_element_type=F32)
            outs.append(o * _safe_inv(l))
        o_ref[0, i * Q_BLOCK:(i + 1) * Q_BLOCK, :] = jnp.where((lane // HEAD_DIM) == 0, outs[0], outs[1])


def _moba_prompt(aq, akv):
    b, seq, _ = aq.shape
    hp = A_WIDTH // LANES
    return pl.pallas_call(
        functools.partial(_moba_prompt_kernel, seq=seq),
        grid=(b, hp),
        in_specs=[pl.BlockSpec((1, seq, LANES), lambda i, j: (i, 0, j)),
                  pl.BlockSpec((1, seq, LANES), lambda i, j: (i, 0, j)),
                  pl.BlockSpec((1, seq, LANES), lambda i, j: (i, 0, hp + j))],
        out_specs=pl.BlockSpec((1, seq, LANES), lambda i, j: (i, 0, j)),
        out_shape=jax.ShapeDtypeStruct((b, seq, A_WIDTH), F32),
        compiler_params=_params(("parallel", "parallel")),
    )(aq, akv, akv)


def _compress_rows(xk_ref, xv_ref, n_rows, pe_ref, wa_ref, wb_ref, w2_ref):
    acc_a = jnp.zeros((n_rows, 2 * LANES), F32)
    acc_b = jnp.zeros((n_rows, 2 * LANES), F32)
    for tt in range(CMP_STRIDE):
        rows_tt = pl.ds(tt, n_rows, stride=CMP_STRIDE)
        xt = jnp.concatenate([xk_ref[rows_tt, :], xv_ref[rows_tt, :]], axis=1)
        acc_a += jnp.dot((xt + pe_ref[tt:tt + 1, :]).astype(BF16), wa_ref[tt], preferred_element_type=F32)
        acc_b += jnp.dot((xt + pe_ref[CMP_STRIDE + tt:CMP_STRIDE + tt + 1, :]).astype(BF16), wb_ref[tt],
                         preferred_element_type=F32)
    hidden = acc_a + pltpu.roll(acc_b, n_rows - 1, 0)
    out = jnp.dot(jax.nn.gelu(hidden).astype(BF16), w2_ref[...], preferred_element_type=F32)
    return jnp.where(_row_iota(out.shape) < n_rows - 1, out, 0.0)


def _compress_prompt_kernel(xk_ref, xv_ref, pe_ref, wa_ref, wb_ref, w2_ref, o_ref, *, seq):
    o_ref[0] = _compress_rows(xk_ref.at[0], xv_ref.at[0], seq // CMP_STRIDE, pe_ref, wa_ref, wb_ref, w2_ref)


def _compress_weights(pe_cmp, w_ck1, w_ck2, w_cv1, w_cv2):
    def diag(mk, mv):
        z = jnp.zeros_like(mk)
        rows = [[mk, z, z, z], [z, mk, z, z], [z, z, mv, z], [z, z, z, mv]]
        return jnp.concatenate([jnp.concatenate(r, axis=-1) for r in rows], axis=-2)

    k1 = w_ck1.reshape(CMP_LEN, HEAD_DIM, CMP_HIDDEN)
    v1 = w_cv1.reshape(CMP_LEN, HEAD_DIM, CMP_HIDDEN)
    w1 = diag(k1, v1).astype(BF16)
    return jnp.tile(pe_cmp, (1, 4)), w1[:CMP_STRIDE], w1[CMP_STRIDE:], diag(w_ck2, w_cv2).astype(BF16)


def _compress_prompt(bkv, cw):
    b, seq, _ = bkv.shape
    n_rows = seq // CMP_STRIDE
    pe, wa, wb, w2 = cw
    full = lambda *s: pl.BlockSpec(s, lambda i: (0,) * len(s))
    return pl.pallas_call(
        functools.partial(_compress_prompt_kernel, seq=seq),
        grid=(b,),
        in_specs=[pl.BlockSpec((1, seq, LANES), lambda i: (i, 0, 0)), pl.BlockSpec((1, seq, LANES), lambda i: (i, 0, 1)),
                  full(CMP_LEN, 2 * LANES), full(CMP_STRIDE, 2 * LANES, 2 * LANES),
                  full(CMP_STRIDE, 2 * LANES, 2 * LANES), full(2 * LANES, 2 * LANES)],
        out_specs=pl.BlockSpec((1, n_rows, 2 * LANES), lambda i: (i, 0, 0)),
        out_shape=jax.ShapeDtypeStruct((b, n_rows, 2 * LANES), F32),
        compiler_params=_params(("parallel",)),
    )(bkv, bkv, pe, wa, wb, w2)


def _overlap_matrix(n_cmp, n_sel, rows, cols):
    c_start = np.arange(n_cmp)[:, None] * CMP_STRIDE
    s_start = np.arange(n_sel)[None, :] * SEL_BLOCK
    ov = np.clip(np.minimum(c_start + CMP_LEN, s_start + SEL_BLOCK) - np.maximum(c_start, s_start), 0, None)
    out = np.zeros((rows, cols), np.float32)
    out[:n_cmp, :n_sel] = ov / CMP_STRIDE
    return jnp.asarray(out)


def _nsa_prompt_kernel(q_ref, qr_ref, gates_ref, ksel_ref, vsel_ref, kwin_ref, vwin_ref, cmp_ref, ov_ref, o_ref,
                       sel_ref, m_ref, l_ref, acc_ref, *, seq):
    g = pl.program_id(1)
    i = pl.program_id(2)
    n_cmp = (seq - CMP_LEN) // CMP_STRIDE + 1
    rows = B_GROUP * Q_BLOCK
    scale = HEAD_DIM ** -0.5
    lane = _lane_iota((1, LANES))
    lane_f = lane.astype(F32)
    in_group = (lane // HEAD_DIM) == g

    def both_halves(x):
        xg = jnp.where(in_group, x, 0.0)
        return xg + pltpu.roll(xg, HEAD_DIM, 1)

    def stack_heads(ref):
        parts = []
        for j in range(B_GROUP):
            x = ref[0, pl.ds(pl.multiple_of(i * Q_BLOCK, Q_BLOCK), Q_BLOCK), (j // 2) * LANES:(j // 2 + 1) * LANES]
            parts.append(jnp.where((lane // HEAD_DIM) == (j % 2), x, 0.0) * scale)
        return jnp.concatenate(parts, axis=0).astype(BF16)

    q4 = stack_heads(q_ref)
    qr4 = stack_heads(qr_ref)
    qpos = i * Q_BLOCK + _row_iota((Q_BLOCK, 1))

    kc = both_halves(cmp_ref[0, :, 0:LANES]).astype(BF16)
    vc = both_halves(cmp_ref[0, :, LANES:2 * LANES]).astype(BF16)
    n_tok = kc.shape[0]
    tok = _lane_iota((1, n_tok))
    cmask = (tok < n_cmp) & (tok * CMP_STRIDE + (CMP_LEN - 1) <= qpos)
    s = _dot_t(q4, kc).reshape(B_GROUP, Q_BLOCK, n_tok)
    p, l = _softmax_rows(s, cmask[None])
    p_cmp = p * _safe_inv(l)
    o_cmp = jnp.dot(p_cmp.reshape(rows, n_tok).astype(BF16), vc, preferred_element_type=F32)
    imp = jnp.dot(jnp.sum(p_cmp, axis=0), ov_ref[...], precision=HIGHEST, preferred_element_type=F32)
    cur = qpos // SEL_BLOCK
    forced = (lane == 0) | (lane == cur) | (lane == cur - 1)
    score = jnp.where(lane <= cur, jnp.where(forced, FORCE_SCORE, imp), -jnp.inf)
    n_sel = -(-seq // SEL_BLOCK)
    sel = _topk_mask(score, min(SEL_TOPN, n_sel), lane_f)
    sel_ref[...] = jnp.where(lane <= cur, sel, 0.0)

    m_ref[...] = jnp.full(m_ref.shape, NEG_BIG, F32)
    l_ref[...] = jnp.zeros(l_ref.shape, F32)
    acc_ref[...] = jnp.zeros(acc_ref.shape, F32)
    blocks_per_tile = Q_BLOCK // SEL_BLOCK

    def sel_step(j, carry):
        start = pl.multiple_of(j * Q_BLOCK, Q_BLOCK)
        kj = both_halves(ksel_ref[0, pl.ds(start, Q_BLOCK), :]).astype(BF16)
        vj = both_halves(vsel_ref[0, pl.ds(start, Q_BLOCK), :]).astype(BF16)
        kpos = j * Q_BLOCK + _lane_iota((1, Q_BLOCK))
        blk = j * blocks_per_tile + _lane_iota((LANES, Q_BLOCK)) // SEL_BLOCK
        expand = jnp.where(_row_iota((LANES, Q_BLOCK)) == blk, 1.0, 0.0).astype(BF16)
        chosen = jnp.dot(sel_ref[...].astype(BF16), expand, preferred_element_type=F32) > 0.5
        mask = chosen & (kpos <= qpos)
        sj = jnp.where(mask[None], _dot_t(qr4, kj).reshape(B_GROUP, Q_BLOCK, Q_BLOCK), NEG_BIG)
        m_old = m_ref[...]
        m_new = jnp.maximum(m_old, jnp.max(sj, axis=-1, keepdims=True))
        pj = jnp.where(mask[None], jnp.exp(sj - m_new), 0.0)
        alpha = jnp.exp(m_old - m_new)
        l_ref[...] = alpha * l_ref[...] + jnp.sum(pj, axis=-1, keepdims=True)
        pv = jnp.dot(pj.reshape(rows, Q_BLOCK).astype(BF16), vj, preferred_element_type=F32)
        acc_ref[...] = alpha * acc_ref[...] + pv.reshape(B_GROUP, Q_BLOCK, LANES)
        m_ref[...] = m_new
        return carry

    lax.fori_loop(0, i + 1, sel_step, 0)
    o_sel = (acc_ref[...] * _safe_inv(l_ref[...])).reshape(rows, LANES)

    span = WINDOW + Q_BLOCK
    w0 = jnp.maximum(i * Q_BLOCK - WINDOW, 0)
    start = pl.multiple_of(w0, Q_BLOCK)
    kw = both_halves(kwin_ref[0, pl.ds(start, span), :]).astype(BF16)
    vw = both_halves(vwin_ref[0, pl.ds(start, span), :]).astype(BF16)
    kpos = w0 + _lane_iota((1, span))
    wmask = (kpos <= qpos) & (kpos > qpos - WINDOW)
    p, l = _softmax_rows(_dot_t(qr4, kw).reshape(B_GROUP, Q_BLOCK, span), wmask[None])
    o_win = jnp.dot(p.reshape(rows, span).astype(BF16), vw, preferred_element_type=F32)
    o_win = o_win * _safe_inv(l).reshape(rows, 1)

    gates = gates_ref[0]
    outs = []
    for j in range(B_GROUP):
        r = slice(j * Q_BLOCK, (j + 1) * Q_BLOCK)
        gt = [_sigmoid(jnp.sum(jnp.where(lane == (g * B_GROUP + j) * 3 + c, gates, 0.0), axis=-1, keepdims=True))
              for c in range(3)]
        outs.append(gt[0] * o_cmp[r] + gt[1] * o_sel[r] + gt[2] * o_win[r])
    left = (lane // HEAD_DIM) == 0
    o_ref[0] = jnp.concatenate([jnp.where(left, outs[0], outs[1]), jnp.where(left, outs[2], outs[3])], axis=1)


def _nsa_prompt(bq, bqr, gates, bkv, bwin, cmp_tok):
    b, seq, _ = bq.shape
    nq = seq // Q_BLOCK
    n_tok = cmp_tok.shape[1]
    n_cmp = (seq - CMP_LEN) // CMP_STRIDE + 1
    n_sel = -(-seq // SEL_BLOCK)
    assert n_sel <= LANES and seq >= WINDOW + Q_BLOCK
    overlap = _overlap_matrix(n_cmp, n_sel, n_tok, LANES)
    qspec = pl.BlockSpec((1, seq, 2 * LANES), lambda i, g, t: (i, 0, g))
    col = lambda c: pl.BlockSpec((1, seq, LANES), lambda i, g, t: (i, 0, c))
    return pl.pallas_call(
        functools.partial(_nsa_prompt_kernel, seq=seq),
        grid=(b, B_KV_HEADS, nq),
        in_specs=[qspec, qspec, pl.BlockSpec((1, Q_BLOCK, LANES), lambda i, g, t: (i, t, 0)),
                  col(2), col(3), col(0), col(1),
                  pl.BlockSpec((1, n_tok, 2 * LANES), lambda i, g, t: (i, 0, 0)),
                  pl.BlockSpec((n_tok, LANES), lambda i, g, t: (0, 0))],
        out_specs=pl.BlockSpec((1, Q_BLOCK, 2 * LANES), lambda i, g, t: (i, t, g)),
        out_shape=jax.ShapeDtypeStruct((b, seq, B_WIDTH), F32),
        scratch_shapes=[pltpu.VMEM((Q_BLOCK, LANES), F32), pltpu.VMEM((B_GROUP, Q_BLOCK, 1), F32),
                        pltpu.VMEM((B_GROUP, Q_BLOCK, 1), F32), pltpu.VMEM((B_GROUP, Q_BLOCK, LANES), F32)],
        compiler_params=_params(("parallel", "parallel", "arbitrary")),
    )(bq, bqr, gates, bkv, bkv, bwin, bwin, cmp_tok, overlap)


def _merge_kernel(x_ref, oa_ref, ob_ref, ga_ref, gb_ref, pa_ref, pb_ref, wo_ref, h_ref):
    ya = jnp.dot(oa_ref[...].astype(BF16), pa_ref[...], preferred_element_type=F32)
    yb = jnp.dot(ob_ref[...].astype(BF16), pb_ref[...], preferred_element_type=F32)
    mixed = _sigmoid(ga_ref[...]) * ya + _sigmoid(gb_ref[...]) * yb
    h_ref[...] = x_ref[...] + jnp.dot(mixed.astype(BF16), wo_ref[...], preferred_element_type=F32)


def _merge(x2d, oa, ob, ga, gb, pa, pb, wo):
    n = x2d.shape[0]
    tm = min(Q_BLOCK, n)
    row = lambda w: pl.BlockSpec((tm, w), lambda i: (i, 0))
    full = lambda a: pl.BlockSpec(a.shape, lambda i: (0, 0))
    return pl.pallas_call(
        _merge_kernel,
        grid=(n // tm,),
        in_specs=[row(D_MODEL), row(A_WIDTH), row(B_WIDTH), row(D_MODEL), row(D_MODEL), full(pa), full(pb), full(wo)],
        out_specs=row(D_MODEL),
        out_shape=jax.ShapeDtypeStruct((n, D_MODEL), F32),
        compiler_params=_params(("parallel",)),
    )(x2d, oa, ob, ga, gb, pa, pb, wo)


def _cross_kernel(h_ref, mem_ref, g_ref, wq_ref, wo_ref, o_ref):
    h = h_ref[0]
    q = jnp.dot(_rms(h, g_ref[...]).astype(BF16), wq_ref[...], preferred_element_type=F32)
    scale = C_HEAD_DIM ** -0.5
    outs = []
    for hd in range(C_HEADS):
        c = slice(hd * C_HEAD_DIM, (hd + 1) * C_HEAD_DIM)
        kh = mem_ref[0, :, c].astype(BF16)
        vh = mem_ref[0, :, C_WIDTH + hd * C_HEAD_DIM:C_WIDTH + (hd + 1) * C_HEAD_DIM].astype(BF16)
        s = _dot_t(q[:, c].astype(BF16), kh) * scale
        m = jnp.max(s, axis=-1, keepdims=True)
        p = jnp.exp(s - m)
        o = jnp.dot(p.astype(BF16), vh, preferred_element_type=F32)
        outs.append(o / jnp.sum(p, axis=-1, keepdims=True))
    att = jnp.concatenate(outs, axis=1).astype(BF16)
    o_ref[0] = h + jnp.dot(att, wo_ref[...], preferred_element_type=F32)


def _cross(h3, mem_kv, g_cross, wq, wo):
    b, t, _ = h3.shape
    tq = min(Q_BLOCK, t)
    mlen = mem_kv.shape[1]
    return pl.pallas_call(
        _cross_kernel,
        grid=(b, t // tq),
        in_specs=[pl.BlockSpec((1, tq, D_MODEL), lambda i, j: (i, j, 0)),
                  pl.BlockSpec((1, mlen, 2 * C_WIDTH), lambda i, j: (i, 0, 0)),
                  pl.BlockSpec((1, D_MODEL), lambda i, j: (0, 0)),
                  pl.BlockSpec(wq.shape, lambda i, j: (0, 0)), pl.BlockSpec(wo.shape, lambda i, j: (0, 0))],
        out_specs=pl.BlockSpec((1, tq, D_MODEL), lambda i, j: (i, j, 0)),
        out_shape=jax.ShapeDtypeStruct((b, t, D_MODEL), F32),
        compiler_params=_params(("parallel", "parallel")),
    )(h3, mem_kv, g_cross.reshape(1, D_MODEL), wq, wo)


def _peer_select_kernel(h_ref, g_ref, wq_ref, k1_ref, k2_ref, z_ref, eid_ref, gate_ref, s_ref, e_scr, w_scr, *, tm):
    z = _rms(h_ref[...], g_ref[...])
    z_ref[...] = z
    zb = z.astype(BF16)
    half = P_QDIM // 2
    k1 = k1_ref[...]
    k2 = k2_ref[...]
    for hd in range(P_HEADS):
        q = jnp.dot(zb, wq_ref[:, hd * P_QDIM:(hd + 1) * P_QDIM], preferred_element_type=F32)
        s1 = _dot_t(k1, q[:, 0:half], HIGHEST)
        s2 = _dot_t(k2, q[:, half:P_QDIM], HIGHEST)
        for tb in range(tm // LANES):
            s_ref[hd, tb, 0] = s1[:, tb * LANES:(tb + 1) * LANES]
            s_ref[hd, tb, 1] = s2[:, tb * LANES:(tb + 1) * LANES]

    n_cand = P_TOPK * P_TOPK
    key_f = _row_iota((P_NKEYS, LANES)).astype(F32)
    cand_f = _row_iota((n_cand, LANES)).astype(F32)

    def take_top(x, ids_f, limit):
        mx = jnp.max(x, axis=0, keepdims=True)
        first = jnp.min(jnp.where(x == mx, ids_f, limit), axis=0, keepdims=True)
        hit = ids_f == first
        return mx, first, hit, jnp.where(hit, -jnp.inf, x)

    def select(x1, x2):
        v1, i1, v2, i2 = [], [], [], []
        for _ in range(P_TOPK):
            m, a, _, x1 = take_top(x1, key_f, float(P_NKEYS))
            v1.append(m)
            i1.append(a)
            m, a, _, x2 = take_top(x2, key_f, float(P_NKEYS))
            v2.append(m)
            i2.append(a)
        v2s = jnp.concatenate(v2, axis=0)
        i2s = jnp.concatenate(i2, axis=0)
        cand = jnp.concatenate([v1[a] + v2s for a in range(P_TOPK)], axis=0)
        cid = jnp.concatenate([i1[a] * float(P_NKEYS) + i2s for a in range(P_TOPK)], axis=0)
        es, ss = [], []
        for _ in range(P_TOPK):
            mx, _, hit, cand = take_top(cand, cand_f, float(n_cand))
            es.append(jnp.max(jnp.where(hit, cid, -1.0), axis=0, keepdims=True))
            ss.append(mx)
        ex = jnp.exp(jnp.concatenate(ss, axis=0) - ss[0])
        return jnp.concatenate(es, axis=0), ex / jnp.sum(ex, axis=0, keepdims=True)

    def step(hd, carry):
        rows = pl.ds(pl.multiple_of(hd * P_TOPK, P_TOPK), P_TOPK)
        for tb in range(tm // LANES):
            e16, w16 = select(s_ref[hd, tb, 0], s_ref[hd, tb, 1])
            e_scr[tb, rows, :] = e16
            w_scr[tb, rows, :] = w16
        return carry

    lax.fori_loop(0, P_HEADS, step, 0)
    for tb in range(tm // LANES):
        eid_ref[tb * LANES:(tb + 1) * LANES, :] = jnp.transpose(e_scr[tb]).astype(jnp.int32)
        gate_ref[tb * LANES:(tb + 1) * LANES, :] = jnp.transpose(w_scr[tb])


def _peer_select(h2d, g_ffn, wq, k1, k2):
    n = h2d.shape[0]
    tm = min(Q_BLOCK, n)
    assert tm % LANES == 0 and n % tm == 0
    row = lambda: pl.BlockSpec((tm, D_MODEL), lambda i: (i, 0))
    pick = lambda: pl.BlockSpec((tm, LANES), lambda i: (i, 0))
    full = lambda a: pl.BlockSpec(a.shape, lambda i: (0, 0))
    return pl.pallas_call(
        functools.partial(_peer_select_kernel, tm=tm),
        grid=(n // tm,),
        in_specs=[row(), pl.BlockSpec((1, D_MODEL), lambda i: (0, 0)), full(wq), full(k1), full(k2)],
        out_specs=[row(), pick(), pick()],
        out_shape=[jax.ShapeDtypeStruct((n, D_MODEL), F32), jax.ShapeDtypeStruct((n, LANES), jnp.int32),
                   jax.ShapeDtypeStruct((n, LANES), F32)],
        scratch_shapes=[pltpu.VMEM((P_HEADS, tm // LANES, 2, P_NKEYS, LANES), F32),
                        pltpu.VMEM((tm // LANES, P_PICKS, LANES), F32), pltpu.VMEM((tm // LANES, P_PICKS, LANES), F32)],
        compiler_params=_params(("parallel",)),
    )(h2d, g_ffn.reshape(1, D_MODEL), wq, k1, k2)


PEER_TOKENS = 8
ROW_TILES = D_MODEL // LANES


def _peer_gather_kernel(eid_ref, h_ref, z_ref, gate_ref, gfin_ref, uv_hbm, y_ref, buf, sem):
    j = pl.program_id(0)
    n_tiles = pl.num_programs(0) - 1
    rows = PEER_TOKENS * P_PICKS

    for to_slot in range(2):
        @pl.when((j < n_tiles) & (j % 2 == to_slot))
        def _(to_slot=to_slot):
            for r in range(rows):
                e = eid_ref[r // P_PICKS, r % P_PICKS]
                pltpu.make_async_copy(uv_hbm.at[e], buf.at[to_slot, pl.ds(r, 1)], sem.at[to_slot]).start()

    @pl.when(j > 0)
    def _():
        slot = (j - 1) % 2
        pltpu.make_async_copy(uv_hbm.at[pl.ds(0, rows), 0], buf.at[slot], sem.at[slot]).wait()
        gate_t = jnp.transpose(jnp.concatenate(
            [gate_ref[...], jnp.zeros((LANES - PEER_TOKENS, P_PICKS), F32)], axis=0))
        for p in range(PEER_TOKENS):
            picks = pl.ds(p * P_PICKS, P_PICKS)
            acc = jnp.zeros((P_PICKS, LANES), F32)
            for s in range(ROW_TILES):
                acc += buf[slot, picks, s * LANES:(s + 1) * LANES] * z_ref[p:p + 1, s * LANES:(s + 1) * LANES]
            act = jax.nn.gelu(jnp.sum(acc, axis=-1, keepdims=True))
            coef = jnp.broadcast_to(gate_t[:, p:p + 1] * act, (P_PICKS, LANES))
            outs = []
            for s in range(ROW_TILES):
                v_s = buf[slot, picks, D_MODEL + s * LANES:D_MODEL + (s + 1) * LANES]
                outs.append(jnp.sum(coef * v_s, axis=0, keepdims=True))
            y_ref[p:p + 1, :] = h_ref[p:p + 1, :] + jnp.concatenate(outs, axis=1)
        y_ref[...] = _rms(y_ref[...], gfin_ref[...])


def _peer_gather(eid, h2d, z, gate, g_final, uv):
    n = h2d.shape[0]
    tiles = n // PEER_TOKENS
    ahead = lambda j: (jnp.minimum(j, tiles - 1), 0)
    behind = lambda j: (jnp.maximum(j - 1, 0), 0)
    row = lambda w: pl.BlockSpec((PEER_TOKENS, w), behind)
    return pl.pallas_call(
        _peer_gather_kernel,
        grid=(tiles + 1,),
        in_specs=[pl.BlockSpec((PEER_TOKENS, P_PICKS), ahead, memory_space=pltpu.SMEM),
                  row(D_MODEL), row(D_MODEL), row(P_PICKS), pl.BlockSpec((1, D_MODEL), lambda j: (0, 0)),
                  pl.BlockSpec(memory_space=pl.ANY)],
        out_specs=row(D_MODEL),
        out_shape=jax.ShapeDtypeStruct((n, D_MODEL), F32),
        scratch_shapes=[pltpu.VMEM((2, PEER_TOKENS * P_PICKS, 2 * D_MODEL), F32),
                        pltpu.SemaphoreType.DMA((2,))],
        compiler_params=_params(("arbitrary",)),
    )(eid, h2d, z, gate, g_final.reshape(1, D_MODEL), uv)


PAGES_PER_STEP = 16
SAMPLE_ROWS = LANES


def _page_specs(rows, row_block):
    def spec(k):
        return pl.BlockSpec((1, rows, PAGE_SIZE), lambda b, c, pt: (pt[b, c * PAGES_PER_STEP + k], row_block, 0))
    return [spec(k) for k in range(PAGES_PER_STEP)]


def _online_step(s, mask, pv_fn, m_ref, l_ref, acc_ref):
    s = jnp.where(mask, s, NEG_BIG)
    m_old = m_ref[...]
    m_new = jnp.maximum(m_old, jnp.max(s, axis=-1, keepdims=True))
    p = jnp.where(mask, jnp.exp(s - m_new), 0.0)
    alpha = jnp.exp(m_old - m_new)
    l_ref[...] = alpha * l_ref[...] + jnp.sum(p, axis=-1, keepdims=True)
    acc_ref[...] = alpha * acc_ref[...] + pv_fn(p.astype(BF16))
    m_ref[...] = m_new


def _online_update(s, mask, v, m_ref, l_ref, acc_ref):
    _online_step(s, mask, lambda p: jnp.dot(p, v, preferred_element_type=F32), m_ref, l_ref, acc_ref)


def _online_update_t(s, mask, v_t, m_ref, l_ref, acc_ref):
    _online_step(s, mask, lambda p: jnp.transpose(_dot_t(v_t, p)), m_ref, l_ref, acc_ref)


def _moba_kmean_kernel(pt_ref, *refs):
    pages, o_ref = refs[:PAGES_PER_STEP], refs[PAGES_PER_STEP]
    c = pl.program_id(1)

    @pl.when(c == 0)
    def _():
        o_ref[...] = jnp.zeros(o_ref.shape, F32)

    n_keys = PAGES_PER_STEP * PAGE_SIZE
    n_lanes = o_ref.shape[2]
    k_t = jnp.concatenate([p[0] for p in pages], axis=1)
    blk = c * (n_keys // A_BLOCK) + _row_iota((n_keys, n_lanes)) // A_BLOCK
    avg = jnp.where(_lane_iota((n_keys, n_lanes)) == blk, 1.0 / A_BLOCK, 0.0)
    o_ref[0] += jnp.dot(k_t, avg, precision=HIGHEST, preferred_element_type=F32)


def _moba_kmean(pool_t, page_table, n_lanes):
    b, n_pages = page_table.shape
    return pl.pallas_call(
        _moba_kmean_kernel,
        grid_spec=pltpu.PrefetchScalarGridSpec(
            num_scalar_prefetch=1, grid=(b, n_pages // PAGES_PER_STEP),
            in_specs=_page_specs(A_WIDTH, 0),
            out_specs=pl.BlockSpec((1, A_WIDTH, n_lanes), lambda i, c, pt: (i, 0, 0))),
        out_shape=jax.ShapeDtypeStruct((b, A_WIDTH, n_lanes), F32),
        compiler_params=_params(("parallel", "arbitrary")),
    )(page_table, *([pool_t] * PAGES_PER_STEP))


def _moba_sample_kernel(pt_ref, *refs, past, t_new):
    pages = refs[:PAGES_PER_STEP]
    kmean_ref, q_ref, new_ref, o_ref, sel_ref, m_ref, l_ref, acc_ref = refs[PAGES_PER_STEP:]
    c = pl.program_id(1)
    rows = A_HEADS * t_new
    n_lanes = sel_ref.shape[1]
    cur = past // A_BLOCK
    q = q_ref[0]
    qb = (q * HEAD_DIM ** -0.5).astype(BF16)

    @pl.when(c == 0)
    def _():
        lane = _lane_iota((1, n_lanes))
        gate = jnp.dot(q, kmean_ref[0], precision=HIGHEST, preferred_element_type=F32)
        gate = jnp.where(lane < cur, gate, -jnp.inf)
        sel = _topk_mask(gate, min(A_TOPK, cur + 1), lane.astype(F32))
        sel_ref[...] = jnp.where(lane < cur, sel, 0.0)
        m_ref[...] = jnp.full(m_ref.shape, NEG_BIG, F32)
        l_ref[...] = jnp.zeros(l_ref.shape, F32)
        acc_ref[...] = jnp.zeros(acc_ref.shape, F32)

    n_keys = PAGES_PER_STEP * PAGE_SIZE
    k_t = jnp.concatenate([p[0, 0:A_WIDTH, :] for p in pages], axis=1).astype(BF16)
    v_t = jnp.concatenate([p[0, A_WIDTH:2 * A_WIDTH, :] for p in pages], axis=1).astype(BF16)
    blk = c * (n_keys // A_BLOCK) + _lane_iota((n_lanes, n_keys)) // A_BLOCK
    expand = jnp.where(_row_iota((n_lanes, n_keys)) == blk, 1.0, 0.0).astype(BF16)
    chosen = jnp.dot(sel_ref[...].astype(BF16), expand, preferred_element_type=F32) > 0.5
    _online_update_t(jnp.dot(qb, k_t, preferred_element_type=F32), chosen, v_t, m_ref, l_ref, acc_ref)

    @pl.when(c == pl.num_programs(1) - 1)
    def _():
        kn = new_ref[0, :, 0:A_WIDTH].astype(BF16)
        vn = new_ref[0, :, A_WIDTH:2 * A_WIDTH].astype(BF16)
        t_key = _lane_iota((SAMPLE_ROWS, kn.shape[0]))
        t_row = _row_iota((SAMPLE_ROWS, kn.shape[0])) % t_new
        _online_update(_dot_t(qb, kn), (t_key <= t_row) & (t_key < t_new), vn, m_ref, l_ref, acc_ref)
        own = (_lane_iota((SAMPLE_ROWS, A_WIDTH)) // HEAD_DIM) == (_row_iota((SAMPLE_ROWS, A_WIDTH)) // t_new)
        o_ref[0] = jnp.where(own, acc_ref[...] * _safe_inv(l_ref[...]), 0.0)[0:rows]


def _moba_sample(pool_t, page_table, kmean_t, q_rows, new_kv, past, t_new):
    b, n_pages = page_table.shape
    rows = A_HEADS * t_new
    n_lanes = kmean_t.shape[2]
    per_b = lambda s: pl.BlockSpec((1,) + s, lambda i, c, pt: (i, 0, 0))
    return pl.pallas_call(
        functools.partial(_moba_sample_kernel, past=past, t_new=t_new),
        grid_spec=pltpu.PrefetchScalarGridSpec(
            num_scalar_prefetch=1, grid=(b, n_pages // PAGES_PER_STEP),
            in_specs=_page_specs(2 * A_WIDTH, 0) + [per_b(kmean_t.shape[1:]), per_b(q_rows.shape[1:]),
                                                     per_b(new_kv.shape[1:])],
            out_specs=per_b((rows, A_WIDTH)),
            scratch_shapes=[pltpu.VMEM((SAMPLE_ROWS, n_lanes), F32), pltpu.VMEM((SAMPLE_ROWS, 1), F32),
                            pltpu.VMEM((SAMPLE_ROWS, 1), F32), pltpu.VMEM((SAMPLE_ROWS, A_WIDTH), F32)]),
        out_shape=jax.ShapeDtypeStruct((b, rows, A_WIDTH), F32),
        compiler_params=_params(("parallel", "arbitrary")),
    )(page_table, *([pool_t] * PAGES_PER_STEP), kmean_t, q_rows, new_kv)


def _compress_sample_kernel(pt_ref, *refs, past):
    pages = refs[:PAGES_PER_STEP]
    pe_ref, wa_ref, wb_ref, w2_ref, o_ref, xk_ref, xv_ref = refs[PAGES_PER_STEP:]
    c = pl.program_id(1)
    for k, page in enumerate(pages):
        start = pl.multiple_of((c * PAGES_PER_STEP + k) * PAGE_SIZE, PAGE_SIZE)
        xk_ref[pl.ds(start, PAGE_SIZE), :] = jnp.transpose(page[0, 0:LANES, :])
        xv_ref[pl.ds(start, PAGE_SIZE), :] = jnp.transpose(page[0, LANES:2 * LANES, :])

    @pl.when(c == pl.num_programs(1) - 1)
    def _():
        o_ref[0] = _compress_rows(xk_ref, xv_ref, past // CMP_STRIDE, pe_ref, wa_ref, wb_ref, w2_ref)


def _compress_sample(pool, page_table, cw, past):
    b, n_pages = page_table.shape
    n_rows = past // CMP_STRIDE
    pe, wa, wb, w2 = cw
    full = lambda a: pl.BlockSpec(a.shape, lambda i, c, pt: (0,) * a.ndim)
    return pl.pallas_call(
        functools.partial(_compress_sample_kernel, past=past),
        grid_spec=pltpu.PrefetchScalarGridSpec(
            num_scalar_prefetch=1, grid=(b, n_pages // PAGES_PER_STEP),
            in_specs=_page_specs(2 * LANES, 0) + [full(pe), full(wa), full(wb), full(w2)],
            out_specs=pl.BlockSpec((1, n_rows, 2 * LANES), lambda i, c, pt: (i, 0, 0)),
            scratch_shapes=[pltpu.VMEM((past, LANES), F32), pltpu.VMEM((past, LANES), F32)]),
        out_shape=jax.ShapeDtypeStruct((b, n_rows, 2 * LANES), F32),
        compiler_params=_params(("parallel", "arbitrary")),
    )(page_table, *([pool] * PAGES_PER_STEP), pe, wa, wb, w2)


def _nsa_sample_kernel(pt_ref, *refs, past, t_new):
    pages = refs[:PAGES_PER_STEP]
    (cmp_ref, q_ref, qr_ref, gates_ref, newsel_ref, win_ref, newwin_ref, ov_ref, o_ref,
     kv_ref, m_ref, l_ref, acc_ref) = refs[PAGES_PER_STEP:]
    c = pl.program_id(1)
    for k, page in enumerate(pages):
        kv_ref[c * PAGES_PER_STEP + k] = page[0]

    @pl.when(c == pl.num_programs(1) - 1)
    def _():
        rows = B_HEADS * t_new
        slab = B_KV_HEADS * t_new
        scale = HEAD_DIM ** -0.5
        qb = (q_ref[0] * scale).astype(BF16)
        qrb = (qr_ref[0] * scale).astype(BF16)
        pos = past + _row_iota((SAMPLE_ROWS, 1)) % t_new

        n_tok = cmp_ref.shape[1]
        n_cmp = n_tok - 1
        kc = cmp_ref[0, :, 0:LANES].astype(BF16)
        vc = cmp_ref[0, :, LANES:2 * LANES].astype(BF16)
        tok = _lane_iota((1, n_tok))
        p, l = _softmax_rows(_dot_t(qb, kc), (tok < n_cmp) & (tok * CMP_STRIDE + (CMP_LEN - 1) <= pos))
        p_cmp = p * _safe_inv(l)
        o_cmp = jnp.dot(p_cmp.astype(BF16), vc, preferred_element_type=F32)
        p_sum = sum(p_cmp[j * slab:(j + 1) * slab] for j in range(B_GROUP))
        imp = jnp.dot(p_sum, ov_ref[...], precision=HIGHEST, preferred_element_type=F32)
        n_lanes = ov_ref.shape[1]
        lane = _lane_iota((1, n_lanes))
        cur = past // SEL_BLOCK
        forced = (lane == 0) | (lane == cur) | (lane == cur - 1)
        score = jnp.where(lane <= cur, jnp.where(forced, FORCE_SCORE, imp), -jnp.inf)
        sel = _topk_mask(score, min(SEL_TOPN, cur + 1), lane.astype(F32))
        sel = jnp.where(lane <= cur, sel, 0.0)
        sel_b = sel.astype(BF16)

        m_ref[...] = jnp.full(m_ref.shape, NEG_BIG, F32)
        l_ref[...] = jnp.zeros(l_ref.shape, F32)
        acc_ref[...] = jnp.zeros(acc_ref.shape, F32)
        n_keys = PAGES_PER_STEP * PAGE_SIZE
        tile_rows = lambda a: jnp.concatenate([a] * (SAMPLE_ROWS // slab), axis=0)
        for ci in range(past // n_keys):
            pages_ci = range(ci * PAGES_PER_STEP, (ci + 1) * PAGES_PER_STEP)
            ks_t = jnp.concatenate([kv_ref[pg, 0:LANES, :] for pg in pages_ci], axis=1).astype(BF16)
            vs_t = jnp.concatenate([kv_ref[pg, LANES:2 * LANES, :] for pg in pages_ci], axis=1).astype(BF16)
            blk = (ci * n_keys + _lane_iota((n_lanes, n_keys))) // SEL_BLOCK
            expand = jnp.where(_row_iota((n_lanes, n_keys)) == blk, 1.0, 0.0).astype(BF16)
            chosen = jnp.dot(sel_b, expand, preferred_element_type=F32) > 0.5
            _online_update_t(jnp.dot(qrb, ks_t, preferred_element_type=F32), tile_rows(chosen), vs_t,
                             m_ref, l_ref, acc_ref)
        kn = newsel_ref[0, :, 0:LANES].astype(BF16)
        vn = newsel_ref[0, :, LANES:2 * LANES].astype(BF16)
        t_key = _lane_iota((SAMPLE_ROWS, kn.shape[0]))
        cur_chosen = tile_rows(jnp.sum(jnp.where(lane == cur, sel, 0.0), axis=-1, keepdims=True)) > 0.5
        _online_update(_dot_t(qrb, kn), cur_chosen & (past + t_key <= pos) & (t_key < t_new), vn,
                       m_ref, l_ref, acc_ref)
        o_sel = acc_ref[...] * _safe_inv(l_ref[...])

        n_win = win_ref.shape[1]
        kw = jnp.concatenate([win_ref[0, :, 0:LANES], newwin_ref[0, :, 0:LANES]], axis=0).astype(BF16)
        vw = jnp.concatenate([win_ref[0, :, LANES:2 * LANES], newwin_ref[0, :, LANES:2 * LANES]], axis=0).astype(BF16)
        wpos = past - n_win + _lane_iota((1, kw.shape[0]))
        p, l = _softmax_rows(_dot_t(qrb, kw), (wpos <= pos) & (wpos > pos - WINDOW))
        o_win = jnp.dot(p.astype(BF16), vw, preferred_element_type=F32) * _safe_inv(l)

        gt = _sigmoid(gates_ref[0])
        o = gt[:, 0:1] * o_cmp + gt[:, 1:2] * o_sel + gt[:, 2:3] * o_win
        shape = (SAMPLE_ROWS, LANES)
        own = (_lane_iota(shape) // HEAD_DIM) == ((_row_iota(shape) // t_new) % B_KV_HEADS)
        o_ref[0] = jnp.where(own, o, 0.0)[0:rows]


def _nsa_sample(pool_t, page_table, cmp_tok, q_rows, qr_rows, gate_rows, new_sel, win_state, new_win, past, t_new):
    b, n_pages = page_table.shape
    rows = B_HEADS * t_new
    n_tok = cmp_tok.shape[1]
    n_sel = past // SEL_BLOCK + 1
    n_lanes = -(-n_sel // LANES) * LANES
    overlap = _overlap_matrix(n_tok - 1, n_sel, n_tok, n_lanes)
    per_b = lambda a: pl.BlockSpec((1,) + a.shape[1:], lambda i, c, pt: (i, 0, 0))
    return pl.pallas_call(
        functools.partial(_nsa_sample_kernel, past=past, t_new=t_new),
        grid_spec=pltpu.PrefetchScalarGridSpec(
            num_scalar_prefetch=1, grid=(b, n_pages // PAGES_PER_STEP),
            in_specs=_page_specs(2 * LANES, 1) + [per_b(cmp_tok), per_b(q_rows), per_b(qr_rows), per_b(gate_rows),
                                                  per_b(new_sel), per_b(win_state), per_b(new_win),
                                                  pl.BlockSpec(overlap.shape, lambda i, c, pt: (0, 0))],
            out_specs=pl.BlockSpec((1, rows, LANES), lambda i, c, pt: (i, 0, 0)),
            scratch_shapes=[pltpu.VMEM((n_pages, 2 * LANES, PAGE_SIZE), F32), pltpu.VMEM((SAMPLE_ROWS, 1), F32),
                            pltpu.VMEM((SAMPLE_ROWS, 1), F32), pltpu.VMEM((SAMPLE_ROWS, LANES), F32)]),
        out_shape=jax.ShapeDtypeStruct((b, rows, LANES), F32),
        compiler_params=_params(("parallel", "arbitrary")),
    )(page_table, *([pool_t] * PAGES_PER_STEP), cmp_tok, q_rows, qr_rows, gate_rows, new_sel, win_state, new_win,
      overlap)


def _layer_weights(w_in, pe_cmp, w_ck1, w_ck2, w_cv1, w_cv2, p_a, p_b, w_o, w_cq, w_ckv, w_co, w_pq, peer_u, peer_v):
    w_proj = jnp.concatenate(
        [w_in[:, :N_MAIN], jnp.pad(w_in[:, N_MAIN:N_MAIN + N_GATES], ((0, 0), (0, LANES - N_GATES))),
         w_in[:, N_MAIN + N_GATES:]], axis=1).astype(BF16)
    uv = jnp.concatenate([peer_u, peer_v], axis=1)[:, None, :]
    return dict(w_proj=w_proj, cw=_compress_weights(pe_cmp, w_ck1, w_ck2, w_cv1, w_cv2),
                p_a=p_a.astype(BF16), p_b=p_b.astype(BF16), w_o=w_o.astype(BF16), w_cq=w_cq.astype(BF16),
                w_ckv=w_ckv.astype(BF16), w_co=w_co.astype(BF16), w_pq=w_pq.astype(BF16),
                uv=uv)


def _channel_and_norm(h2d, batch, mem_kv, w, g_cross, g_ffn, sub_k1, sub_k2, g_final):
    h3 = _cross(h2d.reshape(batch, -1, D_MODEL), mem_kv, g_cross, w["w_cq"], w["w_co"])
    h2 = h3.reshape(-1, D_MODEL)
    z, eid, gate = _peer_select(h2, g_ffn, w["w_pq"], sub_k1, sub_k2)
    return _peer_gather(eid, h2, z, gate, g_final, w["uv"])


def _prompt_group(x, mem, w, g_attn, g_cross, g_mem, g_ffn, sub_k1, sub_k2, g_final):
    b, seq, _ = x.shape
    x2d = x.reshape(b * seq, D_MODEL)
    aq, akv, bq, bqr, bkv, bwin, gates, ga, gb = _projection(x2d, jnp.arange(seq, dtype=jnp.int32), seq, g_attn,
                                                             w["w_proj"])
    r3 = lambda a: a.reshape(b, seq, a.shape[-1])
    oa = _moba_prompt(r3(aq), r3(akv))
    cmp_tok = _compress_prompt(r3(bkv), w["cw"])
    ob = _nsa_prompt(r3(bq), r3(bqr), r3(gates), r3(bkv), r3(bwin), cmp_tok)
    h = _merge(x2d, oa.reshape(-1, A_WIDTH), ob.reshape(-1, B_WIDTH), ga, gb, w["p_a"], w["p_b"], w["w_o"])
    mlen = mem.shape[1]
    mem_kv = _rms_matmul(mem.reshape(b * mlen, D_MODEL), g_mem, w["w_ckv"]).reshape(b, mlen, 2 * C_WIDTH)
    y = _channel_and_norm(h, b, mem_kv, w, g_cross, g_ffn, sub_k1, sub_k2, g_final)
    win = r3(bwin)[:, seq - min(WINDOW, seq):]
    return (y.reshape(b, seq, D_MODEL), r3(akv).reshape(b, seq, 2, A_HEADS, HEAD_DIM),
            r3(bkv).reshape(b, seq, 4, B_KV_HEADS, HEAD_DIM), win.reshape(b, -1, 2, B_KV_HEADS, HEAD_DIM),
            mem_kv.reshape(b, mlen, 2, C_HEADS, C_HEAD_DIM))


def _pad_rows(a, rows):
    return jnp.pad(a, ((0, 0), (0, rows - a.shape[1]), (0, 0)))


def _sample_group(x, moba_pool, nsa_pool, win_state, mem_kv, page_table, w, g_attn, g_cross, g_ffn, sub_k1, sub_k2,
                  g_final):
    b, t, _ = x.shape
    past = page_table.shape[1] * PAGE_SIZE
    assert t * B_KV_HEADS == SUBLANES and past % (PAGES_PER_STEP * PAGE_SIZE) == 0 and B_HEADS * t <= SAMPLE_ROWS
    x2d = x.reshape(b * t, D_MODEL)
    pos = past + jnp.arange(t, dtype=jnp.int32)
    aq, akv, bq, bqr, bkv, bwin, gates, ga, gb = _projection(x2d, pos, t, g_attn, w["w_proj"])
    r3 = lambda a: a.reshape(b, t, a.shape[-1])

    n_pool = moba_pool.shape[0]
    pool_a = moba_pool.transpose(0, 2, 3, 4, 1).reshape(n_pool, 2 * A_WIDTH, PAGE_SIZE)
    pool_b = nsa_pool.transpose(0, 2, 3, 4, 1).reshape(n_pool, 4 * B_KV_WIDTH, PAGE_SIZE)

    n_blocks = past // A_BLOCK
    kmean_t = _moba_kmean(pool_a, page_table, -(-n_blocks // LANES) * LANES)
    qa = aq.reshape(b, t, A_HEADS, HEAD_DIM).transpose(0, 2, 1, 3)
    qa_rows = (qa[:, :, :, None, :] * jnp.eye(A_HEADS, dtype=F32)[None, :, None, :, None]).reshape(b, A_HEADS * t, A_WIDTH)
    oa_rows = _moba_sample(pool_a, page_table, kmean_t, _pad_rows(qa_rows, SAMPLE_ROWS),
                           _pad_rows(r3(akv), SUBLANES), past, t)
    oa = oa_rows.reshape(b, A_HEADS, t, A_WIDTH).sum(axis=1)

    cmp_tok = _compress_sample(pool_b, page_table, w["cw"], past)
    eye_g = jnp.eye(B_KV_HEADS, dtype=F32)[None, None, :, None, :, None]

    def group_rows(a):
        a = a.reshape(b, t, B_KV_HEADS, B_GROUP, HEAD_DIM).transpose(0, 3, 2, 1, 4)
        return _pad_rows((a[:, :, :, :, None, :] * eye_g).reshape(b, B_HEADS * t, LANES), SAMPLE_ROWS)

    gate_rows = gates[:, :N_GATES].reshape(b, t, B_KV_HEADS, B_GROUP, 3).transpose(0, 3, 2, 1, 4)
    gate_rows = jnp.pad(gate_rows.reshape(b, B_HEADS * t, 3), ((0, 0), (0, SAMPLE_ROWS - B_HEADS * t), (0, LANES - 3)))
    win_rows = win_state.reshape(b, win_state.shape[1], 2 * B_KV_WIDTH)
    ob_rows = _nsa_sample(pool_b, page_table, cmp_tok, group_rows(bq), group_rows(bqr), gate_rows,
                          _pad_rows(r3(bkv)[:, :, 2 * B_KV_WIDTH:], SUBLANES), win_rows,
                          _pad_rows(r3(bwin), SUBLANES), past, t)
    ob = ob_rows.reshape(b, B_GROUP, B_KV_HEADS, t, B_KV_HEADS, HEAD_DIM).sum(axis=4)
    ob = ob.transpose(0, 3, 2, 1, 4).reshape(b * t, B_WIDTH)

    h = _merge(x2d, oa.reshape(-1, A_WIDTH), ob, ga, gb, w["p_a"], w["p_b"], w["w_o"])
    mem_rows = mem_kv.reshape(b, mem_kv.shape[1], 2 * C_WIDTH)
    y = _channel_and_norm(h, b, mem_rows, w, g_cross, g_ffn, sub_k1, sub_k2, g_final)
    win = jnp.concatenate([win_rows, r3(bwin)], axis=1)
    win = win[:, win.shape[1] - min(WINDOW, win.shape[1]):]
    return (y.reshape(b, t, D_MODEL), r3(akv).reshape(b, t, 2, A_HEADS, HEAD_DIM),
            r3(bkv).reshape(b, t, 4, B_KV_HEADS, HEAD_DIM), win.reshape(b, -1, 2, B_KV_HEADS, HEAD_DIM))


def kernel(x_prompt, x_sample, cache_moba_kv, cache_nsa_kv, state_nsa_win, cache_mem_kv, page_table, mem_prompt, g_attn, w_in, pe_cmp, w_ck1, w_ck2, w_cv1, w_cv2, p_a, p_b, w_o, g_cross, g_mem, w_cq, w_ckv, w_co, g_ffn, w_pq, sub_k1, sub_k2, peer_u, peer_v, g_final):
    assert g_attn.shape[0] == 1, "the final norm is fused into the last PEER step of a single layer"
    w = _layer_weights(w_in[0], pe_cmp[0], w_ck1[0], w_ck2[0], w_cv1[0], w_cv2[0], p_a[0], p_b[0], w_o[0], w_cq[0],
                       w_ckv[0], w_co[0], w_pq[0], peer_u[0], peer_v[0])
    y_p, moba_p, nsa_p, win_p, mem_p = _prompt_group(x_prompt, mem_prompt, w, g_attn[0], g_cross[0], g_mem[0],
                                                    g_ffn[0], sub_k1[0], sub_k2[0], g_final)
    y_s, moba_s, nsa_s, win_s = _sample_group(x_sample, cache_moba_kv[0], cache_nsa_kv[0], state_nsa_win[0],
                                              cache_mem_kv[0], page_table, w, g_attn[0], g_cross[0], g_ffn[0],
                                              sub_k1[0], sub_k2[0], g_final)
    return (y_p, y_s, moba_p[None], moba_s[None], nsa_p[None], nsa_s[None], win_p[None], win_s[None], mem_p[None])
```

```python
import functools

import numpy as np
import jax
import jax.numpy as jnp
from jax import lax
from jax.experimental import pallas as pl
from jax.experimental.pallas import tpu as pltpu

F32 = jnp.float32
BF16 = jnp.bfloat16
HIGHEST = lax.Precision.HIGHEST

LANES = 128
SUBLANES = 8
VMEM_LIMIT_BYTES = 56 * 1024 * 1024

D_MODEL = 1024
HEAD_DIM = 64
ROPE_THETA = 10000.0
NORM_EPS = 1e-6
NEG_BIG = -1e30
PAGE_SIZE = 128

A_HEADS = 8
A_BLOCK = 256
A_TOPK = 3
A_WIDTH = A_HEADS * HEAD_DIM

B_HEADS = 8
B_KV_HEADS = 2
B_GROUP = B_HEADS // B_KV_HEADS
B_WIDTH = B_HEADS * HEAD_DIM
B_KV_WIDTH = B_KV_HEADS * HEAD_DIM
CMP_LEN = 32
CMP_STRIDE = 16
CMP_HIDDEN = 64
SEL_BLOCK = 64
SEL_TOPN = 16
WINDOW = 512
FORCE_SCORE = 1e4

C_HEADS = 4
C_HEAD_DIM = 128
C_WIDTH = C_HEADS * C_HEAD_DIM

P_HEADS = 8
P_NKEYS = 128
P_QDIM = 256
P_TOPK = 16
P_PICKS = P_HEADS * P_TOPK

Q_BLOCK = 256
N_MAIN = 3 * A_WIDTH + B_WIDTH + 6 * B_KV_WIDTH
N_GATES = 3 * B_HEADS
N_PROJ = N_MAIN + LANES + 2 * D_MODEL


def _params(semantics):
    return pltpu.CompilerParams(dimension_semantics=semantics, vmem_limit_bytes=VMEM_LIMIT_BYTES)


def _lane_iota(shape, dtype=jnp.int32):
    return lax.broadcasted_iota(dtype, shape, len(shape) - 1)


def _row_iota(shape, dtype=jnp.int32):
    return lax.broadcasted_iota(dtype, shape, len(shape) - 2)


def _rms(x, g):
    return x * lax.rsqrt(jnp.mean(x * x, axis=-1, keepdims=True) + NORM_EPS) * g


def _sigmoid(x):
    return 1.0 / (1.0 + jnp.exp(-x))


def _dot_t(a, b, precision=None):
    return lax.dot_general(a, b, (((1,), (1,)), ((), ())), precision=precision, preferred_element_type=F32)


def _topk_mask(x, k, lane_f):
    sel = jnp.zeros(x.shape, F32)
    for _ in range(k):
        mx = jnp.max(x, axis=-1, keepdims=True)
        first = jnp.min(jnp.where(x == mx, lane_f, float(x.shape[-1])), axis=-1, keepdims=True)
        hit = lane_f == first
        sel = jnp.where(hit, 1.0, sel)
        x = jnp.where(hit, -jnp.inf, x)
    return sel


def _softmax_rows(s, mask):
    s = jnp.where(mask, s, NEG_BIG)
    m = jnp.max(s, axis=-1, keepdims=True)
    p = jnp.where(mask, jnp.exp(s - m), 0.0)
    return p, jnp.sum(p, axis=-1, keepdims=True)


def _safe_inv(l):
    return jnp.where(l > 0.0, 1.0 / jnp.where(l > 0.0, l, 1.0), 0.0)


def _proj_kernel(x_ref, g_ref, w_ref, cs_ref, sn_ref, aq_ref, akv_ref, bq_ref, bqr_ref, bkv_ref, bwin_ref,
                 gates_ref, ga_ref, gb_ref):
    ub = _rms(x_ref[...], g_ref[...]).astype(BF16)
    cs = cs_ref[...]
    sn = sn_ref[...]
    first_half = (_lane_iota((1, LANES)) % HEAD_DIM) < (HEAD_DIM // 2)

    def cols(c0, n):
        return jnp.dot(ub, w_ref[:, c0:c0 + n], preferred_element_type=F32)

    def rot(p):
        swapped = jnp.where(first_half, pltpu.roll(p, LANES - HEAD_DIM // 2, 1), pltpu.roll(p, HEAD_DIM // 2, 1))
        return p * cs + swapped * sn

    def rot_wide(p):
        return jnp.concatenate([rot(p[:, c:c + LANES]) for c in range(0, p.shape[1], LANES)], axis=1)

    aq_ref[...] = rot_wide(cols(0, A_WIDTH))
    akv_ref[:, 0:A_WIDTH] = rot_wide(cols(A_WIDTH, A_WIDTH))
    akv_ref[:, A_WIDTH:2 * A_WIDTH] = cols(2 * A_WIDTH, A_WIDTH)
    bq = cols(3 * A_WIDTH, B_WIDTH)
    bq_ref[...] = bq
    bqr_ref[...] = rot_wide(bq)
    c0 = 3 * A_WIDTH + B_WIDTH
    bkv = cols(c0, 4 * B_KV_WIDTH)
    bkv_ref[:, 0:2 * LANES] = bkv[:, 0:2 * LANES]
    bkv_ref[:, 2 * LANES:3 * LANES] = rot(bkv[:, 2 * LANES:3 * LANES])
    bkv_ref[:, 3 * LANES:4 * LANES] = bkv[:, 3 * LANES:4 * LANES]
    bwin = cols(c0 + 4 * B_KV_WIDTH, 2 * B_KV_WIDTH)
    bwin_ref[:, 0:LANES] = rot(bwin[:, 0:LANES])
    bwin_ref[:, LANES:2 * LANES] = bwin[:, LANES:2 * LANES]
    gates_ref[...] = cols(N_MAIN, LANES)
    ga_ref[...] = cols(N_MAIN + LANES, D_MODEL)
    gb_ref[...] = cols(N_MAIN + LANES + D_MODEL, D_MODEL)


def _rope_tables(pos):
    half = HEAD_DIM // 2
    inv_freq = ROPE_THETA ** (-jnp.arange(half, dtype=F32) / half)
    ang = pos.astype(F32)[:, None] * inv_freq[None, :]
    cos, sin = jnp.cos(ang), jnp.sin(ang)
    reps = LANES // HEAD_DIM
    return jnp.tile(jnp.concatenate([cos, cos], axis=1), (1, reps)), jnp.tile(jnp.concatenate([-sin, sin], axis=1), (1, reps))


def _projection(x2d, pos, seq, g_attn, w_proj):
    n = x2d.shape[0]
    tm = min(Q_BLOCK, n)
    cs, sn = _rope_tables(pos)
    if seq >= tm:
        tab_map = lambda i: (i % (seq // tm), 0)
    else:
        cs, sn = jnp.tile(cs, (tm // seq, 1)), jnp.tile(sn, (tm // seq, 1))
        tab_map = lambda i: (0, 0)
    widths = (A_WIDTH, 2 * A_WIDTH, B_WIDTH, B_WIDTH, 4 * B_KV_WIDTH, 2 * B_KV_WIDTH, LANES, D_MODEL, D_MODEL)
    row = lambda i: (i, 0)
    fixed = lambda i: (0, 0)
    return pl.pallas_call(
        _proj_kernel,
        grid=(n // tm,),
        in_specs=[pl.BlockSpec((tm, D_MODEL), row), pl.BlockSpec((1, D_MODEL), fixed),
                  pl.BlockSpec((D_MODEL, N_PROJ), fixed), pl.BlockSpec((tm, LANES), tab_map),
                  pl.BlockSpec((tm, LANES), tab_map)],
        out_specs=[pl.BlockSpec((tm, w), row) for w in widths],
        out_shape=[jax.ShapeDtypeStruct((n, w), F32) for w in widths],
        compiler_params=_params(("parallel",)),
    )(x2d, g_attn.reshape(1, D_MODEL), w_proj, cs, sn)


def _rms_matmul_kernel(x_ref, g_ref, w_ref, o_ref):
    o_ref[...] = jnp.dot(_rms(x_ref[...], g_ref[...]).astype(BF16), w_ref[...], preferred_element_type=F32)


def _rms_matmul(x2d, g, w_bf16):
    n, d = x2d.shape
    m = w_bf16.shape[1]
    tm = min(Q_BLOCK, n)
    return pl.pallas_call(
        _rms_matmul_kernel,
        grid=(n // tm,),
        in_specs=[pl.BlockSpec((tm, d), lambda i: (i, 0)), pl.BlockSpec((1, d), lambda i: (0, 0)),
                  pl.BlockSpec((d, m), lambda i: (0, 0))],
        out_specs=pl.BlockSpec((tm, m), lambda i: (i, 0)),
        out_shape=jax.ShapeDtypeStruct((n, m), F32),
        compiler_params=_params(("parallel",)),
    )(x2d, g.reshape(1, d), w_bf16)


def _moba_prompt_kernel(q_ref, k_ref, v_ref, o_ref, *, seq):
    nb = seq // A_BLOCK
    k = k_ref[0]
    kb = k.astype(BF16)
    vb = v_ref[0].astype(BF16)
    lane = _lane_iota((1, LANES))
    lane_f = lane.astype(F32)
    kmean = jnp.concatenate(
        [jnp.mean(k[j * A_BLOCK:(j + 1) * A_BLOCK], axis=0, keepdims=True) for j in range(nb)]
        + [jnp.zeros((LANES - nb, LANES), F32)], axis=0)
    expand = jnp.where(_row_iota((LANES, seq)) == _lane_iota((LANES, seq)) // A_BLOCK, 1.0, 0.0).astype(BF16)
    scale = HEAD_DIM ** -0.5
    for i in range(nb):
        qi = q_ref[0, i * Q_BLOCK:(i + 1) * Q_BLOCK, :]
        outs = []
        for hh in range(LANES // HEAD_DIM):
            head = (lane // HEAD_DIM) == hh
            qh = jnp.where(head, qi, 0.0)
            gate = _dot_t(qh, kmean, HIGHEST)
            gate = jnp.where(lane < i, gate, -jnp.inf)
            sel = _topk_mask(gate, min(A_TOPK, nb), lane_f)
            sel = jnp.where(lane < i, sel, 0.0)
            s = _dot_t((qh * scale).astype(BF16), kb[0:(i + 1) * A_BLOCK])
            chosen = jnp.dot(sel.astype(BF16), expand[:, 0:(i + 1) * A_BLOCK], preferred_element_type=F32) > 0.5
            key = _lane_iota((Q_BLOCK, (i + 1) * A_BLOCK)) - i * A_BLOCK
            own = (key >= 0) & (key <= _row_iota((Q_BLOCK, (i + 1) * A_BLOCK)))
            p, l = _softmax_rows(s, chosen | own)
            o = jnp.dot(p.astype(BF16), vb[0:(i + 1) * A_BLOCK], preferred_element_type=F32)
            outs.append(o * _safe_inv(l))
        o_ref[0, i * Q_BLOCK:(i + 1) * Q_BLOCK, :] = jnp.where((lane // HEAD_DIM) == 0, outs[0], outs[1])


def _moba_prompt(aq, akv):
    b, seq, _ = aq.shape
    hp = A_WIDTH // LANES
    return pl.pallas_call(
        functools.partial(_moba_prompt_kernel, seq=seq),
        grid=(b, hp),
        in_specs=[pl.BlockSpec((1, seq, LANES), lambda i, j: (i, 0, j)),
                  pl.BlockSpec((1, seq, LANES), lambda i, j: (i, 0, j)),
                  pl.BlockSpec((1, seq, LANES), lambda i, j: (i, 0, hp + j))],
        out_specs=pl.BlockSpec((1, seq, LANES), lambda i, j: (i, 0, j)),
        out_shape=jax.ShapeDtypeStruct((b, seq, A_WIDTH), F32),
        compiler_params=_params(("parallel", "parallel")),
    )(aq, akv, akv)


def _compress_rows(xk_ref, xv_ref, n_rows, pe_ref, wa_ref, wb_ref, w2_ref):
    acc_a = jnp.zeros((n_rows, 2 * LANES), F32)
    acc_b = jnp.zeros((n_rows, 2 * LANES), F32)
    for tt in range(CMP_STRIDE):
        rows_tt = pl.ds(tt, n_rows, stride=CMP_STRIDE)
        xt = jnp.concatenate([xk_ref[rows_tt, :], xv_ref[rows_tt, :]], axis=1)
        acc_a += jnp.dot((xt + pe_ref[tt:tt + 1, :]).astype(BF16), wa_ref[tt], preferred_element_type=F32)
        acc_b += jnp.dot((xt + pe_ref[CMP_STRIDE + tt:CMP_STRIDE + tt + 1, :]).astype(BF16), wb_ref[tt],
                         preferred_element_type=F32)
    hidden = acc_a + pltpu.roll(acc_b, n_rows - 1, 0)
    out = jnp.dot(jax.nn.gelu(hidden).astype(BF16), w2_ref[...], preferred_element_type=F32)
    return jnp.where(_row_iota(out.shape) < n_rows - 1, out, 0.0)


def _compress_prompt_kernel(xk_ref, xv_ref, pe_ref, wa_ref, wb_ref, w2_ref, o_ref, *, seq):
    o_ref[0] = _compress_rows(xk_ref.at[0], xv_ref.at[0], seq // CMP_STRIDE, pe_ref, wa_ref, wb_ref, w2_ref)


def _compress_weights(pe_cmp, w_ck1, w_ck2, w_cv1, w_cv2):
    def diag(mk, mv):
        z = jnp.zeros_like(mk)
        rows = [[mk, z, z, z], [z, mk, z, z], [z, z, mv, z], [z, z, z, mv]]
        return jnp.concatenate([jnp.concatenate(r, axis=-1) for r in rows], axis=-2)

    k1 = w_ck1.reshape(CMP_LEN, HEAD_DIM, CMP_HIDDEN)
    v1 = w_cv1.reshape(CMP_LEN, HEAD_DIM, CMP_HIDDEN)
    w1 = diag(k1, v1).astype(BF16)
    return jnp.tile(pe_cmp, (1, 4)), w1[:CMP_STRIDE], w1[CMP_STRIDE:], diag(w_ck2, w_cv2).astype(BF16)


def _compress_prompt(bkv, cw):
    b, seq, _ = bkv.shape
    n_rows = seq // CMP_STRIDE
    pe, wa, wb, w2 = cw
    full = lambda *s: pl.BlockSpec(s, lambda i: (0,) * len(s))
    return pl.pallas_call(
        functools.partial(_compress_prompt_kernel, seq=seq),
        grid=(b,),
        in_specs=[pl.BlockSpec((1, seq, LANES), lambda i: (i, 0, 0)), pl.BlockSpec((1, seq, LANES), lambda i: (i, 0, 1)),
                  full(CMP_LEN, 2 * LANES), full(CMP_STRIDE, 2 * LANES, 2 * LANES),
                  full(CMP_STRIDE, 2 * LANES, 2 * LANES), full(2 * LANES, 2 * LANES)],
        out_specs=pl.BlockSpec((1, n_rows, 2 * LANES), lambda i: (i, 0, 0)),
        out_shape=jax.ShapeDtypeStruct((b, n_rows, 2 * LANES), F32),
        compiler_params=_params(("parallel",)),
    )(bkv, bkv, pe, wa, wb, w2)


def _overlap_matrix(n_cmp, n_sel, rows, cols):
    c_start = np.arange(n_cmp)[:, None] * CMP_STRIDE
    s_start = np.arange(n_sel)[None, :] * SEL_BLOCK
    ov = np.clip(np.minimum(c_start + CMP_LEN, s_start + SEL_BLOCK) - np.maximum(c_start, s_start), 0, None)
    out = np.zeros((rows, cols), np.float32)
    out[:n_cmp, :n_sel] = ov / CMP_STRIDE
    return jnp.asarray(out)


def _group_halves(x, g):
    xg = jnp.where((_lane_iota((1, LANES)) // HEAD_DIM) == g, x, 0.0)
    return xg + pltpu.roll(xg, HEAD_DIM, 1)


def _gate_column(gates, head, branch):
    lane = _lane_iota((1, LANES))
    return _sigmoid(jnp.sum(jnp.where(lane == head * 3 + branch, gates, 0.0), axis=-1, keepdims=True))


def _nsa_select_kernel(q_ref, gates_ref, cmp_ref, ov_ref, ocmp_ref, sel_ref, *, seq):
    g = pl.program_id(1)
    i = pl.program_id(2)
    n_cmp = (seq - CMP_LEN) // CMP_STRIDE + 1
    rows = B_GROUP * Q_BLOCK
    scale = HEAD_DIM ** -0.5
    lane = _lane_iota((1, LANES))
    lane_f = lane.astype(F32)
    both_halves = lambda x: _group_halves(x, g)

    parts = []
    for j in range(B_GROUP):
        x = q_ref[0, :, (j // 2) * LANES:(j // 2 + 1) * LANES]
        parts.append(jnp.where((lane // HEAD_DIM) == (j % 2), x, 0.0) * scale)
    q4 = jnp.concatenate(parts, axis=0).astype(BF16)
    qpos = i * Q_BLOCK + _row_iota((Q_BLOCK, 1))

    kc = both_halves(cmp_ref[0, :, 0:LANES]).astype(BF16)
    vc = both_halves(cmp_ref[0, :, LANES:2 * LANES]).astype(BF16)
    n_tok = kc.shape[0]
    tok = _lane_iota((1, n_tok))
    cmask = (tok < n_cmp) & (tok * CMP_STRIDE + (CMP_LEN - 1) <= qpos)
    s = _dot_t(q4, kc).reshape(B_GROUP, Q_BLOCK, n_tok)
    p, l = _softmax_rows(s, cmask[None])
    p_cmp = p * _safe_inv(l)
    o_cmp = jnp.dot(p_cmp.reshape(rows, n_tok).astype(BF16), vc, preferred_element_type=F32)
    imp = jnp.dot(jnp.sum(p_cmp, axis=0), ov_ref[...], precision=HIGHEST, preferred_element_type=F32)
    cur = qpos // SEL_BLOCK
    forced = (lane == 0) | (lane == cur) | (lane == cur - 1)
    score = jnp.where(lane <= cur, jnp.where(forced, FORCE_SCORE, imp), -jnp.inf)
    n_sel = -(-seq // SEL_BLOCK)
    sel = _topk_mask(score, min(SEL_TOPN, n_sel), lane_f)
    sel_ref[0, 0] = jnp.where(lane <= cur, sel, 0.0)

    gates = gates_ref[0]
    outs = [_gate_column(gates, g * B_GROUP + j, 0) * o_cmp[j * Q_BLOCK:(j + 1) * Q_BLOCK] for j in range(B_GROUP)]
    left = (lane // HEAD_DIM) == 0
    ocmp_ref[0] = jnp.concatenate([jnp.where(left, outs[0], outs[1]), jnp.where(left, outs[2], outs[3])], axis=1)


def _nsa_attend_kernel(qr_ref, ksel_ref, vsel_ref, kwin_ref, vwin_ref, sel_ref, gates_ref, ocmp_ref, o_ref, *, seq):
    chunk = pl.program_id(1)
    g = chunk // (B_GROUP // 2)
    nq = seq // Q_BLOCK
    scale = HEAD_DIM ** -0.5
    lane = _lane_iota((1, LANES))
    ks = _group_halves(ksel_ref[0], g).astype(BF16)
    vs = _group_halves(vsel_ref[0], g).astype(BF16)
    kw = _group_halves(kwin_ref[0], g).astype(BF16)
    vw = _group_halves(vwin_ref[0], g).astype(BF16)
    expand = jnp.where(_row_iota((LANES, seq)) == _lane_iota((LANES, seq)) // SEL_BLOCK, 1.0, 0.0).astype(BF16)
    for i in range(nq):
        rows = slice(i * Q_BLOCK, (i + 1) * Q_BLOCK)
        n_keys = (i + 1) * Q_BLOCK
        w0 = max(i * Q_BLOCK - WINDOW, 0)
        qpos = i * Q_BLOCK + _row_iota((Q_BLOCK, 1))
        chosen = jnp.dot(sel_ref[0, 0, rows, :].astype(BF16), expand[:, 0:n_keys], preferred_element_type=F32) > 0.5
        smask = chosen & (_lane_iota((1, n_keys)) <= qpos)
        wpos = w0 + _lane_iota((1, n_keys - w0))
        wmask = (wpos <= qpos) & (wpos > qpos - WINDOW)
        qi = qr_ref[0, rows, :]
        gates = gates_ref[0, rows, :]
        outs = []
        for hh in range(LANES // HEAD_DIM):
            qh = (jnp.where((lane // HEAD_DIM) == hh, qi, 0.0) * scale).astype(BF16)
            p, l = _softmax_rows(_dot_t(qh, ks[0:n_keys]), smask)
            o_sel = jnp.dot(p.astype(BF16), vs[0:n_keys], preferred_element_type=F32) * _safe_inv(l)
            p, l = _softmax_rows(_dot_t(qh, kw[w0:n_keys]), wmask)
            o_win = jnp.dot(p.astype(BF16), vw[w0:n_keys], preferred_element_type=F32) * _safe_inv(l)
            head = chunk * (LANES // HEAD_DIM) + hh
            outs.append(_gate_column(gates, head, 1) * o_sel + _gate_column(gates, head, 2) * o_win)
        o_ref[0, rows, :] = ocmp_ref[0, rows, :] + jnp.where((lane // HEAD_DIM) == 0, outs[0], outs[1])


def _nsa_prompt(bq, bqr, gates, bkv, bwin, cmp_tok):
    b, seq, _ = bq.shape
    nq = seq // Q_BLOCK
    n_tok = cmp_tok.shape[1]
    n_cmp = (seq - CMP_LEN) // CMP_STRIDE + 1
    n_sel = -(-seq // SEL_BLOCK)
    assert n_sel <= LANES
    overlap = _overlap_matrix(n_cmp, n_sel, n_tok, LANES)
    ocmp, sel = pl.pallas_call(
        functools.partial(_nsa_select_kernel, seq=seq),
        grid=(b, B_KV_HEADS, nq),
        in_specs=[pl.BlockSpec((1, Q_BLOCK, 2 * LANES), lambda i, g, t: (i, t, g)),
                  pl.BlockSpec((1, Q_BLOCK, LANES), lambda i, g, t: (i, t, 0)),
                  pl.BlockSpec((1, n_tok, 2 * LANES), lambda i, g, t: (i, 0, 0)),
                  pl.BlockSpec((n_tok, LANES), lambda i, g, t: (0, 0))],
        out_specs=[pl.BlockSpec((1, Q_BLOCK, 2 * LANES), lambda i, g, t: (i, t, g)),
                   pl.BlockSpec((1, 1, Q_BLOCK, LANES), lambda i, g, t: (i, g, t, 0))],
        out_shape=[jax.ShapeDtypeStruct((b, seq, B_WIDTH), F32),
                   jax.ShapeDtypeStruct((b, B_KV_HEADS, seq, LANES), F32)],
        compiler_params=_params(("parallel", "parallel", "parallel")),
    )(bq, gates, cmp_tok, overlap)
    chunks = B_WIDTH // LANES
    per_group = chunks // B_KV_HEADS
    own = pl.BlockSpec((1, seq, LANES), lambda i, c: (i, 0, c))
    col = lambda k: pl.BlockSpec((1, seq, LANES), lambda i, c: (i, 0, k))
    return pl.pallas_call(
        functools.partial(_nsa_attend_kernel, seq=seq),
        grid=(b, chunks),
        in_specs=[own, col(2), col(3), col(0), col(1),
                  pl.BlockSpec((1, 1, seq, LANES), lambda i, c: (i, c // per_group, 0, 0)),
                  pl.BlockSpec((1, seq, LANES), lambda i, c: (i, 0, 0)), own],
        out_specs=own,
        out_shape=jax.ShapeDtypeStruct((b, seq, B_WIDTH), F32),
        compiler_params=_params(("parallel", "parallel")),
    )(bqr, bkv, bkv, bwin, bwin, sel, gates, ocmp)


def _merge_kernel(x_ref, oa_ref, ob_ref, ga_ref, gb_ref, pa_ref, pb_ref, wo_ref, h_ref):
    ya = jnp.dot(oa_ref[...].astype(BF16), pa_ref[...], preferred_element_type=F32)
    yb = jnp.dot(ob_ref[...].astype(BF16), pb_ref[...], preferred_element_type=F32)
    mixed = _sigmoid(ga_ref[...]) * ya + _sigmoid(gb_ref[...]) * yb
    h_ref[...] = x_ref[...] + jnp.dot(mixed.astype(BF16), wo_ref[...], preferred_element_type=F32)


def _merge(x2d, oa, ob, ga, gb, pa, pb, wo):
    n = x2d.shape[0]
    tm = min(Q_BLOCK, n)
    row = lambda w: pl.BlockSpec((tm, w), lambda i: (i, 0))
    full = lambda a: pl.BlockSpec(a.shape, lambda i: (0, 0))
    return pl.pallas_call(
        _merge_kernel,
        grid=(n // tm,),
        in_specs=[row(D_MODEL), row(A_WIDTH), row(B_WIDTH), row(D_MODEL), row(D_MODEL), full(pa), full(pb), full(wo)],
        out_specs=row(D_MODEL),
        out_shape=jax.ShapeDtypeStruct((n, D_MODEL), F32),
        compiler_params=_params(("parallel",)),
    )(x2d, oa, ob, ga, gb, pa, pb, wo)


def _cross_kernel(h_ref, mem_ref, g_ref, wq_ref, wo_ref, o_ref):
    h = h_ref[0]
    q = jnp.dot(_rms(h, g_ref[...]).astype(BF16), wq_ref[...], preferred_element_type=F32)
    scale = C_HEAD_DIM ** -0.5
    outs = []
    for hd in range(C_HEADS):
        c = slice(hd * C_HEAD_DIM, (hd + 1) * C_HEAD_DIM)
        kh = mem_ref[0, :, c].astype(BF16)
        vh = mem_ref[0, :, C_WIDTH + hd * C_HEAD_DIM:C_WIDTH + (hd + 1) * C_HEAD_DIM].astype(BF16)
        s = _dot_t(q[:, c].astype(BF16), kh) * scale
        m = jnp.max(s, axis=-1, keepdims=True)
        p = jnp.exp(s - m)
        o = jnp.dot(p.astype(BF16), vh, preferred_element_type=F32)
        outs.append(o / jnp.sum(p, axis=-1, keepdims=True))
    att = jnp.concatenate(outs, axis=1).astype(BF16)
    o_ref[0] = h + jnp.dot(att, wo_ref[...], preferred_element_type=F32)


def _cross(h3, mem_kv, g_cross, wq, wo):
    b, t, _ = h3.shape
    tq = min(Q_BLOCK, t)
    mlen = mem_kv.shape[1]
    return pl.pallas_call(
        _cross_kernel,
        grid=(b, t // tq),
        in_specs=[pl.BlockSpec((1, tq, D_MODEL), lambda i, j: (i, j, 0)),
                  pl.BlockSpec((1, mlen, 2 * C_WIDTH), lambda i, j: (i, 0, 0)),
                  pl.BlockSpec((1, D_MODEL), lambda i, j: (0, 0)),
                  pl.BlockSpec(wq.shape, lambda i, j: (0, 0)), pl.BlockSpec(wo.shape, lambda i, j: (0, 0))],
        out_specs=pl.BlockSpec((1, tq, D_MODEL), lambda i, j: (i, j, 0)),
        out_shape=jax.ShapeDtypeStruct((b, t, D_MODEL), F32),
        compiler_params=_params(("parallel", "parallel")),
    )(h3, mem_kv, g_cross.reshape(1, D_MODEL), wq, wo)


def _peer_candidate_ids():
    ids = -np.ones((7 * SUBLANES,), np.float32)
    layout = [(0, 0, 16), (1, 16, 8), (2, 24, 5), (3, 32, 4), (4, 36, 3), (5, 40, 2), (6, 42, 2), (7, 44, 2)]
    layout += [(a, 40 + a, 1) for a in range(8, 16)]
    for a, row, n in layout:
        assert (a + 1) * n <= P_TOPK < (a + 1) * (n + 1)
        ids[row:row + n] = a * P_TOPK + np.arange(n)
    return jnp.asarray(np.tile(ids[:, None], (1, LANES)))


def _peer_select_kernel(h_ref, g_ref, wq_ref, k1_ref, k2_ref, cid_ref, z_ref, eid_ref, gate_ref, s_ref, e_scr, w_scr,
                        *, tm):
    z = _rms(h_ref[...], g_ref[...])
    z_ref[...] = z
    zb = z.astype(BF16)
    half = P_QDIM // 2
    k1 = k1_ref[...]
    k2 = k2_ref[...]
    for hd in range(P_HEADS):
        q = jnp.dot(zb, wq_ref[:, hd * P_QDIM:(hd + 1) * P_QDIM], preferred_element_type=F32)
        s1 = _dot_t(k1, q[:, 0:half], HIGHEST)
        s2 = _dot_t(k2, q[:, half:P_QDIM], HIGHEST)
        for tb in range(tm // LANES):
            s_ref[hd, tb, 0] = s1[:, tb * LANES:(tb + 1) * LANES]
            s_ref[hd, tb, 1] = s2[:, tb * LANES:(tb + 1) * LANES]

    key_f = _row_iota((P_NKEYS, LANES)).astype(F32)
    cand_valid = cid_ref[...] >= 0.0
    cand_f = jnp.where(cand_valid, cid_ref[...], float(P_TOPK * P_TOPK))
    r8 = _row_iota((SUBLANES, LANES))

    def pair_rows(x1, x2s, combine):
        x1s = jnp.concatenate(x1, axis=0)
        x2_8 = x2s[0:SUBLANES]
        shift = lambda k: pltpu.roll(x2_8, k, 0)
        return jnp.concatenate([
            combine(x1[0], x2s), combine(x1[1], x2_8), combine(x1[2], x2_8),
            jnp.where(r8 < 4, combine(x1[3], x2_8), combine(x1[4], shift(4))),
            jnp.where(r8 < 2, combine(x1[5], x2_8),
                      jnp.where(r8 < 4, combine(x1[6], shift(2)), combine(x1[7], shift(4)))),
            combine(x1s[SUBLANES:2 * SUBLANES], x2s[0:1])], axis=0)

    def take_top(x, ids_f, limit):
        mx = jnp.max(x, axis=0, keepdims=True)
        first = jnp.min(jnp.where(x == mx, ids_f, limit), axis=0, keepdims=True)
        hit = ids_f == first
        return mx, first, hit, jnp.where(hit, -jnp.inf, x)

    def select(x1, x2):
        v1, i1, v2, i2 = [], [], [], []
        for _ in range(P_TOPK):
            m, a, _, x1 = take_top(x1, key_f, float(P_NKEYS))
            v1.append(m)
            i1.append(a)
            m, a, _, x2 = take_top(x2, key_f, float(P_NKEYS))
            v2.append(m)
            i2.append(a)
        v2s = jnp.concatenate(v2, axis=0)
        i2s = jnp.concatenate(i2, axis=0)
        cand = jnp.where(cand_valid, pair_rows(v1, v2s, lambda a, b: a + b), -jnp.inf)
        cid = pair_rows(i1, i2s, lambda a, b: a * float(P_NKEYS) + b)
        es, ss = [], []
        for _ in range(P_TOPK):
            mx, _, hit, cand = take_top(cand, cand_f, float(P_TOPK * P_TOPK))
            es.append(jnp.max(jnp.where(hit, cid, -1.0), axis=0, keepdims=True))
            ss.append(mx)
        ex = jnp.exp(jnp.concatenate(ss, axis=0) - ss[0])
        return jnp.concatenate(es, axis=0), ex / jnp.sum(ex, axis=0, keepdims=True)

    def step(hd, carry):
        rows = pl.ds(pl.multiple_of(hd * P_TOPK, P_TOPK), P_TOPK)
        for tb in range(tm // LANES):
            e16, w16 = select(s_ref[hd, tb, 0], s_ref[hd, tb, 1])
            e_scr[tb, rows, :] = e16
            w_scr[tb, rows, :] = w16
        return carry

    lax.fori_loop(0, P_HEADS, step, 0)
    for tb in range(tm // LANES):
        eid_ref[tb * LANES:(tb + 1) * LANES, :] = jnp.transpose(e_scr[tb]).astype(jnp.int32)
        gate_ref[tb * LANES:(tb + 1) * LANES, :] = jnp.transpose(w_scr[tb])


def _peer_select(h2d, g_ffn, wq, k1, k2):
    n = h2d.shape[0]
    tm = min(Q_BLOCK, n)
    assert tm % LANES == 0 and n % tm == 0
    cand_ids = _peer_candidate_ids()
    row = lambda: pl.BlockSpec((tm, D_MODEL), lambda i: (i, 0))
    pick = lambda: pl.BlockSpec((tm, LANES), lambda i: (i, 0))
    full = lambda a: pl.BlockSpec(a.shape, lambda i: (0, 0))
    return pl.pallas_call(
        functools.partial(_peer_select_kernel, tm=tm),
        grid=(n // tm,),
        in_specs=[row(), pl.BlockSpec((1, D_MODEL), lambda i: (0, 0)), full(wq), full(k1), full(k2), full(cand_ids)],
        out_specs=[row(), pick(), pick()],
        out_shape=[jax.ShapeDtypeStruct((n, D_MODEL), F32), jax.ShapeDtypeStruct((n, LANES), jnp.int32),
                   jax.ShapeDtypeStruct((n, LANES), F32)],
        scratch_shapes=[pltpu.VMEM((P_HEADS, tm // LANES, 2, P_NKEYS, LANES), F32),
                        pltpu.VMEM((tm // LANES, P_PICKS, LANES), F32), pltpu.VMEM((tm // LANES, P_PICKS, LANES), F32)],
        compiler_params=_params(("parallel",)),
    )(h2d, g_ffn.reshape(1, D_MODEL), wq, k1, k2, cand_ids)


PEER_TOKENS = 8
ROW_TILES = D_MODEL // LANES


def _peer_gather_kernel(eid_ref, h_ref, z_ref, gate_ref, gfin_ref, uv_hbm, y_ref, buf, sem):
    j = pl.program_id(0)
    n_tiles = pl.num_programs(0) - 1
    rows = PEER_TOKENS * P_PICKS

    for to_slot in range(2):
        @pl.when((j < n_tiles) & (j % 2 == to_slot))
        def _(to_slot=to_slot):
            for r in range(rows):
                e = eid_ref[r // P_PICKS, r % P_PICKS]
                pltpu.make_async_copy(uv_hbm.at[e], buf.at[to_slot, pl.ds(r, 1)], sem.at[to_slot]).start()

    @pl.when(j > 0)
    def _():
        slot = (j - 1) % 2
        pltpu.make_async_copy(uv_hbm.at[pl.ds(0, rows), 0], buf.at[slot], sem.at[slot]).wait()
        gate_t = jnp.transpose(jnp.concatenate(
            [gate_ref[...], jnp.zeros((LANES - PEER_TOKENS, P_PICKS), F32)], axis=0))
        for p in range(PEER_TOKENS):
            picks = pl.ds(p * P_PICKS, P_PICKS)
            acc = jnp.zeros((P_PICKS, LANES), F32)
            for s in range(ROW_TILES):
                acc += buf[slot, picks, s * LANES:(s + 1) * LANES] * z_ref[p:p + 1, s * LANES:(s + 1) * LANES]
            act = jax.nn.gelu(jnp.sum(acc, axis=-1, keepdims=True))
            coef = jnp.broadcast_to(gate_t[:, p:p + 1] * act, (P_PICKS, LANES))
            outs = []
            for s in range(ROW_TILES):
                v_s = buf[slot, picks, D_MODEL + s * LANES:D_MODEL + (s + 1) * LANES]
                outs.append(jnp.sum(coef * v_s, axis=0, keepdims=True))
            y_ref[p:p + 1, :] = h_ref[p:p + 1, :] + jnp.concatenate(outs, axis=1)
        y_ref[...] = _rms(y_ref[...], gfin_ref[...])


def _peer_gather(eid, h2d, z, gate, g_final, uv):
    n = h2d.shape[0]
    tiles = n // PEER_TOKENS
    ahead = lambda j: (jnp.minimum(j, tiles - 1), 0)
    behind = lambda j: (jnp.maximum(j - 1, 0), 0)
    row = lambda w: pl.BlockSpec((PEER_TOKENS, w), behind)
    return pl.pallas_call(
        _peer_gather_kernel,
        grid=(tiles + 1,),
        in_specs=[pl.BlockSpec((PEER_TOKENS, P_PICKS), ahead, memory_space=pltpu.SMEM),
                  row(D_MODEL), row(D_MODEL), row(P_PICKS), pl.BlockSpec((1, D_MODEL), lambda j: (0, 0)),
                  pl.BlockSpec(memory_space=pl.ANY)],
        out_specs=row(D_MODEL),
        out_shape=jax.ShapeDtypeStruct((n, D_MODEL), F32),
        scratch_shapes=[pltpu.VMEM((2, PEER_TOKENS * P_PICKS, 2 * D_MODEL), F32),
                        pltpu.SemaphoreType.DMA((2,))],
        compiler_params=_params(("arbitrary",)),
    )(eid, h2d, z, gate, g_final.reshape(1, D_MODEL), uv)


PAGES_PER_STEP = 16
SAMPLE_ROWS = LANES


def _page_specs(rows, row_block):
    def spec(k):
        return pl.BlockSpec((1, rows, PAGE_SIZE), lambda b, c, pt: (pt[b, c * PAGES_PER_STEP + k], row_block, 0))
    return [spec(k) for k in range(PAGES_PER_STEP)]


def _online_step(s, mask, pv_fn, m_ref, l_ref, acc_ref):
    s = jnp.where(mask, s, NEG_BIG)
    m_old = m_ref[...]
    m_new = jnp.maximum(m_old, jnp.max(s, axis=-1, keepdims=True))
    p = jnp.where(mask, jnp.exp(s - m_new), 0.0)
    alpha = jnp.exp(m_old - m_new)
    l_ref[...] = alpha * l_ref[...] + jnp.sum(p, axis=-1, keepdims=True)
    acc_ref[...] = alpha * acc_ref[...] + pv_fn(p.astype(BF16))
    m_ref[...] = m_new


def _online_update(s, mask, v, m_ref, l_ref, acc_ref):
    _online_step(s, mask, lambda p: jnp.dot(p, v, preferred_element_type=F32), m_ref, l_ref, acc_ref)


def _online_update_t(s, mask, v_t, m_ref, l_ref, acc_ref):
    _online_step(s, mask, lambda p: jnp.transpose(_dot_t(v_t, p)), m_ref, l_ref, acc_ref)


def _moba_kmean_kernel(pt_ref, *refs):
    pages, o_ref = refs[:PAGES_PER_STEP], refs[PAGES_PER_STEP]
    c = pl.program_id(1)

    @pl.when(c == 0)
    def _():
        o_ref[...] = jnp.zeros(o_ref.shape, F32)

    n_keys = PAGES_PER_STEP * PAGE_SIZE
    n_lanes = o_ref.shape[2]
    k_t = jnp.concatenate([p[0] for p in pages], axis=1)
    blk = c * (n_keys // A_BLOCK) + _row_iota((n_keys, n_lanes)) // A_BLOCK
    avg = jnp.where(_lane_iota((n_keys, n_lanes)) == blk, 1.0 / A_BLOCK, 0.0).astype(BF16)
    k_hi = k_t.astype(BF16)
    k_lo = (k_t - k_hi.astype(F32)).astype(BF16)
    o_ref[0] += (jnp.dot(k_hi, avg, preferred_element_type=F32) + jnp.dot(k_lo, avg, preferred_element_type=F32))


def _moba_kmean(pool_t, page_table, n_lanes):
    b, n_pages = page_table.shape
    return pl.pallas_call(
        _moba_kmean_kernel,
        grid_spec=pltpu.PrefetchScalarGridSpec(
            num_scalar_prefetch=1, grid=(b, n_pages // PAGES_PER_STEP),
            in_specs=_page_specs(A_WIDTH, 0),
            out_specs=pl.BlockSpec((1, A_WIDTH, n_lanes), lambda i, c, pt: (i, 0, 0))),
        out_shape=jax.ShapeDtypeStruct((b, A_WIDTH, n_lanes), F32),
        compiler_params=_params(("parallel", "arbitrary")),
    )(page_table, *([pool_t] * PAGES_PER_STEP))


def _moba_sample_kernel(pt_ref, *refs, past, t_new):
    pages = refs[:PAGES_PER_STEP]
    kmean_ref, q_ref, new_ref, o_ref, sel_ref, m_ref, l_ref, acc_ref = refs[PAGES_PER_STEP:]
    c = pl.program_id(1)
    rows = A_HEADS * t_new
    n_lanes = sel_ref.shape[1]
    cur = past // A_BLOCK
    q = q_ref[0]
    qb = (q * HEAD_DIM ** -0.5).astype(BF16)

    @pl.when(c == 0)
    def _():
        lane = _lane_iota((1, n_lanes))
        gate = jnp.dot(q, kmean_ref[0], precision=HIGHEST, preferred_element_type=F32)
        gate = jnp.where(lane < cur, gate, -jnp.inf)
        sel = _topk_mask(gate, min(A_TOPK, cur + 1), lane.astype(F32))
        sel_ref[...] = jnp.where(lane < cur, sel, 0.0)
        m_ref[...] = jnp.full(m_ref.shape, NEG_BIG, F32)
        l_ref[...] = jnp.zeros(l_ref.shape, F32)
        acc_ref[...] = jnp.zeros(acc_ref.shape, F32)

    n_keys = PAGES_PER_STEP * PAGE_SIZE
    k_t = jnp.concatenate([p[0, 0:A_WIDTH, :] for p in pages], axis=1).astype(BF16)
    v_t = jnp.concatenate([p[0, A_WIDTH:2 * A_WIDTH, :] for p in pages], axis=1).astype(BF16)
    blk = c * (n_keys // A_BLOCK) + _lane_iota((n_lanes, n_keys)) // A_BLOCK
    expand = jnp.where(_row_iota((n_lanes, n_keys)) == blk, 1.0, 0.0).astype(BF16)
    chosen = jnp.dot(sel_ref[...].astype(BF16), expand, preferred_element_type=F32) > 0.5
    _online_update_t(jnp.dot(qb, k_t, preferred_element_type=F32), chosen, v_t, m_ref, l_ref, acc_ref)

    @pl.when(c == pl.num_programs(1) - 1)
    def _():
        kn = new_ref[0, :, 0:A_WIDTH].astype(BF16)
        vn = new_ref[0, :, A_WIDTH:2 * A_WIDTH].astype(BF16)
        t_key = _lane_iota((SAMPLE_ROWS, kn.shape[0]))
        t_row = _row_iota((SAMPLE_ROWS, kn.shape[0])) % t_new
        _online_update(_dot_t(qb, kn), (t_key <= t_row) & (t_key < t_new), vn, m_ref, l_ref, acc_ref)
        own = (_lane_iota((SAMPLE_ROWS, A_WIDTH)) // HEAD_DIM) == (_row_iota((SAMPLE_ROWS, A_WIDTH)) // t_new)
        o_ref[0] = jnp.where(own, acc_ref[...] * _safe_inv(l_ref[...]), 0.0)[0:rows]


def _moba_sample(pool_t, page_table, kmean_t, q_rows, new_kv, past, t_new):
    b, n_pages = page_table.shape
    rows = A_HEADS * t_new
    n_lanes = kmean_t.shape[2]
    per_b = lambda s: pl.BlockSpec((1,) + s, lambda i, c, pt: (i, 0, 0))
    return pl.pallas_call(
        functools.partial(_moba_sample_kernel, past=past, t_new=t_new),
        grid_spec=pltpu.PrefetchScalarGridSpec(
            num_scalar_prefetch=1, grid=(b, n_pages // PAGES_PER_STEP),
            in_specs=_page_specs(2 * A_WIDTH, 0) + [per_b(kmean_t.shape[1:]), per_b(q_rows.shape[1:]),
                                                     per_b(new_kv.shape[1:])],
            out_specs=per_b((rows, A_WIDTH)),
            scratch_shapes=[pltpu.VMEM((SAMPLE_ROWS, n_lanes), F32), pltpu.VMEM((SAMPLE_ROWS, 1), F32),
                            pltpu.VMEM((SAMPLE_ROWS, 1), F32), pltpu.VMEM((SAMPLE_ROWS, A_WIDTH), F32)]),
        out_shape=jax.ShapeDtypeStruct((b, rows, A_WIDTH), F32),
        compiler_params=_params(("parallel", "arbitrary")),
    )(page_table, *([pool_t] * PAGES_PER_STEP), kmean_t, q_rows, new_kv)


def _compress_sample_kernel(pt_ref, *refs, past):
    pages = refs[:PAGES_PER_STEP]
    pe_ref, wa_ref, wb_ref, w2_ref, o_ref, xk_ref, xv_ref = refs[PAGES_PER_STEP:]
    c = pl.program_id(1)
    for k, page in enumerate(pages):
        start = pl.multiple_of((c * PAGES_PER_STEP + k) * PAGE_SIZE, PAGE_SIZE)
        xk_ref[pl.ds(start, PAGE_SIZE), :] = jnp.transpose(page[0, 0:LANES, :])
        xv_ref[pl.ds(start, PAGE_SIZE), :] = jnp.transpose(page[0, LANES:2 * LANES, :])

    @pl.when(c == pl.num_programs(1) - 1)
    def _():
        o_ref[0] = _compress_rows(xk_ref, xv_ref, past // CMP_STRIDE, pe_ref, wa_ref, wb_ref, w2_ref)


def _compress_sample(pool, page_table, cw, past):
    b, n_pages = page_table.shape
    n_rows = past // CMP_STRIDE
    pe, wa, wb, w2 = cw
    full = lambda a: pl.BlockSpec(a.shape, lambda i, c, pt: (0,) * a.ndim)
    return pl.pallas_call(
        functools.partial(_compress_sample_kernel, past=past),
        grid_spec=pltpu.PrefetchScalarGridSpec(
            num_scalar_prefetch=1, grid=(b, n_pages // PAGES_PER_STEP),
            in_specs=_page_specs(2 * LANES, 0) + [full(pe), full(wa), full(wb), full(w2)],
            out_specs=pl.BlockSpec((1, n_rows, 2 * LANES), lambda i, c, pt: (i, 0, 0)),
            scratch_shapes=[pltpu.VMEM((past, LANES), F32), pltpu.VMEM((past, LANES), F32)]),
        out_shape=jax.ShapeDtypeStruct((b, n_rows, 2 * LANES), F32),
        compiler_params=_params(("parallel", "arbitrary")),
    )(page_table, *([pool] * PAGES_PER_STEP), pe, wa, wb, w2)


def _nsa_sample_kernel(pt_ref, *refs, past, t_new):
    pages = refs[:PAGES_PER_STEP]
    (cmp_ref, q_ref, qr_ref, gates_ref, newsel_ref, win_ref, newwin_ref, ov_ref, o_ref,
     kv_ref, m_ref, l_ref, acc_ref) = refs[PAGES_PER_STEP:]
    c = pl.program_id(1)
    for k, page in enumerate(pages):
        kv_ref[c * PAGES_PER_STEP + k] = page[0]

    @pl.when(c == pl.num_programs(1) - 1)
    def _():
        rows = B_HEADS * t_new
        slab = B_KV_HEADS * t_new
        scale = HEAD_DIM ** -0.5
        qb = (q_ref[0] * scale).astype(BF16)
        qrb = (qr_ref[0] * scale).astype(BF16)
        pos = past + _row_iota((SAMPLE_ROWS, 1)) % t_new

        n_tok = cmp_ref.shape[1]
        n_cmp = n_tok - 1
        kc = cmp_ref[0, :, 0:LANES].astype(BF16)
        vc = cmp_ref[0, :, LANES:2 * LANES].astype(BF16)
        tok = _lane_iota((1, n_tok))
        p, l = _softmax_rows(_dot_t(qb, kc), (tok < n_cmp) & (tok * CMP_STRIDE + (CMP_LEN - 1) <= pos))
        p_cmp = p * _safe_inv(l)
        o_cmp = jnp.dot(p_cmp.astype(BF16), vc, preferred_element_type=F32)
        p_sum = sum(p_cmp[j * slab:(j + 1) * slab] for j in range(B_GROUP))
        imp = jnp.dot(p_sum, ov_ref[...], precision=HIGHEST, preferred_element_type=F32)
        n_lanes = ov_ref.shape[1]
        lane = _lane_iota((1, n_lanes))
        cur = past // SEL_BLOCK
        forced = (lane == 0) | (lane == cur) | (lane == cur - 1)
        score = jnp.where(lane <= cur, jnp.where(forced, FORCE_SCORE, imp), -jnp.inf)
        sel = _topk_mask(score, min(SEL_TOPN, cur + 1), lane.astype(F32))
        sel = jnp.where(lane <= cur, sel, 0.0)
        sel_b = sel.astype(BF16)

        m_ref[...] = jnp.full(m_ref.shape, NEG_BIG, F32)
        l_ref[...] = jnp.zeros(l_ref.shape, F32)
        acc_ref[...] = jnp.zeros(acc_ref.shape, F32)
        n_keys = PAGES_PER_STEP * PAGE_SIZE
        tile_rows = lambda a: jnp.concatenate([a] * (SAMPLE_ROWS // slab), axis=0)
        for ci in range(past // n_keys):
            pages_ci = range(ci * PAGES_PER_STEP, (ci + 1) * PAGES_PER_STEP)
            ks_t = jnp.concatenate([kv_ref[pg, 0:LANES, :] for pg in pages_ci], axis=1).astype(BF16)
            vs_t = jnp.concatenate([kv_ref[pg, LANES:2 * LANES, :] for pg in pages_ci], axis=1).astype(BF16)
            blk = (ci * n_keys + _lane_iota((n_lanes, n_keys))) // SEL_BLOCK
            expand = jnp.where(_row_iota((n_lanes, n_keys)) == blk, 1.0, 0.0).astype(BF16)
            chosen = jnp.dot(sel_b, expand, preferred_element_type=F32) > 0.5
            _online_update_t(jnp.dot(qrb, ks_t, preferred_element_type=F32), tile_rows(chosen), vs_t,
                             m_ref, l_ref, acc_ref)
        kn = newsel_ref[0, :, 0:LANES].astype(BF16)
        vn = newsel_ref[0, :, LANES:2 * LANES].astype(BF16)
        t_key = _lane_iota((SAMPLE_ROWS, kn.shape[0]))
        cur_chosen = tile_rows(jnp.sum(jnp.where(lane == cur, sel, 0.0), axis=-1, keepdims=True)) > 0.5
        _online_update(_dot_t(qrb, kn), cur_chosen & (past + t_key <= pos) & (t_key < t_new), vn,
                       m_ref, l_ref, acc_ref)
        o_sel = acc_ref[...] * _safe_inv(l_ref[...])

        n_win = win_ref.shape[1]
        kw = jnp.concatenate([win_ref[0, :, 0:LANES], newwin_ref[0, :, 0:LANES]], axis=0).astype(BF16)
        vw = jnp.concatenate([win_ref[0, :, LANES:2 * LANES], newwin_ref[0, :, LANES:2 * LANES]], axis=0).astype(BF16)
        wpos = past - n_win + _lane_iota((1, kw.shape[0]))
        p, l = _softmax_rows(_dot_t(qrb, kw), (wpos <= pos) & (wpos > pos - WINDOW))
        o_win = jnp.dot(p.astype(BF16), vw, preferred_element_type=F32) * _safe_inv(l)

        gt = _sigmoid(gates_ref[0])
        o = gt[:, 0:1] * o_cmp + gt[:, 1:2] * o_sel + gt[:, 2:3] * o_win
        shape = (SAMPLE_ROWS, LANES)
        own = (_lane_iota(shape) // HEAD_DIM) == ((_row_iota(shape) // t_new) % B_KV_HEADS)
        o_ref[0] = jnp.where(own, o, 0.0)[0:rows]


def _nsa_sample(pool_t, page_table, cmp_tok, q_rows, qr_rows, gate_rows, new_sel, win_state, new_win, past, t_new):
    b, n_pages = page_table.shape
    rows = B_HEADS * t_new
    n_tok = cmp_tok.shape[1]
    n_sel = past // SEL_BLOCK + 1
    n_lanes = -(-n_sel // LANES) * LANES
    overlap = _overlap_matrix(n_tok - 1, n_sel, n_tok, n_lanes)
    per_b = lambda a: pl.BlockSpec((1,) + a.shape[1:], lambda i, c, pt: (i, 0, 0))
    return pl.pallas_call(
        functools.partial(_nsa_sample_kernel, past=past, t_new=t_new),
        grid_spec=pltpu.PrefetchScalarGridSpec(
            num_scalar_prefetch=1, grid=(b, n_pages // PAGES_PER_STEP),
            in_specs=_page_specs(2 * LANES, 1) + [per_b(cmp_tok), per_b(q_rows), per_b(qr_rows), per_b(gate_rows),
                                                  per_b(new_sel), per_b(win_state), per_b(new_win),
                                                  pl.BlockSpec(overlap.shape, lambda i, c, pt: (0, 0))],
            out_specs=pl.BlockSpec((1, rows, LANES), lambda i, c, pt: (i, 0, 0)),
            scratch_shapes=[pltpu.VMEM((n_pages, 2 * LANES, PAGE_SIZE), F32), pltpu.VMEM((SAMPLE_ROWS, 1), F32),
                            pltpu.VMEM((SAMPLE_ROWS, 1), F32), pltpu.VMEM((SAMPLE_ROWS, LANES), F32)]),
        out_shape=jax.ShapeDtypeStruct((b, rows, LANES), F32),
        compiler_params=_params(("parallel", "arbitrary")),
    )(page_table, *([pool_t] * PAGES_PER_STEP), cmp_tok, q_rows, qr_rows, gate_rows, new_sel, win_state, new_win,
      overlap)


def _layer_weights(w_in, pe_cmp, w_ck1, w_ck2, w_cv1, w_cv2, p_a, p_b, w_o, w_cq, w_ckv, w_co, w_pq, peer_u, peer_v):
    w_proj = jnp.concatenate(
        [w_in[:, :N_MAIN], jnp.pad(w_in[:, N_MAIN:N_MAIN + N_GATES], ((0, 0), (0, LANES - N_GATES))),
         w_in[:, N_MAIN + N_GATES:]], axis=1).astype(BF16)
    uv = jnp.concatenate([peer_u, peer_v], axis=1)[:, None, :]
    return dict(w_proj=w_proj, cw=_compress_weights(pe_cmp, w_ck1, w_ck2, w_cv1, w_cv2),
                p_a=p_a.astype(BF16), p_b=p_b.astype(BF16), w_o=w_o.astype(BF16), w_cq=w_cq.astype(BF16),
                w_ckv=w_ckv.astype(BF16), w_co=w_co.astype(BF16), w_pq=w_pq.astype(BF16),
                uv=uv)


def _channel_and_norm(h2d, batch, mem_kv, w, g_cross, g_ffn, sub_k1, sub_k2, g_final):
    h3 = _cross(h2d.reshape(batch, -1, D_MODEL), mem_kv, g_cross, w["w_cq"], w["w_co"])
    h2 = h3.reshape(-1, D_MODEL)
    z, eid, gate = _peer_select(h2, g_ffn, w["w_pq"], sub_k1, sub_k2)
    return _peer_gather(eid, h2, z, gate, g_final, w["uv"])


def _prompt_group(x, mem, w, g_attn, g_cross, g_mem, g_ffn, sub_k1, sub_k2, g_final):
    b, seq, _ = x.shape
    x2d = x.reshape(b * seq, D_MODEL)
    aq, akv, bq, bqr, bkv, bwin, gates, ga, gb = _projection(x2d, jnp.arange(seq, dtype=jnp.int32), seq, g_attn,
                                                             w["w_proj"])
    r3 = lambda a: a.reshape(b, seq, a.shape[-1])
    oa = _moba_prompt(r3(aq), r3(akv))
    cmp_tok = _compress_prompt(r3(bkv), w["cw"])
    ob = _nsa_prompt(r3(bq), r3(bqr), r3(gates), r3(bkv), r3(bwin), cmp_tok)
    h = _merge(x2d, oa.reshape(-1, A_WIDTH), ob.reshape(-1, B_WIDTH), ga, gb, w["p_a"], w["p_b"], w["w_o"])
    mlen = mem.shape[1]
    mem_kv = _rms_matmul(mem.reshape(b * mlen, D_MODEL), g_mem, w["w_ckv"]).reshape(b, mlen, 2 * C_WIDTH)
    y = _channel_and_norm(h, b, mem_kv, w, g_cross, g_ffn, sub_k1, sub_k2, g_final)
    win = r3(bwin)[:, seq - min(WINDOW, seq):]
    return (y.reshape(b, seq, D_MODEL), r3(akv).reshape(b, seq, 2, A_HEADS, HEAD_DIM),
            r3(bkv).reshape(b, seq, 4, B_KV_HEADS, HEAD_DIM), win.reshape(b, -1, 2, B_KV_HEADS, HEAD_DIM),
            mem_kv.reshape(b, mlen, 2, C_HEADS, C_HEAD_DIM))


def _pad_rows(a, rows):
    return jnp.pad(a, ((0, 0), (0, rows - a.shape[1]), (0, 0)))


def _sample_group(x, moba_pool, nsa_pool, win_state, mem_kv, page_table, w, g_attn, g_cross, g_ffn, sub_k1, sub_k2,
                  g_final):
    b, t, _ = x.shape
    past = page_table.shape[1] * PAGE_SIZE
    assert t * B_KV_HEADS == SUBLANES and past % (PAGES_PER_STEP * PAGE_SIZE) == 0 and B_HEADS * t <= SAMPLE_ROWS
    x2d = x.reshape(b * t, D_MODEL)
    pos = past + jnp.arange(t, dtype=jnp.int32)
    aq, akv, bq, bqr, bkv, bwin, gates, ga, gb = _projection(x2d, pos, t, g_attn, w["w_proj"])
    r3 = lambda a: a.reshape(b, t, a.shape[-1])

    n_pool = moba_pool.shape[0]
    pool_a = moba_pool.transpose(0, 2, 3, 4, 1).reshape(n_pool, 2 * A_WIDTH, PAGE_SIZE)
    pool_b = nsa_pool.transpose(0, 2, 3, 4, 1).reshape(n_pool, 4 * B_KV_WIDTH, PAGE_SIZE)

    n_blocks = past // A_BLOCK
    kmean_t = _moba_kmean(pool_a, page_table, -(-n_blocks // LANES) * LANES)
    qa = aq.reshape(b, t, A_HEADS, HEAD_DIM).transpose(0, 2, 1, 3)
    qa_rows = (qa[:, :, :, None, :] * jnp.eye(A_HEADS, dtype=F32)[None, :, None, :, None]).reshape(b, A_HEADS * t, A_WIDTH)
    oa_rows = _moba_sample(pool_a, page_table, kmean_t, _pad_rows(qa_rows, SAMPLE_ROWS),
                           _pad_rows(r3(akv), SUBLANES), past, t)
    oa = oa_rows.reshape(b, A_HEADS, t, A_WIDTH).sum(axis=1)

    cmp_tok = _compress_sample(pool_b, page_table, w["cw"], past)
    eye_g = jnp.eye(B_KV_HEADS, dtype=F32)[None, None, :, None, :, None]

    def group_rows(a):
        a = a.reshape(b, t, B_KV_HEADS, B_GROUP, HEAD_DIM).transpose(0, 3, 2, 1, 4)
        return _pad_rows((a[:, :, :, :, None, :] * eye_g).reshape(b, B_HEADS * t, LANES), SAMPLE_ROWS)

    gate_rows = gates[:, :N_GATES].reshape(b, t, B_KV_HEADS, B_GROUP, 3).transpose(0, 3, 2, 1, 4)
    gate_rows = jnp.pad(gate_rows.reshape(b, B_HEADS * t, 3), ((0, 0), (0, SAMPLE_ROWS - B_HEADS * t), (0, LANES - 3)))
    win_rows = win_state.reshape(b, win_state.shape[1], 2 * B_KV_WIDTH)
    ob_rows = _nsa_sample(pool_b, page_table, cmp_tok, group_rows(bq), group_rows(bqr), gate_rows,
                          _pad_rows(r3(bkv)[:, :, 2 * B_KV_WIDTH:], SUBLANES), win_rows,
                          _pad_rows(r3(bwin), SUBLANES), past, t)
    ob = ob_rows.reshape(b, B_GROUP, B_KV_HEADS, t, B_KV_HEADS, HEAD_DIM).sum(axis=4)
    ob = ob.transpose(0, 3, 2, 1, 4).reshape(b * t, B_WIDTH)

    h = _merge(x2d, oa.reshape(-1, A_WIDTH), ob, ga, gb, w["p_a"], w["p_b"], w["w_o"])
    mem_rows = mem_kv.reshape(b, mem_kv.shape[1], 2 * C_WIDTH)
    y = _channel_and_norm(h, b, mem_rows, w, g_cross, g_ffn, sub_k1, sub_k2, g_final)
    win = jnp.concatenate([win_rows, r3(bwin)], axis=1)
    win = win[:, win.shape[1] - min(WINDOW, win.shape[1]):]
    return (y.reshape(b, t, D_MODEL), r3(akv).reshape(b, t, 2, A_HEADS, HEAD_DIM),
            r3(bkv).reshape(b, t, 4, B_KV_HEADS, HEAD_DIM), win.reshape(b, -1, 2, B_KV_HEADS, HEAD_DIM))


def kernel(x_prompt, x_sample, cache_moba_kv, cache_nsa_kv, state_nsa_win, cache_mem_kv, page_table, mem_prompt, g_attn, w_in, pe_cmp, w_ck1, w_ck2, w_cv1, w_cv2, p_a, p_b, w_o, g_cross, g_mem, w_cq, w_ckv, w_co, g_ffn, w_pq, sub_k1, sub_k2, peer_u, peer_v, g_final):
    assert g_attn.shape[0] == 1, "the final norm is fused into the last PEER step of a single layer"
    w = _layer_weights(w_in[0], pe_cmp[0], w_ck1[0], w_ck2[0], w_cv1[0], w_cv2[0], p_a[0], p_b[0], w_o[0], w_cq[0],
                       w_ckv[0], w_co[0], w_pq[0], peer_u[0], peer_v[0])
    y_p, moba_p, nsa_p, win_p, mem_p = _prompt_group(x_prompt, mem_prompt, w, g_attn[0], g_cross[0], g_mem[0],
                                                    g_ffn[0], sub_k1[0], sub_k2[0], g_final)
    y_s, moba_s, nsa_s, win_s = _sample_group(x_sample, cache_moba_kv[0], cache_nsa_kv[0], state_nsa_win[0],
                                              cache_mem_kv[0], page_table, w, g_attn[0], g_cross[0], g_ffn[0],
                                              sub_k1[0], sub_k2[0], g_final)
    return (y_p, y_s, moba_p[None], moba_s[None], nsa_p[None], nsa_s[None], win_p[None], win_s[None], mem_p[None])
```

```python
import functools

import numpy as np
import jax
import jax.numpy as jnp
from jax import lax
from jax.experimental import pallas as pl
from jax.experimental.pallas import tpu as pltpu

F32 = jnp.float32
BF16 = jnp.bfloat16
HIGHEST = lax.Precision.HIGHEST

LANES = 128
SUBLANES = 8
VMEM_LIMIT_BYTES = 56 * 1024 * 1024

D_MODEL = 1024
HEAD_DIM = 64
ROPE_THETA = 10000.0
NORM_EPS = 1e-6
NEG_BIG = -1e30
PAGE_SIZE = 128

A_HEADS = 8
A_BLOCK = 256
A_TOPK = 3
A_WIDTH = A_HEADS * HEAD_DIM

B_HEADS = 8
B_KV_HEADS = 2
B_GROUP = B_HEADS // B_KV_HEADS
B_WIDTH = B_HEADS * HEAD_DIM
B_KV_WIDTH = B_KV_HEADS * HEAD_DIM
CMP_LEN = 32
CMP_STRIDE = 16
CMP_HIDDEN = 64
SEL_BLOCK = 64
SEL_TOPN = 16
WINDOW = 512
FORCE_SCORE = 1e4

C_HEADS = 4
C_HEAD_DIM = 128
C_WIDTH = C_HEADS * C_HEAD_DIM

P_HEADS = 8
P_NKEYS = 128
P_QDIM = 256
P_TOPK = 16
P_PICKS = P_HEADS * P_TOPK

Q_BLOCK = 256
N_MAIN = 3 * A_WIDTH + B_WIDTH + 6 * B_KV_WIDTH
N_GATES = 3 * B_HEADS
N_PROJ = N_MAIN + LANES + 2 * D_MODEL


def _params(semantics):
    return pltpu.CompilerParams(dimension_semantics=semantics, vmem_limit_bytes=VMEM_LIMIT_BYTES)


def _lane_iota(shape, dtype=jnp.int32):
    return lax.broadcasted_iota(dtype, shape, len(shape) - 1)


def _row_iota(shape, dtype=jnp.int32):
    return lax.broadcasted_iota(dtype, shape, len(shape) - 2)


def _rms(x, g):
    return x * lax.rsqrt(jnp.mean(x * x, axis=-1, keepdims=True) + NORM_EPS) * g


def _sigmoid(x):
    return 1.0 / (1.0 + jnp.exp(-x))


def _dot_t(a, b, precision=None):
    return lax.dot_general(a, b, (((1,), (1,)), ((), ())), precision=precision, preferred_element_type=F32)


def _topk_mask(x, k, lane_f):
    sel = jnp.zeros(x.shape, F32)
    for _ in range(k):
        mx = jnp.max(x, axis=-1, keepdims=True)
        first = jnp.min(jnp.where(x == mx, lane_f, float(x.shape[-1])), axis=-1, keepdims=True)
        hit = lane_f == first
        sel = jnp.where(hit, 1.0, sel)
        x = jnp.where(hit, -jnp.inf, x)
    return sel


def _topk_mask_t(x, k, ids_f):
    sel = jnp.zeros(x.shape, F32)
    for _ in range(k):
        mx = jnp.max(x, axis=0, keepdims=True)
        first = jnp.min(jnp.where(x == mx, ids_f, float(2 ** 20)), axis=0, keepdims=True)
        hit = ids_f == first
        sel = jnp.where(hit, 1.0, sel)
        x = jnp.where(hit, -jnp.inf, x)
    return sel


def _rows_to_lanes(sel_t):
    pad = jnp.zeros((LANES - sel_t.shape[0], sel_t.shape[1]), F32)
    return jnp.transpose(jnp.concatenate([sel_t, pad], axis=0))


def _softmax_rows(s, mask):
    s = jnp.where(mask, s, NEG_BIG)
    m = jnp.max(s, axis=-1, keepdims=True)
    p = jnp.where(mask, jnp.exp(s - m), 0.0)
    return p, jnp.sum(p, axis=-1, keepdims=True)


def _safe_inv(l):
    return jnp.where(l > 0.0, 1.0 / jnp.where(l > 0.0, l, 1.0), 0.0)


def _proj_kernel(x_ref, g_ref, w_ref, cs_ref, sn_ref, aq_ref, akv_ref, bq_ref, bqr_ref, bkv_ref, bwin_ref,
                 gates_ref, ga_ref, gb_ref):
    ub = _rms(x_ref[...], g_ref[...]).astype(BF16)
    cs = cs_ref[...]
    sn = sn_ref[...]
    first_half = (_lane_iota((1, LANES)) % HEAD_DIM) < (HEAD_DIM // 2)

    def cols(c0, n):
        return jnp.dot(ub, w_ref[:, c0:c0 + n], preferred_element_type=F32)

    def rot(p):
        swapped = jnp.where(first_half, pltpu.roll(p, LANES - HEAD_DIM // 2, 1), pltpu.roll(p, HEAD_DIM // 2, 1))
        return p * cs + swapped * sn

    def rot_wide(p):
        return jnp.concatenate([rot(p[:, c:c + LANES]) for c in range(0, p.shape[1], LANES)], axis=1)

    aq_ref[...] = rot_wide(cols(0, A_WIDTH))
    akv_ref[:, 0:A_WIDTH] = rot_wide(cols(A_WIDTH, A_WIDTH))
    akv_ref[:, A_WIDTH:2 * A_WIDTH] = cols(2 * A_WIDTH, A_WIDTH)
    bq = cols(3 * A_WIDTH, B_WIDTH)
    bq_ref[...] = bq
    bqr_ref[...] = rot_wide(bq)
    c0 = 3 * A_WIDTH + B_WIDTH
    bkv = cols(c0, 4 * B_KV_WIDTH)
    bkv_ref[:, 0:2 * LANES] = bkv[:, 0:2 * LANES]
    bkv_ref[:, 2 * LANES:3 * LANES] = rot(bkv[:, 2 * LANES:3 * LANES])
    bkv_ref[:, 3 * LANES:4 * LANES] = bkv[:, 3 * LANES:4 * LANES]
    bwin = cols(c0 + 4 * B_KV_WIDTH, 2 * B_KV_WIDTH)
    bwin_ref[:, 0:LANES] = rot(bwin[:, 0:LANES])
    bwin_ref[:, LANES:2 * LANES] = bwin[:, LANES:2 * LANES]
    gates_ref[...] = cols(N_MAIN, LANES)
    ga_ref[...] = cols(N_MAIN + LANES, D_MODEL)
    gb_ref[...] = cols(N_MAIN + LANES + D_MODEL, D_MODEL)


def _rope_tables(pos):
    half = HEAD_DIM // 2
    inv_freq = ROPE_THETA ** (-jnp.arange(half, dtype=F32) / half)
    ang = pos.astype(F32)[:, None] * inv_freq[None, :]
    cos, sin = jnp.cos(ang), jnp.sin(ang)
    reps = LANES // HEAD_DIM
    return jnp.tile(jnp.concatenate([cos, cos], axis=1), (1, reps)), jnp.tile(jnp.concatenate([-sin, sin], axis=1), (1, reps))


def _projection(x2d, pos, seq, g_attn, w_proj):
    n = x2d.shape[0]
    tm = min(Q_BLOCK, n)
    cs, sn = _rope_tables(pos)
    if seq >= tm:
        tab_map = lambda i: (i % (seq // tm), 0)
    else:
        cs, sn = jnp.tile(cs, (tm // seq, 1)), jnp.tile(sn, (tm // seq, 1))
        tab_map = lambda i: (0, 0)
    widths = (A_WIDTH, 2 * A_WIDTH, B_WIDTH, B_WIDTH, 4 * B_KV_WIDTH, 2 * B_KV_WIDTH, LANES, D_MODEL, D_MODEL)
    row = lambda i: (i, 0)
    fixed = lambda i: (0, 0)
    return pl.pallas_call(
        _proj_kernel,
        grid=(n // tm,),
        in_specs=[pl.BlockSpec((tm, D_MODEL), row), pl.BlockSpec((1, D_MODEL), fixed),
                  pl.BlockSpec((D_MODEL, N_PROJ), fixed), pl.BlockSpec((tm, LANES), tab_map),
                  pl.BlockSpec((tm, LANES), tab_map)],
        out_specs=[pl.BlockSpec((tm, w), row) for w in widths],
        out_shape=[jax.ShapeDtypeStruct((n, w), F32) for w in widths],
        compiler_params=_params(("parallel",)),
    )(x2d, g_attn.reshape(1, D_MODEL), w_proj, cs, sn)


def _rms_matmul_kernel(x_ref, g_ref, w_ref, o_ref):
    o_ref[...] = jnp.dot(_rms(x_ref[...], g_ref[...]).astype(BF16), w_ref[...], preferred_element_type=F32)


def _rms_matmul(x2d, g, w_bf16):
    n, d = x2d.shape
    m = w_bf16.shape[1]
    tm = min(Q_BLOCK, n)
    return pl.pallas_call(
        _rms_matmul_kernel,
        grid=(n // tm,),
        in_specs=[pl.BlockSpec((tm, d), lambda i: (i, 0)), pl.BlockSpec((1, d), lambda i: (0, 0)),
                  pl.BlockSpec((d, m), lambda i: (0, 0))],
        out_specs=pl.BlockSpec((tm, m), lambda i: (i, 0)),
        out_shape=jax.ShapeDtypeStruct((n, m), F32),
        compiler_params=_params(("parallel",)),
    )(x2d, g.reshape(1, d), w_bf16)


def _moba_prompt_kernel(q_ref, k_ref, v_ref, o_ref, *, seq):
    nb = seq // A_BLOCK
    k = k_ref[0]
    kb = k.astype(BF16)
    vb = v_ref[0].astype(BF16)
    lane = _lane_iota((1, LANES))
    nb_rows = -(-nb // SUBLANES) * SUBLANES
    kmean = jnp.concatenate(
        [jnp.mean(k[j * A_BLOCK:(j + 1) * A_BLOCK], axis=0, keepdims=True) for j in range(nb)]
        + [jnp.zeros((nb_rows - nb, LANES), F32)] * (nb_rows > nb), axis=0)
    blk_t = _row_iota((nb_rows, Q_BLOCK))
    expand = jnp.where(_row_iota((LANES, seq)) == _lane_iota((LANES, seq)) // A_BLOCK, 1.0, 0.0).astype(BF16)
    scale = HEAD_DIM ** -0.5
    for i in range(nb):
        qi = q_ref[0, i * Q_BLOCK:(i + 1) * Q_BLOCK, :]
        outs = []
        for hh in range(LANES // HEAD_DIM):
            head = (lane // HEAD_DIM) == hh
            qh = jnp.where(head, qi, 0.0)
            if i <= A_TOPK:
                sel = jnp.broadcast_to(jnp.where(lane < i, 1.0, 0.0), (Q_BLOCK, LANES))
            else:
                gate = _dot_t(kmean[0:nb_rows], qh, HIGHEST)
                gate = jnp.where(blk_t < i, gate, -jnp.inf)
                sel = _rows_to_lanes(jnp.where(blk_t < i, _topk_mask_t(gate, A_TOPK, blk_t.astype(F32)), 0.0))
            s = _dot_t((qh * scale).astype(BF16), kb[0:(i + 1) * A_BLOCK])
            chosen = jnp.dot(sel.astype(BF16), expand[:, 0:(i + 1) * A_BLOCK], preferred_element_type=F32) > 0.5
            key = _lane_iota((Q_BLOCK, (i + 1) * A_BLOCK)) - i * A_BLOCK
            own = (key >= 0) & (key <= _row_iota((Q_BLOCK, (i + 1) * A_BLOCK)))
            p, l = _softmax_rows(s, chosen | own)
            o = jnp.dot(p.astype(BF16), vb[0:(i + 1) * A_BLOCK], preferred_element_type=F32)
            outs.append(o * _safe_inv(l))
        o_ref[0, i * Q_BLOCK:(i + 1) * Q_BLOCK, :] = jnp.where((lane // HEAD_DIM) == 0, outs[0], outs[1])


def _moba_prompt(aq, akv):
    b, seq, _ = aq.shape
    hp = A_WIDTH // LANES
    return pl.pallas_call(
        functools.partial(_moba_prompt_kernel, seq=seq),
        grid=(b, hp),
        in_specs=[pl.BlockSpec((1, seq, LANES), lambda i, j: (i, 0, j)),
                  pl.BlockSpec((1, seq, LANES), lambda i, j: (i, 0, j)),
                  pl.BlockSpec((1, seq, LANES), lambda i, j: (i, 0, hp + j))],
        out_specs=pl.BlockSpec((1, seq, LANES), lambda i, j: (i, 0, j)),
        out_shape=jax.ShapeDtypeStruct((b, seq, A_WIDTH), F32),
        compiler_params=_params(("parallel", "parallel")),
    )(aq, akv, akv)


def _compress_rows(xk_ref, xv_ref, n_rows, pe_ref, wa_ref, wb_ref, w2_ref):
    acc_a = jnp.zeros((n_rows, 2 * LANES), F32)
    acc_b = jnp.zeros((n_rows, 2 * LANES), F32)
    for tt in range(CMP_STRIDE):
        rows_tt = pl.ds(tt, n_rows, stride=CMP_STRIDE)
        xt = jnp.concatenate([xk_ref[rows_tt, :], xv_ref[rows_tt, :]], axis=1)
        acc_a += jnp.dot((xt + pe_ref[tt:tt + 1, :]).astype(BF16), wa_ref[tt], preferred_element_type=F32)
        acc_b += jnp.dot((xt + pe_ref[CMP_STRIDE + tt:CMP_STRIDE + tt + 1, :]).astype(BF16), wb_ref[tt],
                         preferred_element_type=F32)
    hidden = acc_a + pltpu.roll(acc_b, n_rows - 1, 0)
    out = jnp.dot(jax.nn.gelu(hidden).astype(BF16), w2_ref[...], preferred_element_type=F32)
    return jnp.where(_row_iota(out.shape) < n_rows - 1, out, 0.0)


def _compress_prompt_kernel(xk_ref, xv_ref, pe_ref, wa_ref, wb_ref, w2_ref, o_ref, *, seq):
    o_ref[0] = _compress_rows(xk_ref.at[0], xv_ref.at[0], seq // CMP_STRIDE, pe_ref, wa_ref, wb_ref, w2_ref)


def _compress_weights(pe_cmp, w_ck1, w_ck2, w_cv1, w_cv2):
    def diag(mk, mv):
        z = jnp.zeros_like(mk)
        rows = [[mk, z, z, z], [z, mk, z, z], [z, z, mv, z], [z, z, z, mv]]
        return jnp.concatenate([jnp.concatenate(r, axis=-1) for r in rows], axis=-2)

    k1 = w_ck1.reshape(CMP_LEN, HEAD_DIM, CMP_HIDDEN)
    v1 = w_cv1.reshape(CMP_LEN, HEAD_DIM, CMP_HIDDEN)
    w1 = diag(k1, v1).astype(BF16)
    return jnp.tile(pe_cmp, (1, 4)), w1[:CMP_STRIDE], w1[CMP_STRIDE:], diag(w_ck2, w_cv2).astype(BF16)


def _compress_prompt(bkv, cw):
    b, seq, _ = bkv.shape
    n_rows = seq // CMP_STRIDE
    pe, wa, wb, w2 = cw
    full = lambda *s: pl.BlockSpec(s, lambda i: (0,) * len(s))
    return pl.pallas_call(
        functools.partial(_compress_prompt_kernel, seq=seq),
        grid=(b,),
        in_specs=[pl.BlockSpec((1, seq, LANES), lambda i: (i, 0, 0)), pl.BlockSpec((1, seq, LANES), lambda i: (i, 0, 1)),
                  full(CMP_LEN, 2 * LANES), full(CMP_STRIDE, 2 * LANES, 2 * LANES),
                  full(CMP_STRIDE, 2 * LANES, 2 * LANES), full(2 * LANES, 2 * LANES)],
        out_specs=pl.BlockSpec((1, n_rows, 2 * LANES), lambda i: (i, 0, 0)),
        out_shape=jax.ShapeDtypeStruct((b, n_rows, 2 * LANES), F32),
        compiler_params=_params(("parallel",)),
    )(bkv, bkv, pe, wa, wb, w2)


def _overlap_matrix(n_cmp, n_sel, rows, cols):
    c_start = np.arange(n_cmp)[:, None] * CMP_STRIDE
    s_start = np.arange(n_sel)[None, :] * SEL_BLOCK
    ov = np.clip(np.minimum(c_start + CMP_LEN, s_start + SEL_BLOCK) - np.maximum(c_start, s_start), 0, None)
    out = np.zeros((rows, cols), np.float32)
    out[:n_cmp, :n_sel] = ov / CMP_STRIDE
    return jnp.asarray(out)


def _group_halves(x, g):
    xg = jnp.where((_lane_iota((1, LANES)) // HEAD_DIM) == g, x, 0.0)
    return xg + pltpu.roll(xg, HEAD_DIM, 1)


def _gate_column(gates, head, branch):
    lane = _lane_iota((1, LANES))
    return _sigmoid(jnp.sum(jnp.where(lane == head * 3 + branch, gates, 0.0), axis=-1, keepdims=True))


def _nsa_select_kernel(q_ref, gates_ref, cmp_ref, ovt_ref, ocmp_ref, sel_ref, *, seq):
    g = pl.program_id(1)
    i = pl.program_id(2)
    n_cmp = (seq - CMP_LEN) // CMP_STRIDE + 1
    rows = B_GROUP * Q_BLOCK
    scale = HEAD_DIM ** -0.5
    lane = _lane_iota((1, LANES))
    both_halves = lambda x: _group_halves(x, g)

    parts = []
    for j in range(B_GROUP):
        x = q_ref[0, :, (j // 2) * LANES:(j // 2 + 1) * LANES]
        parts.append(jnp.where((lane // HEAD_DIM) == (j % 2), x, 0.0) * scale)
    q4 = jnp.concatenate(parts, axis=0).astype(BF16)
    qpos = i * Q_BLOCK + _row_iota((Q_BLOCK, 1))

    kc = both_halves(cmp_ref[0, :, 0:LANES]).astype(BF16)
    vc = both_halves(cmp_ref[0, :, LANES:2 * LANES]).astype(BF16)
    n_tok = kc.shape[0]
    tok = _lane_iota((1, n_tok))
    cmask = (tok < n_cmp) & (tok * CMP_STRIDE + (CMP_LEN - 1) <= qpos)
    s = _dot_t(q4, kc).reshape(B_GROUP, Q_BLOCK, n_tok)
    p, l = _softmax_rows(s, cmask[None])
    p_cmp = p * _safe_inv(l)
    o_cmp = jnp.dot(p_cmp.reshape(rows, n_tok).astype(BF16), vc, preferred_element_type=F32)
    n_sel = -(-seq // SEL_BLOCK)
    sel_rows = -(-n_sel // SUBLANES) * SUBLANES
    imp = _dot_t(ovt_ref[0:sel_rows, :], jnp.sum(p_cmp, axis=0), HIGHEST)
    blk = _row_iota((sel_rows, Q_BLOCK))
    cur = (i * Q_BLOCK + _lane_iota((1, Q_BLOCK))) // SEL_BLOCK
    forced = (blk == 0) | (blk == cur) | (blk == cur - 1)
    score = jnp.where(blk <= cur, jnp.where(forced, FORCE_SCORE, imp), -jnp.inf)
    sel = _topk_mask_t(score, min(SEL_TOPN, n_sel), blk.astype(F32))
    sel_ref[0, 0] = _rows_to_lanes(jnp.where(blk <= cur, sel, 0.0))

    gates = gates_ref[0]
    outs = [_gate_column(gates, g * B_GROUP + j, 0) * o_cmp[j * Q_BLOCK:(j + 1) * Q_BLOCK] for j in range(B_GROUP)]
    left = (lane // HEAD_DIM) == 0
    ocmp_ref[0] = jnp.concatenate([jnp.where(left, outs[0], outs[1]), jnp.where(left, outs[2], outs[3])], axis=1)


def _nsa_attend_kernel(qr_ref, ksel_ref, vsel_ref, kwin_ref, vwin_ref, sel_ref, gates_ref, ocmp_ref, o_ref, *, seq):
    chunk = pl.program_id(1)
    g = chunk // (B_GROUP // 2)
    nq = seq // Q_BLOCK
    scale = HEAD_DIM ** -0.5
    lane = _lane_iota((1, LANES))
    ks = _group_halves(ksel_ref[0], g).astype(BF16)
    vs = _group_halves(vsel_ref[0], g).astype(BF16)
    kw = _group_halves(kwin_ref[0], g).astype(BF16)
    vw = _group_halves(vwin_ref[0], g).astype(BF16)
    expand = jnp.where(_row_iota((LANES, seq)) == _lane_iota((LANES, seq)) // SEL_BLOCK, 1.0, 0.0).astype(BF16)
    for i in range(nq):
        rows = slice(i * Q_BLOCK, (i + 1) * Q_BLOCK)
        n_keys = (i + 1) * Q_BLOCK
        w0 = max(i * Q_BLOCK - WINDOW, 0)
        qpos = i * Q_BLOCK + _row_iota((Q_BLOCK, 1))
        chosen = jnp.dot(sel_ref[0, 0, rows, :].astype(BF16), expand[:, 0:n_keys], preferred_element_type=F32) > 0.5
        smask = chosen & (_lane_iota((1, n_keys)) <= qpos)
        wpos = w0 + _lane_iota((1, n_keys - w0))
        wmask = (wpos <= qpos) & (wpos > qpos - WINDOW)
        qi = qr_ref[0, rows, :]
        gates = gates_ref[0, rows, :]
        outs = []
        for hh in range(LANES // HEAD_DIM):
            qh = (jnp.where((lane // HEAD_DIM) == hh, qi, 0.0) * scale).astype(BF16)
            p, l = _softmax_rows(_dot_t(qh, ks[0:n_keys]), smask)
            o_sel = jnp.dot(p.astype(BF16), vs[0:n_keys], preferred_element_type=F32) * _safe_inv(l)
            p, l = _softmax_rows(_dot_t(qh, kw[w0:n_keys]), wmask)
            o_win = jnp.dot(p.astype(BF16), vw[w0:n_keys], preferred_element_type=F32) * _safe_inv(l)
            head = chunk * (LANES // HEAD_DIM) + hh
            outs.append(_gate_column(gates, head, 1) * o_sel + _gate_column(gates, head, 2) * o_win)
        o_ref[0, rows, :] = ocmp_ref[0, rows, :] + jnp.where((lane // HEAD_DIM) == 0, outs[0], outs[1])


def _nsa_prompt(bq, bqr, gates, bkv, bwin, cmp_tok):
    b, seq, _ = bq.shape
    nq = seq // Q_BLOCK
    n_tok = cmp_tok.shape[1]
    n_cmp = (seq - CMP_LEN) // CMP_STRIDE + 1
    n_sel = -(-seq // SEL_BLOCK)
    assert n_sel <= LANES
    overlap_t = _overlap_matrix(n_cmp, n_sel, n_tok, LANES).T
    ocmp, sel = pl.pallas_call(
        functools.partial(_nsa_select_kernel, seq=seq),
        grid=(b, B_KV_HEADS, nq),
        in_specs=[pl.BlockSpec((1, Q_BLOCK, 2 * LANES), lambda i, g, t: (i, t, g)),
                  pl.BlockSpec((1, Q_BLOCK, LANES), lambda i, g, t: (i, t, 0)),
                  pl.BlockSpec((1, n_tok, 2 * LANES), lambda i, g, t: (i, 0, 0)),
                  pl.BlockSpec((LANES, n_tok), lambda i, g, t: (0, 0))],
        out_specs=[pl.BlockSpec((1, Q_BLOCK, 2 * LANES), lambda i, g, t: (i, t, g)),
                   pl.BlockSpec((1, 1, Q_BLOCK, LANES), lambda i, g, t: (i, g, t, 0))],
        out_shape=[jax.ShapeDtypeStruct((b, seq, B_WIDTH), F32),
                   jax.ShapeDtypeStruct((b, B_KV_HEADS, seq, LANES), F32)],
        compiler_params=_params(("parallel", "parallel", "parallel")),
    )(bq, gates, cmp_tok, overlap_t)
    chunks = B_WIDTH // LANES
    per_group = chunks // B_KV_HEADS
    own = pl.BlockSpec((1, seq, LANES), lambda i, c: (i, 0, c))
    col = lambda k: pl.BlockSpec((1, seq, LANES), lambda i, c: (i, 0, k))
    return pl.pallas_call(
        functools.partial(_nsa_attend_kernel, seq=seq),
        grid=(b, chunks),
        in_specs=[own, col(2), col(3), col(0), col(1),
                  pl.BlockSpec((1, 1, seq, LANES), lambda i, c: (i, c // per_group, 0, 0)),
                  pl.BlockSpec((1, seq, LANES), lambda i, c: (i, 0, 0)), own],
        out_specs=own,
        out_shape=jax.ShapeDtypeStruct((b, seq, B_WIDTH), F32),
        compiler_params=_params(("parallel", "parallel")),
    )(bqr, bkv, bkv, bwin, bwin, sel, gates, ocmp)


def _merge_kernel(x_ref, oa_ref, ob_ref, ga_ref, gb_ref, pa_ref, pb_ref, wo_ref, h_ref):
    ya = jnp.dot(oa_ref[...].astype(BF16), pa_ref[...], preferred_element_type=F32)
    yb = jnp.dot(ob_ref[...].astype(BF16), pb_ref[...], preferred_element_type=F32)
    mixed = _sigmoid(ga_ref[...]) * ya + _sigmoid(gb_ref[...]) * yb
    h_ref[...] = x_ref[...] + jnp.dot(mixed.astype(BF16), wo_ref[...], preferred_element_type=F32)


def _merge(x2d, oa, ob, ga, gb, pa, pb, wo):
    n = x2d.shape[0]
    tm = min(Q_BLOCK, n)
    row = lambda w: pl.BlockSpec((tm, w), lambda i: (i, 0))
    full = lambda a: pl.BlockSpec(a.shape, lambda i: (0, 0))
    return pl.pallas_call(
        _merge_kernel,
        grid=(n // tm,),
        in_specs=[row(D_MODEL), row(A_WIDTH), row(B_WIDTH), row(D_MODEL), row(D_MODEL), full(pa), full(pb), full(wo)],
        out_specs=row(D_MODEL),
        out_shape=jax.ShapeDtypeStruct((n, D_MODEL), F32),
        compiler_params=_params(("parallel",)),
    )(x2d, oa, ob, ga, gb, pa, pb, wo)


def _cross_kernel(h_ref, mem_ref, g_ref, wq_ref, wo_ref, o_ref):
    h = h_ref[0]
    q = jnp.dot(_rms(h, g_ref[...]).astype(BF16), wq_ref[...], preferred_element_type=F32)
    scale = C_HEAD_DIM ** -0.5
    outs = []
    for hd in range(C_HEADS):
        c = slice(hd * C_HEAD_DIM, (hd + 1) * C_HEAD_DIM)
        kh = mem_ref[0, :, c].astype(BF16)
        vh = mem_ref[0, :, C_WIDTH + hd * C_HEAD_DIM:C_WIDTH + (hd + 1) * C_HEAD_DIM].astype(BF16)
        s = _dot_t(q[:, c].astype(BF16), kh) * scale
        m = jnp.max(s, axis=-1, keepdims=True)
        p = jnp.exp(s - m)
        o = jnp.dot(p.astype(BF16), vh, preferred_element_type=F32)
        outs.append(o / jnp.sum(p, axis=-1, keepdims=True))
    att = jnp.concatenate(outs, axis=1).astype(BF16)
    o_ref[0] = h + jnp.dot(att, wo_ref[...], preferred_element_type=F32)


def _cross(h3, mem_kv, g_cross, wq, wo):
    b, t, _ = h3.shape
    tq = min(Q_BLOCK, t)
    mlen = mem_kv.shape[1]
    return pl.pallas_call(
        _cross_kernel,
        grid=(b, t // tq),
        in_specs=[pl.BlockSpec((1, tq, D_MODEL), lambda i, j: (i, j, 0)),
                  pl.BlockSpec((1, mlen, 2 * C_WIDTH), lambda i, j: (i, 0, 0)),
                  pl.BlockSpec((1, D_MODEL), lambda i, j: (0, 0)),
                  pl.BlockSpec(wq.shape, lambda i, j: (0, 0)), pl.BlockSpec(wo.shape, lambda i, j: (0, 0))],
        out_specs=pl.BlockSpec((1, tq, D_MODEL), lambda i, j: (i, j, 0)),
        out_shape=jax.ShapeDtypeStruct((b, t, D_MODEL), F32),
        compiler_params=_params(("parallel", "parallel")),
    )(h3, mem_kv, g_cross.reshape(1, D_MODEL), wq, wo)


def _peer_candidate_ids():
    ids = -np.ones((7 * SUBLANES,), np.float32)
    layout = [(0, 0, 16), (1, 16, 8), (2, 24, 5), (3, 32, 4), (4, 36, 3), (5, 40, 2), (6, 42, 2), (7, 44, 2)]
    layout += [(a, 40 + a, 1) for a in range(8, 16)]
    for a, row, n in layout:
        assert (a + 1) * n <= P_TOPK < (a + 1) * (n + 1)
        ids[row:row + n] = a * P_TOPK + np.arange(n)
    return jnp.asarray(np.tile(ids[:, None], (1, LANES)))


def _peer_select_kernel(h_ref, g_ref, wq_ref, k1_ref, k2_ref, cid_ref, z_ref, eid_ref, gate_ref, s_ref, e_scr, w_scr,
                        *, tm):
    z = _rms(h_ref[...], g_ref[...])
    z_ref[...] = z
    zb = z.astype(BF16)
    half = P_QDIM // 2
    k1 = k1_ref[...]
    k2 = k2_ref[...]
    for hd in range(P_HEADS):
        q = jnp.dot(zb, wq_ref[:, hd * P_QDIM:(hd + 1) * P_QDIM], preferred_element_type=F32)
        s1 = _dot_t(k1, q[:, 0:half], HIGHEST)
        s2 = _dot_t(k2, q[:, half:P_QDIM], HIGHEST)
        for tb in range(tm // LANES):
            s_ref[hd, tb, 0] = s1[:, tb * LANES:(tb + 1) * LANES]
            s_ref[hd, tb, 1] = s2[:, tb * LANES:(tb + 1) * LANES]

    key_f = _row_iota((P_NKEYS, LANES)).astype(F32)
    cand_valid = cid_ref[...] >= 0.0
    cand_f = jnp.where(cand_valid, cid_ref[...], float(P_TOPK * P_TOPK))
    r8 = _row_iota((SUBLANES, LANES))

    def pair_rows(x1, x2s, combine):
        x1s = jnp.concatenate(x1, axis=0)
        x2_8 = x2s[0:SUBLANES]
        shift = lambda k: pltpu.roll(x2_8, k, 0)
        return jnp.concatenate([
            combine(x1[0], x2s), combine(x1[1], x2_8), combine(x1[2], x2_8),
            jnp.where(r8 < 4, combine(x1[3], x2_8), combine(x1[4], shift(4))),
            jnp.where(r8 < 2, combine(x1[5], x2_8),
                      jnp.where(r8 < 4, combine(x1[6], shift(2)), combine(x1[7], shift(4)))),
            combine(x1s[SUBLANES:2 * SUBLANES], x2s[0:1])], axis=0)

    def take_top(x, ids_f, limit):
        mx = jnp.max(x, axis=0, keepdims=True)
        first = jnp.min(jnp.where(x == mx, ids_f, limit), axis=0, keepdims=True)
        hit = ids_f == first
        return mx, first, hit, jnp.where(hit, -jnp.inf, x)

    def select(x1, x2):
        v1, i1, v2, i2 = [], [], [], []
        for _ in range(P_TOPK):
            m, a, _, x1 = take_top(x1, key_f, float(P_NKEYS))
            v1.append(m)
            i1.append(a)
            m, a, _, x2 = take_top(x2, key_f, float(P_NKEYS))
            v2.append(m)
            i2.append(a)
        v2s = jnp.concatenate(v2, axis=0)
        i2s = jnp.concatenate(i2, axis=0)
        cand = jnp.where(cand_valid, pair_rows(v1, v2s, lambda a, b: a + b), -jnp.inf)
        cid = pair_rows(i1, i2s, lambda a, b: a * float(P_NKEYS) + b)
        es, ss = [], []
        for _ in range(P_TOPK):
            mx, _, hit, cand = take_top(cand, cand_f, float(P_TOPK * P_TOPK))
            es.append(jnp.max(jnp.where(hit, cid, -1.0), axis=0, keepdims=True))
            ss.append(mx)
        ex = jnp.exp(jnp.concatenate(ss, axis=0) - ss[0])
        return jnp.concatenate(es, axis=0), ex / jnp.sum(ex, axis=0, keepdims=True)

    def step(hd, carry):
        rows = pl.ds(pl.multiple_of(hd * P_TOPK, P_TOPK), P_TOPK)
        for tb in range(tm // LANES):
            e16, w16 = select(s_ref[hd, tb, 0], s_ref[hd, tb, 1])
            e_scr[tb, rows, :] = e16
            w_scr[tb, rows, :] = w16
        return carry

    lax.fori_loop(0, P_HEADS, step, 0)
    for tb in range(tm // LANES):
        eid_ref[tb * LANES:(tb + 1) * LANES, :] = jnp.transpose(e_scr[tb]).astype(jnp.int32)
        gate_ref[tb * LANES:(tb + 1) * LANES, :] = jnp.transpose(w_scr[tb])


def _peer_select(h2d, g_ffn, wq, k1, k2):
    n = h2d.shape[0]
    tm = min(Q_BLOCK, n)
    assert tm % LANES == 0 and n % tm == 0
    cand_ids = _peer_candidate_ids()
    row = lambda: pl.BlockSpec((tm, D_MODEL), lambda i: (i, 0))
    pick = lambda: pl.BlockSpec((tm, LANES), lambda i: (i, 0))
    full = lambda a: pl.BlockSpec(a.shape, lambda i: (0, 0))
    return pl.pallas_call(
        functools.partial(_peer_select_kernel, tm=tm),
        grid=(n // tm,),
        in_specs=[row(), pl.BlockSpec((1, D_MODEL), lambda i: (0, 0)), full(wq), full(k1), full(k2), full(cand_ids)],
        out_specs=[row(), pick(), pick()],
        out_shape=[jax.ShapeDtypeStruct((n, D_MODEL), F32), jax.ShapeDtypeStruct((n, LANES), jnp.int32),
                   jax.ShapeDtypeStruct((n, LANES), F32)],
        scratch_shapes=[pltpu.VMEM((P_HEADS, tm // LANES, 2, P_NKEYS, LANES), F32),
                        pltpu.VMEM((tm // LANES, P_PICKS, LANES), F32), pltpu.VMEM((tm // LANES, P_PICKS, LANES), F32)],
        compiler_params=_params(("parallel",)),
    )(h2d, g_ffn.reshape(1, D_MODEL), wq, k1, k2, cand_ids)


PEER_TOKENS = 8
ROW_TILES = D_MODEL // LANES


def _peer_gather_kernel(eid_ref, h_ref, z_ref, gate_ref, gfin_ref, uv_hbm, y_ref, buf, sem):
    j = pl.program_id(0)
    n_tiles = pl.num_programs(0) - 1
    rows = PEER_TOKENS * P_PICKS

    for to_slot in range(2):
        @pl.when((j < n_tiles) & (j % 2 == to_slot))
        def _(to_slot=to_slot):
            for r in range(rows):
                e = eid_ref[r // P_PICKS, r % P_PICKS]
                pltpu.make_async_copy(uv_hbm.at[e], buf.at[to_slot, pl.ds(r, 1)], sem.at[to_slot]).start()

    @pl.when(j > 0)
    def _():
        slot = (j - 1) % 2
        pltpu.make_async_copy(uv_hbm.at[pl.ds(0, rows), 0], buf.at[slot], sem.at[slot]).wait()
        gate_t = jnp.transpose(jnp.concatenate(
            [gate_ref[...], jnp.zeros((LANES - PEER_TOKENS, P_PICKS), F32)], axis=0))
        for p in range(PEER_TOKENS):
            picks = pl.ds(p * P_PICKS, P_PICKS)
            acc = jnp.zeros((P_PICKS, LANES), F32)
            for s in range(ROW_TILES):
                acc += buf[slot, picks, s * LANES:(s + 1) * LANES] * z_ref[p:p + 1, s * LANES:(s + 1) * LANES]
            act = jax.nn.gelu(jnp.sum(acc, axis=-1, keepdims=True))
            coef = jnp.broadcast_to(gate_t[:, p:p + 1] * act, (P_PICKS, LANES))
            outs = []
            for s in range(ROW_TILES):
                v_s = buf[slot, picks, D_MODEL + s * LANES:D_MODEL + (s + 1) * LANES]
                outs.append(jnp.sum(coef * v_s, axis=0, keepdims=True))
            y_ref[p:p + 1, :] = h_ref[p:p + 1, :] + jnp.concatenate(outs, axis=1)
        y_ref[...] = _rms(y_ref[...], gfin_ref[...])


def _peer_gather(eid, h2d, z, gate, g_final, uv):
    n = h2d.shape[0]
    tiles = n // PEER_TOKENS
    ahead = lambda j: (jnp.minimum(j, tiles - 1), 0)
    behind = lambda j: (jnp.maximum(j - 1, 0), 0)
    row = lambda w: pl.BlockSpec((PEER_TOKENS, w), behind)
    return pl.pallas_call(
        _peer_gather_kernel,
        grid=(tiles + 1,),
        in_specs=[pl.BlockSpec((PEER_TOKENS, P_PICKS), ahead, memory_space=pltpu.SMEM),
                  row(D_MODEL), row(D_MODEL), row(P_PICKS), pl.BlockSpec((1, D_MODEL), lambda j: (0, 0)),
                  pl.BlockSpec(memory_space=pl.ANY)],
        out_specs=row(D_MODEL),
        out_shape=jax.ShapeDtypeStruct((n, D_MODEL), F32),
        scratch_shapes=[pltpu.VMEM((2, PEER_TOKENS * P_PICKS, 2 * D_MODEL), F32),
                        pltpu.SemaphoreType.DMA((2,))],
        compiler_params=_params(("arbitrary",)),
    )(eid, h2d, z, gate, g_final.reshape(1, D_MODEL), uv)


PAGES_PER_STEP = 16
SAMPLE_ROWS = LANES


def _page_specs(rows, row_block):
    def spec(k):
        return pl.BlockSpec((1, rows, PAGE_SIZE), lambda b, c, pt: (pt[b, c * PAGES_PER_STEP + k], row_block, 0))
    return [spec(k) for k in range(PAGES_PER_STEP)]


def _online_step(s, mask, pv_fn, m_ref, l_ref, acc_ref):
    s = jnp.where(mask, s, NEG_BIG)
    m_old = m_ref[...]
    m_new = jnp.maximum(m_old, jnp.max(s, axis=-1, keepdims=True))
    p = jnp.where(mask, jnp.exp(s - m_new), 0.0)
    alpha = jnp.exp(m_old - m_new)
    l_ref[...] = alpha * l_ref[...] + jnp.sum(p, axis=-1, keepdims=True)
    acc_ref[...] = alpha * acc_ref[...] + pv_fn(p.astype(BF16))
    m_ref[...] = m_new


def _online_update(s, mask, v, m_ref, l_ref, acc_ref):
    _online_step(s, mask, lambda p: jnp.dot(p, v, preferred_element_type=F32), m_ref, l_ref, acc_ref)


def _online_update_t(s, mask, v_t, m_ref, l_ref, acc_ref):
    _online_step(s, mask, lambda p: jnp.transpose(_dot_t(v_t, p)), m_ref, l_ref, acc_ref)


def _moba_kmean_kernel(pt_ref, *refs):
    pages, o_ref = refs[:PAGES_PER_STEP], refs[PAGES_PER_STEP]
    c = pl.program_id(1)

    @pl.when(c == 0)
    def _():
        o_ref[...] = jnp.zeros(o_ref.shape, F32)

    n_keys = PAGES_PER_STEP * PAGE_SIZE
    n_lanes = o_ref.shape[2]
    k_t = jnp.concatenate([p[0] for p in pages], axis=1)
    blk = c * (n_keys // A_BLOCK) + _row_iota((n_keys, n_lanes)) // A_BLOCK
    avg = jnp.where(_lane_iota((n_keys, n_lanes)) == blk, 1.0 / A_BLOCK, 0.0).astype(BF16)
    k_hi = k_t.astype(BF16)
    k_lo = (k_t - k_hi.astype(F32)).astype(BF16)
    o_ref[0] += (jnp.dot(k_hi, avg, preferred_element_type=F32) + jnp.dot(k_lo, avg, preferred_element_type=F32))


def _moba_kmean(pool_t, page_table, n_lanes):
    b, n_pages = page_table.shape
    return pl.pallas_call(
        _moba_kmean_kernel,
        grid_spec=pltpu.PrefetchScalarGridSpec(
            num_scalar_prefetch=1, grid=(b, n_pages // PAGES_PER_STEP),
            in_specs=_page_specs(A_WIDTH, 0),
            out_specs=pl.BlockSpec((1, A_WIDTH, n_lanes), lambda i, c, pt: (i, 0, 0))),
        out_shape=jax.ShapeDtypeStruct((b, A_WIDTH, n_lanes), F32),
        compiler_params=_params(("parallel", "arbitrary")),
    )(page_table, *([pool_t] * PAGES_PER_STEP))


def _moba_sample_kernel(pt_ref, *refs, past, t_new):
    pages = refs[:PAGES_PER_STEP]
    kmean_ref, q_ref, new_ref, o_ref, sel_ref, m_ref, l_ref, acc_ref = refs[PAGES_PER_STEP:]
    c = pl.program_id(1)
    rows = A_HEADS * t_new
    n_lanes = sel_ref.shape[1]
    cur = past // A_BLOCK
    q = q_ref[0]
    qb = (q * HEAD_DIM ** -0.5).astype(BF16)

    @pl.when(c == 0)
    def _():
        lane = _lane_iota((1, n_lanes))
        gate = jnp.dot(q, kmean_ref[0], precision=HIGHEST, preferred_element_type=F32)
        gate = jnp.where(lane < cur, gate, -jnp.inf)
        sel = _topk_mask(gate, min(A_TOPK, cur + 1), lane.astype(F32))
        sel_ref[...] = jnp.where(lane < cur, sel, 0.0)
        m_ref[...] = jnp.full(m_ref.shape, NEG_BIG, F32)
        l_ref[...] = jnp.zeros(l_ref.shape, F32)
        acc_ref[...] = jnp.zeros(acc_ref.shape, F32)

    n_keys = PAGES_PER_STEP * PAGE_SIZE
    k_t = jnp.concatenate([p[0, 0:A_WIDTH, :] for p in pages], axis=1).astype(BF16)
    v_t = jnp.concatenate([p[0, A_WIDTH:2 * A_WIDTH, :] for p in pages], axis=1).astype(BF16)
    blk = c * (n_keys // A_BLOCK) + _lane_iota((n_lanes, n_keys)) // A_BLOCK
    expand = jnp.where(_row_iota((n_lanes, n_keys)) == blk, 1.0, 0.0).astype(BF16)
    chosen = jnp.dot(sel_ref[...].astype(BF16), expand, preferred_element_type=F32) > 0.5
    _online_update_t(jnp.dot(qb, k_t, preferred_element_type=F32), chosen, v_t, m_ref, l_ref, acc_ref)

    @pl.when(c == pl.num_programs(1) - 1)
    def _():
        kn = new_ref[0, :, 0:A_WIDTH].astype(BF16)
        vn = new_ref[0, :, A_WIDTH:2 * A_WIDTH].astype(BF16)
        t_key = _lane_iota((SAMPLE_ROWS, kn.shape[0]))
        t_row = _row_iota((SAMPLE_ROWS, kn.shape[0])) % t_new
        _online_update(_dot_t(qb, kn), (t_key <= t_row) & (t_key < t_new), vn, m_ref, l_ref, acc_ref)
        own = (_lane_iota((SAMPLE_ROWS, A_WIDTH)) // HEAD_DIM) == (_row_iota((SAMPLE_ROWS, A_WIDTH)) // t_new)
        o_ref[0] = jnp.where(own, acc_ref[...] * _safe_inv(l_ref[...]), 0.0)[0:rows]


def _moba_sample(pool_t, page_table, kmean_t, q_rows, new_kv, past, t_new):
    b, n_pages = page_table.shape
    rows = A_HEADS * t_new
    n_lanes = kmean_t.shape[2]
    per_b = lambda s: pl.BlockSpec((1,) + s, lambda i, c, pt: (i, 0, 0))
    return pl.pallas_call(
        functools.partial(_moba_sample_kernel, past=past, t_new=t_new),
        grid_spec=pltpu.PrefetchScalarGridSpec(
            num_scalar_prefetch=1, grid=(b, n_pages // PAGES_PER_STEP),
            in_specs=_page_specs(2 * A_WIDTH, 0) + [per_b(kmean_t.shape[1:]), per_b(q_rows.shape[1:]),
                                                     per_b(new_kv.shape[1:])],
            out_specs=per_b((rows, A_WIDTH)),
            scratch_shapes=[pltpu.VMEM((SAMPLE_ROWS, n_lanes), F32), pltpu.VMEM((SAMPLE_ROWS, 1), F32),
                            pltpu.VMEM((SAMPLE_ROWS, 1), F32), pltpu.VMEM((SAMPLE_ROWS, A_WIDTH), F32)]),
        out_shape=jax.ShapeDtypeStruct((b, rows, A_WIDTH), F32),
        compiler_params=_params(("parallel", "arbitrary")),
    )(page_table, *([pool_t] * PAGES_PER_STEP), kmean_t, q_rows, new_kv)


def _compress_sample_kernel(pt_ref, *refs, past):
    pages = refs[:PAGES_PER_STEP]
    pe_ref, wa_ref, wb_ref, w2_ref, o_ref, xk_ref, xv_ref = refs[PAGES_PER_STEP:]
    c = pl.program_id(1)
    for k, page in enumerate(pages):
        start = pl.multiple_of((c * PAGES_PER_STEP + k) * PAGE_SIZE, PAGE_SIZE)
        xk_ref[pl.ds(start, PAGE_SIZE), :] = jnp.transpose(page[0, 0:LANES, :])
        xv_ref[pl.ds(start, PAGE_SIZE), :] = jnp.transpose(page[0, LANES:2 * LANES, :])

    @pl.when(c == pl.num_programs(1) - 1)
    def _():
        o_ref[0] = _compress_rows(xk_ref, xv_ref, past // CMP_STRIDE, pe_ref, wa_ref, wb_ref, w2_ref)


def _compress_sample(pool, page_table, cw, past):
    b, n_pages = page_table.shape
    n_rows = past // CMP_STRIDE
    pe, wa, wb, w2 = cw
    full = lambda a: pl.BlockSpec(a.shape, lambda i, c, pt: (0,) * a.ndim)
    return pl.pallas_call(
        functools.partial(_compress_sample_kernel, past=past),
        grid_spec=pltpu.PrefetchScalarGridSpec(
            num_scalar_prefetch=1, grid=(b, n_pages // PAGES_PER_STEP),
            in_specs=_page_specs(2 * LANES, 0) + [full(pe), full(wa), full(wb), full(w2)],
            out_specs=pl.BlockSpec((1, n_rows, 2 * LANES), lambda i, c, pt: (i, 0, 0)),
            scratch_shapes=[pltpu.VMEM((past, LANES), F32), pltpu.VMEM((past, LANES), F32)]),
        out_shape=jax.ShapeDtypeStruct((b, n_rows, 2 * LANES), F32),
        compiler_params=_params(("parallel", "arbitrary")),
    )(page_table, *([pool] * PAGES_PER_STEP), pe, wa, wb, w2)


def _nsa_sample_kernel(pt_ref, *refs, past, t_new):
    pages = refs[:PAGES_PER_STEP]
    (cmp_ref, q_ref, qr_ref, gates_ref, newsel_ref, win_ref, newwin_ref, ov_ref, o_ref,
     kv_ref, m_ref, l_ref, acc_ref) = refs[PAGES_PER_STEP:]
    c = pl.program_id(1)
    for k, page in enumerate(pages):
        kv_ref[c * PAGES_PER_STEP + k] = page[0]

    @pl.when(c == pl.num_programs(1) - 1)
    def _():
        rows = B_HEADS * t_new
        slab = B_KV_HEADS * t_new
        scale = HEAD_DIM ** -0.5
        qb = (q_ref[0] * scale).astype(BF16)
        qrb = (qr_ref[0] * scale).astype(BF16)
        pos = past + _row_iota((SAMPLE_ROWS, 1)) % t_new

        n_tok = cmp_ref.shape[1]
        n_cmp = n_tok - 1
        kc = cmp_ref[0, :, 0:LANES].astype(BF16)
        vc = cmp_ref[0, :, LANES:2 * LANES].astype(BF16)
        tok = _lane_iota((1, n_tok))
        p, l = _softmax_rows(_dot_t(qb, kc), (tok < n_cmp) & (tok * CMP_STRIDE + (CMP_LEN - 1) <= pos))
        p_cmp = p * _safe_inv(l)
        o_cmp = jnp.dot(p_cmp.astype(BF16), vc, preferred_element_type=F32)
        p_sum = sum(p_cmp[j * slab:(j + 1) * slab] for j in range(B_GROUP))
        imp = jnp.dot(p_sum, ov_ref[...], precision=HIGHEST, preferred_element_type=F32)
        n_lanes = ov_ref.shape[1]
        lane = _lane_iota((1, n_lanes))
        cur = past // SEL_BLOCK
        forced = (lane == 0) | (lane == cur) | (lane == cur - 1)
        score = jnp.where(lane <= cur, jnp.where(forced, FORCE_SCORE, imp), -jnp.inf)
        sel = _topk_mask(score, min(SEL_TOPN, cur + 1), lane.astype(F32))
        sel = jnp.where(lane <= cur, sel, 0.0)
        sel_b = sel.astype(BF16)

        m_ref[...] = jnp.full(m_ref.shape, NEG_BIG, F32)
        l_ref[...] = jnp.zeros(l_ref.shape, F32)
        acc_ref[...] = jnp.zeros(acc_ref.shape, F32)
        n_keys = PAGES_PER_STEP * PAGE_SIZE
        tile_rows = lambda a: jnp.concatenate([a] * (SAMPLE_ROWS // slab), axis=0)
        for ci in range(past // n_keys):
            pages_ci = range(ci * PAGES_PER_STEP, (ci + 1) * PAGES_PER_STEP)
            ks_t = jnp.concatenate([kv_ref[pg, 0:LANES, :] for pg in pages_ci], axis=1).astype(BF16)
            vs_t = jnp.concatenate([kv_ref[pg, LANES:2 * LANES, :] for pg in pages_ci], axis=1).astype(BF16)
            blk = (ci * n_keys + _lane_iota((n_lanes, n_keys))) // SEL_BLOCK
            expand = jnp.where(_row_iota((n_lanes, n_keys)) == blk, 1.0, 0.0).astype(BF16)
            chosen = jnp.dot(sel_b, expand, preferred_element_type=F32) > 0.5
            _online_update_t(jnp.dot(qrb, ks_t, preferred_element_type=F32), tile_rows(chosen), vs_t,
                             m_ref, l_ref, acc_ref)
        kn = newsel_ref[0, :, 0:LANES].astype(BF16)
        vn = newsel_ref[0, :, LANES:2 * LANES].astype(BF16)
        t_key = _lane_iota((SAMPLE_ROWS, kn.shape[0]))
        cur_chosen = tile_rows(jnp.sum(jnp.where(lane == cur, sel, 0.0), axis=-1, keepdims=True)) > 0.5
        _online_update(_dot_t(qrb, kn), cur_chosen & (past + t_key <= pos) & (t_key < t_new), vn,
                       m_ref, l_ref, acc_ref)
        o_sel = acc_ref[...] * _safe_inv(l_ref[...])

        n_win = win_ref.shape[1]
        kw = jnp.concatenate([win_ref[0, :, 0:LANES], newwin_ref[0, :, 0:LANES]], axis=0).astype(BF16)
        vw = jnp.concatenate([win_ref[0, :, LANES:2 * LANES], newwin_ref[0, :, LANES:2 * LANES]], axis=0).astype(BF16)
        wpos = past - n_win + _lane_iota((1, kw.shape[0]))
        p, l = _softmax_rows(_dot_t(qrb, kw), (wpos <= pos) & (wpos > pos - WINDOW))
        o_win = jnp.dot(p.astype(BF16), vw, preferred_element_type=F32) * _safe_inv(l)

        gt = _sigmoid(gates_ref[0])
        o = gt[:, 0:1] * o_cmp + gt[:, 1:2] * o_sel + gt[:, 2:3] * o_win
        shape = (SAMPLE_ROWS, LANES)
        own = (_lane_iota(shape) // HEAD_DIM) == ((_row_iota(shape) // t_new) % B_KV_HEADS)
        o_ref[0] = jnp.where(own, o, 0.0)[0:rows]


def _nsa_sample(pool_t, page_table, cmp_tok, q_rows, qr_rows, gate_rows, new_sel, win_state, new_win, past, t_new):
    b, n_pages = page_table.shape
    rows = B_HEADS * t_new
    n_tok = cmp_tok.shape[1]
    n_sel = past // SEL_BLOCK + 1
    n_lanes = -(-n_sel // LANES) * LANES
    overlap = _overlap_matrix(n_tok - 1, n_sel, n_tok, n_lanes)
    per_b = lambda a: pl.BlockSpec((1,) + a.shape[1:], lambda i, c, pt: (i, 0, 0))
    return pl.pallas_call(
        functools.partial(_nsa_sample_kernel, past=past, t_new=t_new),
        grid_spec=pltpu.PrefetchScalarGridSpec(
            num_scalar_prefetch=1, grid=(b, n_pages // PAGES_PER_STEP),
            in_specs=_page_specs(2 * LANES, 1) + [per_b(cmp_tok), per_b(q_rows), per_b(qr_rows), per_b(gate_rows),
                                                  per_b(new_sel), per_b(win_state), per_b(new_win),
                                                  pl.BlockSpec(overlap.shape, lambda i, c, pt: (0, 0))],
            out_specs=pl.BlockSpec((1, rows, LANES), lambda i, c, pt: (i, 0, 0)),
            scratch_shapes=[pltpu.VMEM((n_pages, 2 * LANES, PAGE_SIZE), F32), pltpu.VMEM((SAMPLE_ROWS, 1), F32),
                            pltpu.VMEM((SAMPLE_ROWS, 1), F32), pltpu.VMEM((SAMPLE_ROWS, LANES), F32)]),
        out_shape=jax.ShapeDtypeStruct((b, rows, LANES), F32),
        compiler_params=_params(("parallel", "arbitrary")),
    )(page_table, *([pool_t] * PAGES_PER_STEP), cmp_tok, q_rows, qr_rows, gate_rows, new_sel, win_state, new_win,
      overlap)


def _layer_weights(w_in, pe_cmp, w_ck1, w_ck2, w_cv1, w_cv2, p_a, p_b, w_o, w_cq, w_ckv, w_co, w_pq, peer_u, peer_v):
    w_proj = jnp.concatenate(
        [w_in[:, :N_MAIN], jnp.pad(w_in[:, N_MAIN:N_MAIN + N_GATES], ((0, 0), (0, LANES - N_GATES))),
         w_in[:, N_MAIN + N_GATES:]], axis=1).astype(BF16)
    uv = jnp.concatenate([peer_u, peer_v], axis=1)[:, None, :]
    return dict(w_proj=w_proj, cw=_compress_weights(pe_cmp, w_ck1, w_ck2, w_cv1, w_cv2),
                p_a=p_a.astype(BF16), p_b=p_b.astype(BF16), w_o=w_o.astype(BF16), w_cq=w_cq.astype(BF16),
                w_ckv=w_ckv.astype(BF16), w_co=w_co.astype(BF16), w_pq=w_pq.astype(BF16),
                uv=uv)


def _channel_and_norm(h2d, batch, mem_kv, w, g_cross, g_ffn, sub_k1, sub_k2, g_final):
    h3 = _cross(h2d.reshape(batch, -1, D_MODEL), mem_kv, g_cross, w["w_cq"], w["w_co"])
    h2 = h3.reshape(-1, D_MODEL)
    z, eid, gate = _peer_select(h2, g_ffn, w["w_pq"], sub_k1, sub_k2)
    return _peer_gather(eid, h2, z, gate, g_final, w["uv"])


def _prompt_group(x, mem, w, g_attn, g_cross, g_mem, g_ffn, sub_k1, sub_k2, g_final):
    b, seq, _ = x.shape
    x2d = x.reshape(b * seq, D_MODEL)
    aq, akv, bq, bqr, bkv, bwin, gates, ga, gb = _projection(x2d, jnp.arange(seq, dtype=jnp.int32), seq, g_attn,
                                                             w["w_proj"])
    r3 = lambda a: a.reshape(b, seq, a.shape[-1])
    oa = _moba_prompt(r3(aq), r3(akv))
    cmp_tok = _compress_prompt(r3(bkv), w["cw"])
    ob = _nsa_prompt(r3(bq), r3(bqr), r3(gates), r3(bkv), r3(bwin), cmp_tok)
    h = _merge(x2d, oa.reshape(-1, A_WIDTH), ob.reshape(-1, B_WIDTH), ga, gb, w["p_a"], w["p_b"], w["w_o"])
    mlen = mem.shape[1]
    mem_kv = _rms_matmul(mem.reshape(b * mlen, D_MODEL), g_mem, w["w_ckv"]).reshape(b, mlen, 2 * C_WIDTH)
    y = _channel_and_norm(h, b, mem_kv, w, g_cross, g_ffn, sub_k1, sub_k2, g_final)
    win = r3(bwin)[:, seq - min(WINDOW, seq):]
    return (y.reshape(b, seq, D_MODEL), r3(akv).reshape(b, seq, 2, A_HEADS, HEAD_DIM),
            r3(bkv).reshape(b, seq, 4, B_KV_HEADS, HEAD_DIM), win.reshape(b, -1, 2, B_KV_HEADS, HEAD_DIM),
            mem_kv.reshape(b, mlen, 2, C_HEADS, C_HEAD_DIM))


def _pad_rows(a, rows):
    return jnp.pad(a, ((0, 0), (0, rows - a.shape[1]), (0, 0)))


def _sample_group(x, moba_pool, nsa_pool, win_state, mem_kv, page_table, w, g_attn, g_cross, g_ffn, sub_k1, sub_k2,
                  g_final):
    b, t, _ = x.shape
    past = page_table.shape[1] * PAGE_SIZE
    assert t * B_KV_HEADS == SUBLANES and past % (PAGES_PER_STEP * PAGE_SIZE) == 0 and B_HEADS * t <= SAMPLE_ROWS
    x2d = x.reshape(b * t, D_MODEL)
    pos = past + jnp.arange(t, dtype=jnp.int32)
    aq, akv, bq, bqr, bkv, bwin, gates, ga, gb = _projection(x2d, pos, t, g_attn, w["w_proj"])
    r3 = lambda a: a.reshape(b, t, a.shape[-1])

    n_pool = moba_pool.shape[0]
    pool_a = moba_pool.transpose(0, 2, 3, 4, 1).reshape(n_pool, 2 * A_WIDTH, PAGE_SIZE)
    pool_b = nsa_pool.transpose(0, 2, 3, 4, 1).reshape(n_pool, 4 * B_KV_WIDTH, PAGE_SIZE)

    n_blocks = past // A_BLOCK
    kmean_t = _moba_kmean(pool_a, page_table, -(-n_blocks // LANES) * LANES)
    qa = aq.reshape(b, t, A_HEADS, HEAD_DIM).transpose(0, 2, 1, 3)
    qa_rows = (qa[:, :, :, None, :] * jnp.eye(A_HEADS, dtype=F32)[None, :, None, :, None]).reshape(b, A_HEADS * t, A_WIDTH)
    oa_rows = _moba_sample(pool_a, page_table, kmean_t, _pad_rows(qa_rows, SAMPLE_ROWS),
                           _pad_rows(r3(akv), SUBLANES), past, t)
    oa = oa_rows.reshape(b, A_HEADS, t, A_WIDTH).sum(axis=1)

    cmp_tok = _compress_sample(pool_b, page_table, w["cw"], past)
    eye_g = jnp.eye(B_KV_HEADS, dtype=F32)[None, None, :, None, :, None]

    def group_rows(a):
        a = a.reshape(b, t, B_KV_HEADS, B_GROUP, HEAD_DIM).transpose(0, 3, 2, 1, 4)
        return _pad_rows((a[:, :, :, :, None, :] * eye_g).reshape(b, B_HEADS * t, LANES), SAMPLE_ROWS)

    gate_rows = gates[:, :N_GATES].reshape(b, t, B_KV_HEADS, B_GROUP, 3).transpose(0, 3, 2, 1, 4)
    gate_rows = jnp.pad(gate_rows.reshape(b, B_HEADS * t, 3), ((0, 0), (0, SAMPLE_ROWS - B_HEADS * t), (0, LANES - 3)))
    win_rows = win_state.reshape(b, win_state.shape[1], 2 * B_KV_WIDTH)
    ob_rows = _nsa_sample(pool_b, page_table, cmp_tok, group_rows(bq), group_rows(bqr), gate_rows,
                          _pad_rows(r3(bkv)[:, :, 2 * B_KV_WIDTH:], SUBLANES), win_rows,
                          _pad_rows(r3(bwin), SUBLANES), past, t)
    ob = ob_rows.reshape(b, B_GROUP, B_KV_HEADS, t, B_KV_HEADS, HEAD_DIM).sum(axis=4)
    ob = ob.transpose(0, 3, 2, 1, 4).reshape(b * t, B_WIDTH)

    h = _merge(x2d, oa.reshape(-1, A_WIDTH), ob, ga, gb, w["p_a"], w["p_b"], w["w_o"])
    mem_rows = mem_kv.reshape(b, mem_kv.shape[1], 2 * C_WIDTH)
    y = _channel_and_norm(h, b, mem_rows, w, g_cross, g_ffn, sub_k1, sub_k2, g_final)
    win = jnp.concatenate([win_rows, r3(bwin)], axis=1)
    win = win[:, win.shape[1] - min(WINDOW, win.shape[1]):]
    return (y.reshape(b, t, D_MODEL), r3(akv).reshape(b, t, 2, A_HEADS, HEAD_DIM),
            r3(bkv).reshape(b, t, 4, B_KV_HEADS, HEAD_DIM), win.reshape(b, -1, 2, B_KV_HEADS, HEAD_DIM))


def kernel(x_prompt, x_sample, cache_moba_kv, cache_nsa_kv, state_nsa_win, cache_mem_kv, page_table, mem_prompt, g_attn, w_in, pe_cmp, w_ck1, w_ck2, w_cv1, w_cv2, p_a, p_b, w_o, g_cross, g_mem, w_cq, w_ckv, w_co, g_ffn, w_pq, sub_k1, sub_k2, peer_u, peer_v, g_final):
    assert g_attn.shape[0] == 1, "the final norm is fused into the last PEER step of a single layer"
    w = _layer_weights(w_in[0], pe_cmp[0], w_ck1[0], w_ck2[0], w_cv1[0], w_cv2[0], p_a[0], p_b[0], w_o[0], w_cq[0],
                       w_ckv[0], w_co[0], w_pq[0], peer_u[0], peer_v[0])
    y_p, moba_p, nsa_p, win_p, mem_p = _prompt_group(x_prompt, mem_prompt, w, g_attn[0], g_cross[0], g_mem[0],
                                                    g_ffn[0], sub_k1[0], sub_k2[0], g_final)
    y_s, moba_s, nsa_s, win_s = _sample_group(x_sample, cache_moba_kv[0], cache_nsa_kv[0], state_nsa_win[0],
                                              cache_mem_kv[0], page_table, w, g_attn[0], g_cross[0], g_ffn[0],
                                              sub_k1[0], sub_k2[0], g_final)
    return (y_p, y_s, moba_p[None], moba_s[None], nsa_p[None], nsa_s[None], win_p[None], win_s[None], mem_p[None])
```

```python
import functools

import numpy as np
import jax
import jax.numpy as jnp
from jax import lax
from jax.experimental import pallas as pl
from jax.experimental.pallas import tpu as pltpu

F32 = jnp.float32
BF16 = jnp.bfloat16
HIGHEST = lax.Precision.HIGHEST

LANES = 128
SUBLANES = 8
VMEM_LIMIT_BYTES = 56 * 1024 * 1024

D_MODEL = 1024
HEAD_DIM = 64
ROPE_THETA = 10000.0
NORM_EPS = 1e-6
NEG_BIG = -1e30
PAGE_SIZE = 128

A_HEADS = 8
A_BLOCK = 256
A_TOPK = 3
A_WIDTH = A_HEADS * HEAD_DIM

B_HEADS = 8
B_KV_HEADS = 2
B_GROUP = B_HEADS // B_KV_HEADS
B_WIDTH = B_HEADS * HEAD_DIM
B_KV_WIDTH = B_KV_HEADS * HEAD_DIM
CMP_LEN = 32
CMP_STRIDE = 16
CMP_HIDDEN = 64
SEL_BLOCK = 64
SEL_TOPN = 16
WINDOW = 512
FORCE_SCORE = 1e4

C_HEADS = 4
C_HEAD_DIM = 128
C_WIDTH = C_HEADS * C_HEAD_DIM

P_HEADS = 8
P_NKEYS = 128
P_QDIM = 256
P_TOPK = 16
P_PICKS = P_HEADS * P_TOPK

Q_BLOCK = 256
N_MAIN = 3 * A_WIDTH + B_WIDTH + 6 * B_KV_WIDTH
N_GATES = 3 * B_HEADS
N_PROJ = N_MAIN + LANES + 2 * D_MODEL


def _params(semantics):
    return pltpu.CompilerParams(dimension_semantics=semantics, vmem_limit_bytes=VMEM_LIMIT_BYTES)


def _lane_iota(shape, dtype=jnp.int32):
    return lax.broadcasted_iota(dtype, shape, len(shape) - 1)


def _row_iota(shape, dtype=jnp.int32):
    return lax.broadcasted_iota(dtype, shape, len(shape) - 2)


def _rms(x, g):
    return x * lax.rsqrt(jnp.mean(x * x, axis=-1, keepdims=True) + NORM_EPS) * g


def _sigmoid(x):
    return 1.0 / (1.0 + jnp.exp(-x))


def _dot_t(a, b, precision=None):
    return lax.dot_general(a, b, (((1,), (1,)), ((), ())), precision=precision, preferred_element_type=F32)


def _topk_mask(x, k, lane_f):
    sel = jnp.zeros(x.shape, F32)
    for _ in range(k):
        mx = jnp.max(x, axis=-1, keepdims=True)
        first = jnp.min(jnp.where(x == mx, lane_f, float(x.shape[-1])), axis=-1, keepdims=True)
        hit = lane_f == first
        sel = jnp.where(hit, 1.0, sel)
        x = jnp.where(hit, -jnp.inf, x)
    return sel


def _topk_mask_t(x, k, ids_f):
    sel = jnp.zeros(x.shape, F32)
    for _ in range(k):
        mx = jnp.max(x, axis=0, keepdims=True)
        first = jnp.min(jnp.where(x == mx, ids_f, float(2 ** 20)), axis=0, keepdims=True)
        hit = ids_f == first
        sel = jnp.where(hit, 1.0, sel)
        x = jnp.where(hit, -jnp.inf, x)
    return sel


def _rows_to_lanes(sel_t):
    pad = jnp.zeros((LANES - sel_t.shape[0], sel_t.shape[1]), F32)
    return jnp.transpose(jnp.concatenate([sel_t, pad], axis=0))


def _softmax_rows(s, mask):
    s = jnp.where(mask, s, NEG_BIG)
    m = jnp.max(s, axis=-1, keepdims=True)
    p = jnp.where(mask, jnp.exp(s - m), 0.0)
    return p, jnp.sum(p, axis=-1, keepdims=True)


def _softmax_biased(s, bias):
    s = s + bias
    p = jnp.exp(s - jnp.max(s, axis=-1, keepdims=True))
    return p, jnp.sum(p, axis=-1, keepdims=True)


def _mask_bias(mask):
    return jnp.where(mask, 0.0, NEG_BIG)


def _safe_inv(l):
    return jnp.where(l > 0.0, 1.0 / jnp.where(l > 0.0, l, 1.0), 0.0)


def _proj_kernel(x_ref, g_ref, w_ref, cs_ref, sn_ref, aq_ref, akv_ref, bq_ref, bqr_ref, bkv_ref, bwin_ref,
                 gates_ref, ga_ref, gb_ref):
    ub = _rms(x_ref[...], g_ref[...]).astype(BF16)
    cs = cs_ref[...]
    sn = sn_ref[...]
    first_half = (_lane_iota((1, LANES)) % HEAD_DIM) < (HEAD_DIM // 2)

    def cols(c0, n):
        return jnp.dot(ub, w_ref[:, c0:c0 + n], preferred_element_type=F32)

    def rot(p):
        swapped = jnp.where(first_half, pltpu.roll(p, LANES - HEAD_DIM // 2, 1), pltpu.roll(p, HEAD_DIM // 2, 1))
        return p * cs + swapped * sn

    def rot_wide(p):
        return jnp.concatenate([rot(p[:, c:c + LANES]) for c in range(0, p.shape[1], LANES)], axis=1)

    aq_ref[...] = rot_wide(cols(0, A_WIDTH))
    akv_ref[:, 0:A_WIDTH] = rot_wide(cols(A_WIDTH, A_WIDTH))
    akv_ref[:, A_WIDTH:2 * A_WIDTH] = cols(2 * A_WIDTH, A_WIDTH)
    bq = cols(3 * A_WIDTH, B_WIDTH)
    bq_ref[...] = bq
    bqr_ref[...] = rot_wide(bq)
    c0 = 3 * A_WIDTH + B_WIDTH
    bkv = cols(c0, 4 * B_KV_WIDTH)
    bkv_ref[:, 0:2 * LANES] = bkv[:, 0:2 * LANES]
    bkv_ref[:, 2 * LANES:3 * LANES] = rot(bkv[:, 2 * LANES:3 * LANES])
    bkv_ref[:, 3 * LANES:4 * LANES] = bkv[:, 3 * LANES:4 * LANES]
    bwin = cols(c0 + 4 * B_KV_WIDTH, 2 * B_KV_WIDTH)
    bwin_ref[:, 0:LANES] = rot(bwin[:, 0:LANES])
    bwin_ref[:, LANES:2 * LANES] = bwin[:, LANES:2 * LANES]
    gates_ref[...] = cols(N_MAIN, LANES)
    ga_ref[...] = cols(N_MAIN + LANES, D_MODEL)
    gb_ref[...] = cols(N_MAIN + LANES + D_MODEL, D_MODEL)


def _rope_tables(pos):
    half = HEAD_DIM // 2
    inv_freq = ROPE_THETA ** (-jnp.arange(half, dtype=F32) / half)
    ang = pos.astype(F32)[:, None] * inv_freq[None, :]
    cos, sin = jnp.cos(ang), jnp.sin(ang)
    reps = LANES // HEAD_DIM
    return jnp.tile(jnp.concatenate([cos, cos], axis=1), (1, reps)), jnp.tile(jnp.concatenate([-sin, sin], axis=1), (1, reps))


def _projection(x2d, pos, seq, g_attn, w_proj):
    n = x2d.shape[0]
    tm = min(Q_BLOCK, n)
    cs, sn = _rope_tables(pos)
    if seq >= tm:
        tab_map = lambda i: (i % (seq // tm), 0)
    else:
        cs, sn = jnp.tile(cs, (tm // seq, 1)), jnp.tile(sn, (tm // seq, 1))
        tab_map = lambda i: (0, 0)
    widths = (A_WIDTH, 2 * A_WIDTH, B_WIDTH, B_WIDTH, 4 * B_KV_WIDTH, 2 * B_KV_WIDTH, LANES, D_MODEL, D_MODEL)
    row = lambda i: (i, 0)
    fixed = lambda i: (0, 0)
    return pl.pallas_call(
        _proj_kernel,
        grid=(n // tm,),
        in_specs=[pl.BlockSpec((tm, D_MODEL), row), pl.BlockSpec((1, D_MODEL), fixed),
                  pl.BlockSpec((D_MODEL, N_PROJ), fixed), pl.BlockSpec((tm, LANES), tab_map),
                  pl.BlockSpec((tm, LANES), tab_map)],
        out_specs=[pl.BlockSpec((tm, w), row) for w in widths],
        out_shape=[jax.ShapeDtypeStruct((n, w), F32) for w in widths],
        compiler_params=_params(("parallel",)),
    )(x2d, g_attn.reshape(1, D_MODEL), w_proj, cs, sn)


def _rms_matmul_kernel(x_ref, g_ref, w_ref, o_ref):
    o_ref[...] = jnp.dot(_rms(x_ref[...], g_ref[...]).astype(BF16), w_ref[...], preferred_element_type=F32)


def _rms_matmul(x2d, g, w_bf16):
    n, d = x2d.shape
    m = w_bf16.shape[1]
    tm = min(Q_BLOCK, n)
    return pl.pallas_call(
        _rms_matmul_kernel,
        grid=(n // tm,),
        in_specs=[pl.BlockSpec((tm, d), lambda i: (i, 0)), pl.BlockSpec((1, d), lambda i: (0, 0)),
                  pl.BlockSpec((d, m), lambda i: (0, 0))],
        out_specs=pl.BlockSpec((tm, m), lambda i: (i, 0)),
        out_shape=jax.ShapeDtypeStruct((n, m), F32),
        compiler_params=_params(("parallel",)),
    )(x2d, g.reshape(1, d), w_bf16)


def _moba_prompt_kernel(q_ref, k_ref, v_ref, o_ref, *, seq):
    nb = seq // A_BLOCK
    k = k_ref[0]
    kb = k.astype(BF16)
    vb = v_ref[0].astype(BF16)
    lane = _lane_iota((1, LANES))
    nb_rows = -(-nb // SUBLANES) * SUBLANES
    kmean = jnp.concatenate(
        [jnp.mean(k[j * A_BLOCK:(j + 1) * A_BLOCK], axis=0, keepdims=True) for j in range(nb)]
        + [jnp.zeros((nb_rows - nb, LANES), F32)] * (nb_rows > nb), axis=0)
    blk_t = _row_iota((nb_rows, Q_BLOCK))
    expand = jnp.where(_row_iota((LANES, seq)) == _lane_iota((LANES, seq)) // A_BLOCK, 1.0, 0.0).astype(BF16)
    own_bias = _mask_bias(_lane_iota((Q_BLOCK, A_BLOCK)) <= _row_iota((Q_BLOCK, A_BLOCK)))
    scale = HEAD_DIM ** -0.5
    for i in range(nb):
        qi = q_ref[0, i * Q_BLOCK:(i + 1) * Q_BLOCK, :]
        outs = []
        for hh in range(LANES // HEAD_DIM):
            head = (lane // HEAD_DIM) == hh
            qh = jnp.where(head, qi, 0.0)
            if i > A_TOPK:
                gate = _dot_t(kmean[0:nb_rows], qh, HIGHEST)
                gate = jnp.where(blk_t < i, gate, -jnp.inf)
                sel = _rows_to_lanes(jnp.where(blk_t < i, _topk_mask_t(gate, A_TOPK, blk_t.astype(F32)), 0.0))
            s = _dot_t((qh * scale).astype(BF16), kb[0:(i + 1) * A_BLOCK])
            bias = [own_bias]
            if i > A_TOPK:
                chosen = jnp.dot(sel.astype(BF16), expand[:, 0:i * A_BLOCK], preferred_element_type=F32) > 0.5
                bias = [_mask_bias(chosen)] + bias
            elif i > 0:
                bias = [jnp.zeros((Q_BLOCK, i * A_BLOCK), F32)] + bias
            p, l = _softmax_biased(s, jnp.concatenate(bias, axis=1))
            o = jnp.dot(p.astype(BF16), vb[0:(i + 1) * A_BLOCK], preferred_element_type=F32)
            outs.append(o * _safe_inv(l))
        o_ref[0, i * Q_BLOCK:(i + 1) * Q_BLOCK, :] = jnp.where((lane // HEAD_DIM) == 0, outs[0], outs[1])


def _moba_prompt(aq, akv):
    b, seq, _ = aq.shape
    hp = A_WIDTH // LANES
    return pl.pallas_call(
        functools.partial(_moba_prompt_kernel, seq=seq),
        grid=(b, hp),
        in_specs=[pl.BlockSpec((1, seq, LANES), lambda i, j: (i, 0, j)),
                  pl.BlockSpec((1, seq, LANES), lambda i, j: (i, 0, j)),
                  pl.BlockSpec((1, seq, LANES), lambda i, j: (i, 0, hp + j))],
        out_specs=pl.BlockSpec((1, seq, LANES), lambda i, j: (i, 0, j)),
        out_shape=jax.ShapeDtypeStruct((b, seq, A_WIDTH), F32),
        compiler_params=_params(("parallel", "parallel")),
    )(aq, akv, akv)


def _compress_rows(xk_ref, xv_ref, n_rows, pe_ref, wa_ref, wb_ref, w2_ref):
    acc_a = jnp.zeros((n_rows, 2 * LANES), F32)
    acc_b = jnp.zeros((n_rows, 2 * LANES), F32)
    for tt in range(CMP_STRIDE):
        rows_tt = pl.ds(tt, n_rows, stride=CMP_STRIDE)
        xt = jnp.concatenate([xk_ref[rows_tt, :], xv_ref[rows_tt, :]], axis=1)
        acc_a += jnp.dot((xt + pe_ref[tt:tt + 1, :]).astype(BF16), wa_ref[tt], preferred_element_type=F32)
        acc_b += jnp.dot((xt + pe_ref[CMP_STRIDE + tt:CMP_STRIDE + tt + 1, :]).astype(BF16), wb_ref[tt],
                         preferred_element_type=F32)
    hidden = acc_a + pltpu.roll(acc_b, n_rows - 1, 0)
    out = jnp.dot(jax.nn.gelu(hidden).astype(BF16), w2_ref[...], preferred_element_type=F32)
    return jnp.where(_row_iota(out.shape) < n_rows - 1, out, 0.0)


def _compress_prompt_kernel(xk_ref, xv_ref, pe_ref, wa_ref, wb_ref, w2_ref, o_ref, *, seq):
    o_ref[0] = _compress_rows(xk_ref.at[0], xv_ref.at[0], seq // CMP_STRIDE, pe_ref, wa_ref, wb_ref, w2_ref)


def _compress_weights(pe_cmp, w_ck1, w_ck2, w_cv1, w_cv2):
    def diag(mk, mv):
        z = jnp.zeros_like(mk)
        rows = [[mk, z, z, z], [z, mk, z, z], [z, z, mv, z], [z, z, z, mv]]
        return jnp.concatenate([jnp.concatenate(r, axis=-1) for r in rows], axis=-2)

    k1 = w_ck1.reshape(CMP_LEN, HEAD_DIM, CMP_HIDDEN)
    v1 = w_cv1.reshape(CMP_LEN, HEAD_DIM, CMP_HIDDEN)
    w1 = diag(k1, v1).astype(BF16)
    return jnp.tile(pe_cmp, (1, 4)), w1[:CMP_STRIDE], w1[CMP_STRIDE:], diag(w_ck2, w_cv2).astype(BF16)


def _compress_prompt(bkv, cw):
    b, seq, _ = bkv.shape
    n_rows = seq // CMP_STRIDE
    pe, wa, wb, w2 = cw
    full = lambda *s: pl.BlockSpec(s, lambda i: (0,) * len(s))
    return pl.pallas_call(
        functools.partial(_compress_prompt_kernel, seq=seq),
        grid=(b,),
        in_specs=[pl.BlockSpec((1, seq, LANES), lambda i: (i, 0, 0)), pl.BlockSpec((1, seq, LANES), lambda i: (i, 0, 1)),
                  full(CMP_LEN, 2 * LANES), full(CMP_STRIDE, 2 * LANES, 2 * LANES),
                  full(CMP_STRIDE, 2 * LANES, 2 * LANES), full(2 * LANES, 2 * LANES)],
        out_specs=pl.BlockSpec((1, n_rows, 2 * LANES), lambda i: (i, 0, 0)),
        out_shape=jax.ShapeDtypeStruct((b, n_rows, 2 * LANES), F32),
        compiler_params=_params(("parallel",)),
    )(bkv, bkv, pe, wa, wb, w2)


def _overlap_matrix(n_cmp, n_sel, rows, cols):
    c_start = np.arange(n_cmp)[:, None] * CMP_STRIDE
    s_start = np.arange(n_sel)[None, :] * SEL_BLOCK
    ov = np.clip(np.minimum(c_start + CMP_LEN, s_start + SEL_BLOCK) - np.maximum(c_start, s_start), 0, None)
    out = np.zeros((rows, cols), np.float32)
    out[:n_cmp, :n_sel] = ov / CMP_STRIDE
    return jnp.asarray(out)


def _group_halves(x, g):
    xg = jnp.where((_lane_iota((1, LANES)) // HEAD_DIM) == g, x, 0.0)
    return xg + pltpu.roll(xg, HEAD_DIM, 1)


def _gate_column(gates, head, branch):
    lane = _lane_iota((1, LANES))
    return _sigmoid(jnp.sum(jnp.where(lane == head * 3 + branch, gates, 0.0), axis=-1, keepdims=True))


def _nsa_select_kernel(q_ref, gates_ref, cmp_ref, ovt_ref, ocmp_ref, sel_ref, *, seq):
    g = pl.program_id(1)
    i = pl.program_id(2)
    n_cmp = (seq - CMP_LEN) // CMP_STRIDE + 1
    rows = B_GROUP * Q_BLOCK
    scale = HEAD_DIM ** -0.5
    lane = _lane_iota((1, LANES))
    both_halves = lambda x: _group_halves(x, g)

    parts = []
    for j in range(B_GROUP):
        x = q_ref[0, :, (j // 2) * LANES:(j // 2 + 1) * LANES]
        parts.append(jnp.where((lane // HEAD_DIM) == (j % 2), x, 0.0) * scale)
    q4 = jnp.concatenate(parts, axis=0).astype(BF16)
    qpos = i * Q_BLOCK + _row_iota((Q_BLOCK, 1))

    kc = both_halves(cmp_ref[0, :, 0:LANES]).astype(BF16)
    vc = both_halves(cmp_ref[0, :, LANES:2 * LANES]).astype(BF16)
    n_tok = kc.shape[0]
    tok = _lane_iota((1, n_tok))
    cmask = (tok < n_cmp) & (tok * CMP_STRIDE + (CMP_LEN - 1) <= qpos)
    s = _dot_t(q4, kc).reshape(B_GROUP, Q_BLOCK, n_tok)
    p, l = _softmax_rows(s, cmask[None])
    p_cmp = p * _safe_inv(l)
    o_cmp = jnp.dot(p_cmp.reshape(rows, n_tok).astype(BF16), vc, preferred_element_type=F32)
    n_sel = -(-seq // SEL_BLOCK)
    sel_rows = -(-n_sel // SUBLANES) * SUBLANES
    imp = _dot_t(ovt_ref[0:sel_rows, :], jnp.sum(p_cmp, axis=0), HIGHEST)
    blk = _row_iota((sel_rows, Q_BLOCK))
    cur = (i * Q_BLOCK + _lane_iota((1, Q_BLOCK))) // SEL_BLOCK
    forced = (blk == 0) | (blk == cur) | (blk == cur - 1)
    score = jnp.where(blk <= cur, jnp.where(forced, FORCE_SCORE, imp), -jnp.inf)
    sel = _topk_mask_t(score, min(SEL_TOPN, n_sel), blk.astype(F32))
    sel_ref[0, 0] = _rows_to_lanes(jnp.where(blk <= cur, sel, 0.0))

    gates = gates_ref[0]
    outs = [_gate_column(gates, g * B_GROUP + j, 0) * o_cmp[j * Q_BLOCK:(j + 1) * Q_BLOCK] for j in range(B_GROUP)]
    left = (lane // HEAD_DIM) == 0
    ocmp_ref[0] = jnp.concatenate([jnp.where(left, outs[0], outs[1]), jnp.where(left, outs[2], outs[3])], axis=1)


def _nsa_attend_kernel(qr_ref, ksel_ref, vsel_ref, kwin_ref, vwin_ref, sel_ref, gates_ref, ocmp_ref, o_ref, *, seq):
    chunk = pl.program_id(1)
    g = chunk // (B_GROUP // 2)
    nq = seq // Q_BLOCK
    scale = HEAD_DIM ** -0.5
    lane = _lane_iota((1, LANES))
    ks = _group_halves(ksel_ref[0], g).astype(BF16)
    vs = _group_halves(vsel_ref[0], g).astype(BF16)
    kw = _group_halves(kwin_ref[0], g).astype(BF16)
    vw = _group_halves(vwin_ref[0], g).astype(BF16)
    expand = jnp.where(_row_iota((LANES, seq)) == _lane_iota((LANES, seq)) // SEL_BLOCK, 1.0, 0.0).astype(BF16)
    for i in range(nq):
        rows = slice(i * Q_BLOCK, (i + 1) * Q_BLOCK)
        n_keys = (i + 1) * Q_BLOCK
        w0 = max(i * Q_BLOCK - WINDOW, 0)
        qpos = i * Q_BLOCK + _row_iota((Q_BLOCK, 1))
        chosen = jnp.dot(sel_ref[0, 0, rows, :].astype(BF16), expand[:, 0:n_keys], preferred_element_type=F32) > 0.5
        sbias = _mask_bias(chosen & (_lane_iota((1, n_keys)) <= qpos))
        wpos = w0 + _lane_iota((1, n_keys - w0))
        wbias = _mask_bias((wpos <= qpos) & (wpos > qpos - WINDOW))
        qi = qr_ref[0, rows, :]
        gates = gates_ref[0, rows, :]
        outs = []
        for hh in range(LANES // HEAD_DIM):
            qh = (jnp.where((lane // HEAD_DIM) == hh, qi, 0.0) * scale).astype(BF16)
            p, l = _softmax_biased(_dot_t(qh, ks[0:n_keys]), sbias)
            o_sel = jnp.dot(p.astype(BF16), vs[0:n_keys], preferred_element_type=F32) * _safe_inv(l)
            p, l = _softmax_biased(_dot_t(qh, kw[w0:n_keys]), wbias)
            o_win = jnp.dot(p.astype(BF16), vw[w0:n_keys], preferred_element_type=F32) * _safe_inv(l)
            head = chunk * (LANES // HEAD_DIM) + hh
            outs.append(_gate_column(gates, head, 1) * o_sel + _gate_column(gates, head, 2) * o_win)
        o_ref[0, rows, :] = ocmp_ref[0, rows, :] + jnp.where((lane // HEAD_DIM) == 0, outs[0], outs[1])


def _nsa_prompt(bq, bqr, gates, bkv, bwin, cmp_tok):
    b, seq, _ = bq.shape
    nq = seq // Q_BLOCK
    n_tok = cmp_tok.shape[1]
    n_cmp = (seq - CMP_LEN) // CMP_STRIDE + 1
    n_sel = -(-seq // SEL_BLOCK)
    assert n_sel <= LANES
    overlap_t = _overlap_matrix(n_cmp, n_sel, n_tok, LANES).T
    ocmp, sel = pl.pallas_call(
        functools.partial(_nsa_select_kernel, seq=seq),
        grid=(b, B_KV_HEADS, nq),
        in_specs=[pl.BlockSpec((1, Q_BLOCK, 2 * LANES), lambda i, g, t: (i, t, g)),
                  pl.BlockSpec((1, Q_BLOCK, LANES), lambda i, g, t: (i, t, 0)),
                  pl.BlockSpec((1, n_tok, 2 * LANES), lambda i, g, t: (i, 0, 0)),
                  pl.BlockSpec((LANES, n_tok), lambda i, g, t: (0, 0))],
        out_specs=[pl.BlockSpec((1, Q_BLOCK, 2 * LANES), lambda i, g, t: (i, t, g)),
                   pl.BlockSpec((1, 1, Q_BLOCK, LANES), lambda i, g, t: (i, g, t, 0))],
        out_shape=[jax.ShapeDtypeStruct((b, seq, B_WIDTH), F32),
                   jax.ShapeDtypeStruct((b, B_KV_HEADS, seq, LANES), F32)],
        compiler_params=_params(("parallel", "parallel", "parallel")),
    )(bq, gates, cmp_tok, overlap_t)
    chunks = B_WIDTH // LANES
    per_group = chunks // B_KV_HEADS
    own = pl.BlockSpec((1, seq, LANES), lambda i, c: (i, 0, c))
    col = lambda k: pl.BlockSpec((1, seq, LANES), lambda i, c: (i, 0, k))
    return pl.pallas_call(
        functools.partial(_nsa_attend_kernel, seq=seq),
        grid=(b, chunks),
        in_specs=[own, col(2), col(3), col(0), col(1),
                  pl.BlockSpec((1, 1, seq, LANES), lambda i, c: (i, c // per_group, 0, 0)),
                  pl.BlockSpec((1, seq, LANES), lambda i, c: (i, 0, 0)), own],
        out_specs=own,
        out_shape=jax.ShapeDtypeStruct((b, seq, B_WIDTH), F32),
        compiler_params=_params(("parallel", "parallel")),
    )(bqr, bkv, bkv, bwin, bwin, sel, gates, ocmp)


def _merge_kernel(x_ref, oa_ref, ob_ref, ga_ref, gb_ref, pa_ref, pb_ref, wo_ref, h_ref):
    ya = jnp.dot(oa_ref[...].astype(BF16), pa_ref[...], preferred_element_type=F32)
    yb = jnp.dot(ob_ref[...].astype(BF16), pb_ref[...], preferred_element_type=F32)
    mixed = _sigmoid(ga_ref[...]) * ya + _sigmoid(gb_ref[...]) * yb
    h_ref[...] = x_ref[...] + jnp.dot(mixed.astype(BF16), wo_ref[...], preferred_element_type=F32)


def _merge(x2d, oa, ob, ga, gb, pa, pb, wo):
    n = x2d.shape[0]
    tm = min(Q_BLOCK, n)
    row = lambda w: pl.BlockSpec((tm, w), lambda i: (i, 0))
    full = lambda a: pl.BlockSpec(a.shape, lambda i: (0, 0))
    return pl.pallas_call(
        _merge_kernel,
        grid=(n // tm,),
        in_specs=[row(D_MODEL), row(A_WIDTH), row(B_WIDTH), row(D_MODEL), row(D_MODEL), full(pa), full(pb), full(wo)],
        out_specs=row(D_MODEL),
        out_shape=jax.ShapeDtypeStruct((n, D_MODEL), F32),
        compiler_params=_params(("parallel",)),
    )(x2d, oa, ob, ga, gb, pa, pb, wo)


def _cross_kernel(h_ref, mem_ref, g_ref, wq_ref, wo_ref, o_ref):
    h = h_ref[0]
    q = jnp.dot(_rms(h, g_ref[...]).astype(BF16), wq_ref[...], preferred_element_type=F32)
    scale = C_HEAD_DIM ** -0.5
    outs = []
    for hd in range(C_HEADS):
        c = slice(hd * C_HEAD_DIM, (hd + 1) * C_HEAD_DIM)
        kh = mem_ref[0, :, c].astype(BF16)
        vh = mem_ref[0, :, C_WIDTH + hd * C_HEAD_DIM:C_WIDTH + (hd + 1) * C_HEAD_DIM].astype(BF16)
        s = _dot_t(q[:, c].astype(BF16), kh) * scale
        m = jnp.max(s, axis=-1, keepdims=True)
        p = jnp.exp(s - m)
        o = jnp.dot(p.astype(BF16), vh, preferred_element_type=F32)
        outs.append(o / jnp.sum(p, axis=-1, keepdims=True))
    att = jnp.concatenate(outs, axis=1).astype(BF16)
    o_ref[0] = h + jnp.dot(att, wo_ref[...], preferred_element_type=F32)


def _cross(h3, mem_kv, g_cross, wq, wo):
    b, t, _ = h3.shape
    tq = min(Q_BLOCK, t)
    mlen = mem_kv.shape[1]
    return pl.pallas_call(
        _cross_kernel,
        grid=(b, t // tq),
        in_specs=[pl.BlockSpec((1, tq, D_MODEL), lambda i, j: (i, j, 0)),
                  pl.BlockSpec((1, mlen, 2 * C_WIDTH), lambda i, j: (i, 0, 0)),
                  pl.BlockSpec((1, D_MODEL), lambda i, j: (0, 0)),
                  pl.BlockSpec(wq.shape, lambda i, j: (0, 0)), pl.BlockSpec(wo.shape, lambda i, j: (0, 0))],
        out_specs=pl.BlockSpec((1, tq, D_MODEL), lambda i, j: (i, j, 0)),
        out_shape=jax.ShapeDtypeStruct((b, t, D_MODEL), F32),
        compiler_params=_params(("parallel", "parallel")),
    )(h3, mem_kv, g_cross.reshape(1, D_MODEL), wq, wo)


PEER_SELECT_ROWS = 512


def _peer_candidate_ids():
    ids = -np.ones((7 * SUBLANES,), np.float32)
    layout = [(0, 0, 16), (1, 16, 8), (2, 24, 5), (3, 32, 4), (4, 36, 3), (5, 40, 2), (6, 42, 2), (7, 44, 2)]
    layout += [(a, 40 + a, 1) for a in range(8, 16)]
    for a, row, n in layout:
        assert (a + 1) * n <= P_TOPK < (a + 1) * (n + 1)
        ids[row:row + n] = a * P_TOPK + np.arange(n)
    return jnp.asarray(np.tile(ids[:, None], (1, LANES)))


def _peer_select_kernel(h_ref, g_ref, wq_ref, k1_ref, k2_ref, cid_ref, z_ref, eid_ref, gate_ref, s_ref, e_scr, w_scr,
                        *, tm):
    z = _rms(h_ref[...], g_ref[...])
    z_ref[...] = z
    zb = z.astype(BF16)
    half = P_QDIM // 2
    k1 = k1_ref[...]
    k2 = k2_ref[...]
    for hd in range(P_HEADS):
        q = jnp.dot(zb, wq_ref[:, hd * P_QDIM:(hd + 1) * P_QDIM], preferred_element_type=F32)
        s1 = _dot_t(k1, q[:, 0:half], HIGHEST)
        s2 = _dot_t(k2, q[:, half:P_QDIM], HIGHEST)
        for tb in range(tm // LANES):
            s_ref[hd, tb, 0] = s1[:, tb * LANES:(tb + 1) * LANES]
            s_ref[hd, tb, 1] = s2[:, tb * LANES:(tb + 1) * LANES]

    key_f = _row_iota((P_NKEYS, LANES)).astype(F32)
    cand_valid = cid_ref[...] >= 0.0
    cand_f = jnp.where(cand_valid, cid_ref[...], float(P_TOPK * P_TOPK))
    r8 = _row_iota((SUBLANES, LANES))

    def pair_rows(x1, x2s, combine):
        x1s = jnp.concatenate(x1, axis=0)
        x2_8 = x2s[0:SUBLANES]
        shift = lambda k: pltpu.roll(x2_8, k, 0)
        return jnp.concatenate([
            combine(x1[0], x2s), combine(x1[1], x2_8), combine(x1[2], x2_8),
            jnp.where(r8 < 4, combine(x1[3], x2_8), combine(x1[4], shift(4))),
            jnp.where(r8 < 2, combine(x1[5], x2_8),
                      jnp.where(r8 < 4, combine(x1[6], shift(2)), combine(x1[7], shift(4)))),
            combine(x1s[SUBLANES:2 * SUBLANES], x2s[0:1])], axis=0)

    def take_top(x, ids_f, limit):
        mx = jnp.max(x, axis=0, keepdims=True)
        first = jnp.min(jnp.where(x == mx, ids_f, limit), axis=0, keepdims=True)
        hit = ids_f == first
        return mx, first, hit, jnp.where(hit, -jnp.inf, x)

    def select(x1, x2):
        v1, i1, v2, i2 = [], [], [], []
        for _ in range(P_TOPK):
            m, a, _, x1 = take_top(x1, key_f, float(P_NKEYS))
            v1.append(m)
            i1.append(a)
            m, a, _, x2 = take_top(x2, key_f, float(P_NKEYS))
            v2.append(m)
            i2.append(a)
        v2s = jnp.concatenate(v2, axis=0)
        i2s = jnp.concatenate(i2, axis=0)
        cand = jnp.where(cand_valid, pair_rows(v1, v2s, lambda a, b: a + b), -jnp.inf)
        cid = pair_rows(i1, i2s, lambda a, b: a * float(P_NKEYS) + b)
        es, ss = [], []
        for _ in range(P_TOPK):
            mx, _, hit, cand = take_top(cand, cand_f, float(P_TOPK * P_TOPK))
            es.append(jnp.max(jnp.where(hit, cid, -1.0), axis=0, keepdims=True))
            ss.append(mx)
        ex = jnp.exp(jnp.concatenate(ss, axis=0) - ss[0])
        return jnp.concatenate(es, axis=0), ex / jnp.sum(ex, axis=0, keepdims=True)

    def step(hd, carry):
        rows = pl.ds(pl.multiple_of(hd * P_TOPK, P_TOPK), P_TOPK)
        for tb in range(tm // LANES):
            e16, w16 = select(s_ref[hd, tb, 0], s_ref[hd, tb, 1])
            e_scr[tb, rows, :] = e16
            w_scr[tb, rows, :] = w16
        return carry

    lax.fori_loop(0, P_HEADS, step, 0)
    for tb in range(tm // LANES):
        eid_ref[tb * LANES:(tb + 1) * LANES, :] = jnp.transpose(e_scr[tb]).astype(jnp.int32)
        gate_ref[tb * LANES:(tb + 1) * LANES, :] = jnp.transpose(w_scr[tb])


def _peer_select(h2d, g_ffn, wq, k1, k2):
    n = h2d.shape[0]
    tm = min(PEER_SELECT_ROWS, n)
    assert tm % LANES == 0 and n % tm == 0
    cand_ids = _peer_candidate_ids()
    row = lambda: pl.BlockSpec((tm, D_MODEL), lambda i: (i, 0))
    pick = lambda: pl.BlockSpec((tm, LANES), lambda i: (i, 0))
    full = lambda a: pl.BlockSpec(a.shape, lambda i: (0, 0))
    return pl.pallas_call(
        functools.partial(_peer_select_kernel, tm=tm),
        grid=(n // tm,),
        in_specs=[row(), pl.BlockSpec((1, D_MODEL), lambda i: (0, 0)), full(wq), full(k1), full(k2), full(cand_ids)],
        out_specs=[row(), pick(), pick()],
        out_shape=[jax.ShapeDtypeStruct((n, D_MODEL), F32), jax.ShapeDtypeStruct((n, LANES), jnp.int32),
                   jax.ShapeDtypeStruct((n, LANES), F32)],
        scratch_shapes=[pltpu.VMEM((P_HEADS, tm // LANES, 2, P_NKEYS, LANES), F32),
                        pltpu.VMEM((tm // LANES, P_PICKS, LANES), F32), pltpu.VMEM((tm // LANES, P_PICKS, LANES), F32)],
        compiler_params=_params(("parallel",)),
    )(h2d, g_ffn.reshape(1, D_MODEL), wq, k1, k2, cand_ids)


PEER_TOKENS = 8
ROW_TILES = D_MODEL // LANES


def _peer_gather_kernel(eid_ref, h_ref, z_ref, gate_ref, gfin_ref, uv_hbm, y_ref, buf, sem):
    j = pl.program_id(0)
    n_tiles = pl.num_programs(0) - 1
    rows = PEER_TOKENS * P_PICKS

    for to_slot in range(2):
        @pl.when((j < n_tiles) & (j % 2 == to_slot))
        def _(to_slot=to_slot):
            for r in range(rows):
                e = eid_ref[r // P_PICKS, r % P_PICKS]
                pltpu.make_async_copy(uv_hbm.at[e], buf.at[to_slot, pl.ds(r, 1)], sem.at[to_slot]).start()

    @pl.when(j > 0)
    def _():
        slot = (j - 1) % 2
        pltpu.make_async_copy(uv_hbm.at[pl.ds(0, rows), 0], buf.at[slot], sem.at[slot]).wait()
        gate_t = jnp.transpose(jnp.concatenate(
            [gate_ref[...], jnp.zeros((LANES - PEER_TOKENS, P_PICKS), F32)], axis=0))
        for p in range(PEER_TOKENS):
            picks = pl.ds(p * P_PICKS, P_PICKS)
            acc = jnp.zeros((P_PICKS, LANES), F32)
            for s in range(ROW_TILES):
                acc += buf[slot, picks, s * LANES:(s + 1) * LANES] * z_ref[p:p + 1, s * LANES:(s + 1) * LANES]
            act = jax.nn.gelu(jnp.sum(acc, axis=-1, keepdims=True))
            coef = jnp.broadcast_to(gate_t[:, p:p + 1] * act, (P_PICKS, LANES))
            outs = []
            for s in range(ROW_TILES):
                v_s = buf[slot, picks, D_MODEL + s * LANES:D_MODEL + (s + 1) * LANES]
                outs.append(jnp.sum(coef * v_s, axis=0, keepdims=True))
            y_ref[p:p + 1, :] = h_ref[p:p + 1, :] + jnp.concatenate(outs, axis=1)
        y_ref[...] = _rms(y_ref[...], gfin_ref[...])


def _peer_gather(eid, h2d, z, gate, g_final, uv):
    n = h2d.shape[0]
    tiles = n // PEER_TOKENS
    ahead = lambda j: (jnp.minimum(j, tiles - 1), 0)
    behind = lambda j: (jnp.maximum(j - 1, 0), 0)
    row = lambda w: pl.BlockSpec((PEER_TOKENS, w), behind)
    return pl.pallas_call(
        _peer_gather_kernel,
        grid=(tiles + 1,),
        in_specs=[pl.BlockSpec((PEER_TOKENS, P_PICKS), ahead, memory_space=pltpu.SMEM),
                  row(D_MODEL), row(D_MODEL), row(P_PICKS), pl.BlockSpec((1, D_MODEL), lambda j: (0, 0)),
                  pl.BlockSpec(memory_space=pl.ANY)],
        out_specs=row(D_MODEL),
        out_shape=jax.ShapeDtypeStruct((n, D_MODEL), F32),
        scratch_shapes=[pltpu.VMEM((2, PEER_TOKENS * P_PICKS, 2 * D_MODEL), F32),
                        pltpu.SemaphoreType.DMA((2,))],
        compiler_params=_params(("arbitrary",)),
    )(eid, h2d, z, gate, g_final.reshape(1, D_MODEL), uv)


PAGES_PER_STEP = 16
SAMPLE_ROWS = LANES


def _page_specs(rows, row_block):
    def spec(k):
        return pl.BlockSpec((1, rows, PAGE_SIZE), lambda b, c, pt: (pt[b, c * PAGES_PER_STEP + k], row_block, 0))
    return [spec(k) for k in range(PAGES_PER_STEP)]


def _online_step(s, mask, pv_fn, m_ref, l_ref, acc_ref):
    s = jnp.where(mask, s, NEG_BIG)
    m_old = m_ref[...]
    m_new = jnp.maximum(m_old, jnp.max(s, axis=-1, keepdims=True))
    p = jnp.where(mask, jnp.exp(s - m_new), 0.0)
    alpha = jnp.exp(m_old - m_new)
    l_ref[...] = alpha * l_ref[...] + jnp.sum(p, axis=-1, keepdims=True)
    acc_ref[...] = alpha * acc_ref[...] + pv_fn(p.astype(BF16))
    m_ref[...] = m_new


def _online_update(s, mask, v, m_ref, l_ref, acc_ref):
    _online_step(s, mask, lambda p: jnp.dot(p, v, preferred_element_type=F32), m_ref, l_ref, acc_ref)


def _online_update_t(s, mask, v_t, m_ref, l_ref, acc_ref):
    _online_step(s, mask, lambda p: jnp.transpose(_dot_t(v_t, p)), m_ref, l_ref, acc_ref)


def _moba_kmean_kernel(pt_ref, *refs):
    pages, o_ref = refs[:PAGES_PER_STEP], refs[PAGES_PER_STEP]
    c = pl.program_id(1)

    @pl.when(c == 0)
    def _():
        o_ref[...] = jnp.zeros(o_ref.shape, F32)

    n_keys = PAGES_PER_STEP * PAGE_SIZE
    n_lanes = o_ref.shape[2]
    k_t = jnp.concatenate([p[0] for p in pages], axis=1)
    blk = c * (n_keys // A_BLOCK) + _row_iota((n_keys, n_lanes)) // A_BLOCK
    avg = jnp.where(_lane_iota((n_keys, n_lanes)) == blk, 1.0 / A_BLOCK, 0.0).astype(BF16)
    k_hi = k_t.astype(BF16)
    k_lo = (k_t - k_hi.astype(F32)).astype(BF16)
    o_ref[0] += (jnp.dot(k_hi, avg, preferred_element_type=F32) + jnp.dot(k_lo, avg, preferred_element_type=F32))


def _moba_kmean(pool_t, page_table, n_lanes):
    b, n_pages = page_table.shape
    return pl.pallas_call(
        _moba_kmean_kernel,
        grid_spec=pltpu.PrefetchScalarGridSpec(
            num_scalar_prefetch=1, grid=(b, n_pages // PAGES_PER_STEP),
            in_specs=_page_specs(A_WIDTH, 0),
            out_specs=pl.BlockSpec((1, A_WIDTH, n_lanes), lambda i, c, pt: (i, 0, 0))),
        out_shape=jax.ShapeDtypeStruct((b, A_WIDTH, n_lanes), F32),
        compiler_params=_params(("parallel", "arbitrary")),
    )(page_table, *([pool_t] * PAGES_PER_STEP))


def _moba_sample_kernel(pt_ref, *refs, past, t_new):
    pages = refs[:PAGES_PER_STEP]
    kmean_ref, q_ref, new_ref, o_ref, sel_ref, m_ref, l_ref, acc_ref = refs[PAGES_PER_STEP:]
    c = pl.program_id(1)
    rows = A_HEADS * t_new
    n_lanes = sel_ref.shape[1]
    cur = past // A_BLOCK
    q = q_ref[0]
    qb = (q * HEAD_DIM ** -0.5).astype(BF16)

    @pl.when(c == 0)
    def _():
        lane = _lane_iota((1, n_lanes))
        gate = jnp.dot(q, kmean_ref[0], precision=HIGHEST, preferred_element_type=F32)
        gate = jnp.where(lane < cur, gate, -jnp.inf)
        sel = _topk_mask(gate, min(A_TOPK, cur + 1), lane.astype(F32))
        sel_ref[...] = jnp.where(lane < cur, sel, 0.0)
        m_ref[...] = jnp.full(m_ref.shape, NEG_BIG, F32)
        l_ref[...] = jnp.zeros(l_ref.shape, F32)
        acc_ref[...] = jnp.zeros(acc_ref.shape, F32)

    n_keys = PAGES_PER_STEP * PAGE_SIZE
    k_t = jnp.concatenate([p[0, 0:A_WIDTH, :] for p in pages], axis=1).astype(BF16)
    v_t = jnp.concatenate([p[0, A_WIDTH:2 * A_WIDTH, :] for p in pages], axis=1).astype(BF16)
    blk = c * (n_keys // A_BLOCK) + _lane_iota((n_lanes, n_keys)) // A_BLOCK
    expand = jnp.where(_row_iota((n_lanes, n_keys)) == blk, 1.0, 0.0).astype(BF16)
    chosen = jnp.dot(sel_ref[...].astype(BF16), expand, preferred_element_type=F32) > 0.5
    _online_update_t(jnp.dot(qb, k_t, preferred_element_type=F32), chosen, v_t, m_ref, l_ref, acc_ref)

    @pl.when(c == pl.num_programs(1) - 1)
    def _():
        kn = new_ref[0, :, 0:A_WIDTH].astype(BF16)
        vn = new_ref[0, :, A_WIDTH:2 * A_WIDTH].astype(BF16)
        t_key = _lane_iota((SAMPLE_ROWS, kn.shape[0]))
        t_row = _row_iota((SAMPLE_ROWS, kn.shape[0])) % t_new
        _online_update(_dot_t(qb, kn), (t_key <= t_row) & (t_key < t_new), vn, m_ref, l_ref, acc_ref)
        own = (_lane_iota((SAMPLE_ROWS, A_WIDTH)) // HEAD_DIM) == (_row_iota((SAMPLE_ROWS, A_WIDTH)) // t_new)
        o_ref[0] = jnp.where(own, acc_ref[...] * _safe_inv(l_ref[...]), 0.0)[0:rows]


def _moba_sample(pool_t, page_table, kmean_t, q_rows, new_kv, past, t_new):
    b, n_pages = page_table.shape
    rows = A_HEADS * t_new
    n_lanes = kmean_t.shape[2]
    per_b = lambda s: pl.BlockSpec((1,) + s, lambda i, c, pt: (i, 0, 0))
    return pl.pallas_call(
        functools.partial(_moba_sample_kernel, past=past, t_new=t_new),
        grid_spec=pltpu.PrefetchScalarGridSpec(
            num_scalar_prefetch=1, grid=(b, n_pages // PAGES_PER_STEP),
            in_specs=_page_specs(2 * A_WIDTH, 0) + [per_b(kmean_t.shape[1:]), per_b(q_rows.shape[1:]),
                                                     per_b(new_kv.shape[1:])],
            out_specs=per_b((rows, A_WIDTH)),
            scratch_shapes=[pltpu.VMEM((SAMPLE_ROWS, n_lanes), F32), pltpu.VMEM((SAMPLE_ROWS, 1), F32),
                            pltpu.VMEM((SAMPLE_ROWS, 1), F32), pltpu.VMEM((SAMPLE_ROWS, A_WIDTH), F32)]),
        out_shape=jax.ShapeDtypeStruct((b, rows, A_WIDTH), F32),
        compiler_params=_params(("parallel", "arbitrary")),
    )(page_table, *([pool_t] * PAGES_PER_STEP), kmean_t, q_rows, new_kv)


def _compress_sample_kernel(pt_ref, *refs, past):
    pages = refs[:PAGES_PER_STEP]
    pe_ref, wa_ref, wb_ref, w2_ref, o_ref, xk_ref, xv_ref = refs[PAGES_PER_STEP:]
    c = pl.program_id(1)
    for k, page in enumerate(pages):
        start = pl.multiple_of((c * PAGES_PER_STEP + k) * PAGE_SIZE, PAGE_SIZE)
        xk_ref[pl.ds(start, PAGE_SIZE), :] = jnp.transpose(page[0, 0:LANES, :])
        xv_ref[pl.ds(start, PAGE_SIZE), :] = jnp.transpose(page[0, LANES:2 * LANES, :])

    @pl.when(c == pl.num_programs(1) - 1)
    def _():
        o_ref[0] = _compress_rows(xk_ref, xv_ref, past // CMP_STRIDE, pe_ref, wa_ref, wb_ref, w2_ref)


def _compress_sample(pool, page_table, cw, past):
    b, n_pages = page_table.shape
    n_rows = past // CMP_STRIDE
    pe, wa, wb, w2 = cw
    full = lambda a: pl.BlockSpec(a.shape, lambda i, c, pt: (0,) * a.ndim)
    return pl.pallas_call(
        functools.partial(_compress_sample_kernel, past=past),
        grid_spec=pltpu.PrefetchScalarGridSpec(
            num_scalar_prefetch=1, grid=(b, n_pages // PAGES_PER_STEP),
            in_specs=_page_specs(2 * LANES, 0) + [full(pe), full(wa), full(wb), full(w2)],
            out_specs=pl.BlockSpec((1, n_rows, 2 * LANES), lambda i, c, pt: (i, 0, 0)),
            scratch_shapes=[pltpu.VMEM((past, LANES), F32), pltpu.VMEM((past, LANES), F32)]),
        out_shape=jax.ShapeDtypeStruct((b, n_rows, 2 * LANES), F32),
        compiler_params=_params(("parallel", "arbitrary")),
    )(page_table, *([pool] * PAGES_PER_STEP), pe, wa, wb, w2)


def _nsa_sample_kernel(pt_ref, *refs, past, t_new):
    pages = refs[:PAGES_PER_STEP]
    (cmp_ref, q_ref, qr_ref, gates_ref, newsel_ref, win_ref, newwin_ref, ov_ref, o_ref,
     kv_ref, m_ref, l_ref, acc_ref) = refs[PAGES_PER_STEP:]
    c = pl.program_id(1)
    for k, page in enumerate(pages):
        kv_ref[c * PAGES_PER_STEP + k] = page[0]

    @pl.when(c == pl.num_programs(1) - 1)
    def _():
        rows = B_HEADS * t_new
        slab = B_KV_HEADS * t_new
        scale = HEAD_DIM ** -0.5
        qb = (q_ref[0] * scale).astype(BF16)
        qrb = (qr_ref[0] * scale).astype(BF16)
        pos = past + _row_iota((SAMPLE_ROWS, 1)) % t_new

        n_tok = cmp_ref.shape[1]
        n_cmp = n_tok - 1
        kc = cmp_ref[0, :, 0:LANES].astype(BF16)
        vc = cmp_ref[0, :, LANES:2 * LANES].astype(BF16)
        tok = _lane_iota((1, n_tok))
        p, l = _softmax_rows(_dot_t(qb, kc), (tok < n_cmp) & (tok * CMP_STRIDE + (CMP_LEN - 1) <= pos))
        p_cmp = p * _safe_inv(l)
        o_cmp = jnp.dot(p_cmp.astype(BF16), vc, preferred_element_type=F32)
        p_sum = sum(p_cmp[j * slab:(j + 1) * slab] for j in range(B_GROUP))
        imp = jnp.dot(p_sum, ov_ref[...], precision=HIGHEST, preferred_element_type=F32)
        n_lanes = ov_ref.shape[1]
        lane = _lane_iota((1, n_lanes))
        cur = past // SEL_BLOCK
        forced = (lane == 0) | (lane == cur) | (lane == cur - 1)
        score = jnp.where(lane <= cur, jnp.where(forced, FORCE_SCORE, imp), -jnp.inf)
        sel = _topk_mask(score, min(SEL_TOPN, cur + 1), lane.astype(F32))
        sel = jnp.where(lane <= cur, sel, 0.0)
        sel_b = sel.astype(BF16)

        m_ref[...] = jnp.full(m_ref.shape, NEG_BIG, F32)
        l_ref[...] = jnp.zeros(l_ref.shape, F32)
        acc_ref[...] = jnp.zeros(acc_ref.shape, F32)
        n_keys = PAGES_PER_STEP * PAGE_SIZE
        tile_rows = lambda a: jnp.concatenate([a] * (SAMPLE_ROWS // slab), axis=0)
        for ci in range(past // n_keys):
            pages_ci = range(ci * PAGES_PER_STEP, (ci + 1) * PAGES_PER_STEP)
            ks_t = jnp.concatenate([kv_ref[pg, 0:LANES, :] for pg in pages_ci], axis=1).astype(BF16)
            vs_t = jnp.concatenate([kv_ref[pg, LANES:2 * LANES, :] for pg in pages_ci], axis=1).astype(BF16)
            blk = (ci * n_keys + _lane_iota((n_lanes, n_keys))) // SEL_BLOCK
            expand = jnp.where(_row_iota((n_lanes, n_keys)) == blk, 1.0, 0.0).astype(BF16)
            chosen = jnp.dot(sel_b, expand, preferred_element_type=F32) > 0.5
            _online_update_t(jnp.dot(qrb, ks_t, preferred_element_type=F32), tile_rows(chosen), vs_t,
                             m_ref, l_ref, acc_ref)
        kn = newsel_ref[0, :, 0:LANES].astype(BF16)
        vn = newsel_ref[0, :, LANES:2 * LANES].astype(BF16)
        t_key = _lane_iota((SAMPLE_ROWS, kn.shape[0]))
        cur_chosen = tile_rows(jnp.sum(jnp.where(lane == cur, sel, 0.0), axis=-1, keepdims=True)) > 0.5
        _online_update(_dot_t(qrb, kn), cur_chosen & (past + t_key <= pos) & (t_key < t_new), vn,
                       m_ref, l_ref, acc_ref)
        o_sel = acc_ref[...] * _safe_inv(l_ref[...])

        n_win = win_ref.shape[1]
        kw = jnp.concatenate([win_ref[0, :, 0:LANES], newwin_ref[0, :, 0:LANES]], axis=0).astype(BF16)
        vw = jnp.concatenate([win_ref[0, :, LANES:2 * LANES], newwin_ref[0, :, LANES:2 * LANES]], axis=0).astype(BF16)
        wpos = past - n_win + _lane_iota((1, kw.shape[0]))
        p, l = _softmax_rows(_dot_t(qrb, kw), (wpos <= pos) & (wpos > pos - WINDOW))
        o_win = jnp.dot(p.astype(BF16), vw, preferred_element_type=F32) * _safe_inv(l)

        gt = _sigmoid(gates_ref[0])
        o = gt[:, 0:1] * o_cmp + gt[:, 1:2] * o_sel + gt[:, 2:3] * o_win
        shape = (SAMPLE_ROWS, LANES)
        own = (_lane_iota(shape) // HEAD_DIM) == ((_row_iota(shape) // t_new) % B_KV_HEADS)
        o_ref[0] = jnp.where(own, o, 0.0)[0:rows]


def _nsa_sample(pool_t, page_table, cmp_tok, q_rows, qr_rows, gate_rows, new_sel, win_state, new_win, past, t_new):
    b, n_pages = page_table.shape
    rows = B_HEADS * t_new
    n_tok = cmp_tok.shape[1]
    n_sel = past // SEL_BLOCK + 1
    n_lanes = -(-n_sel // LANES) * LANES
    overlap = _overlap_matrix(n_tok - 1, n_sel, n_tok, n_lanes)
    per_b = lambda a: pl.BlockSpec((1,) + a.shape[1:], lambda i, c, pt: (i, 0, 0))
    return pl.pallas_call(
        functools.partial(_nsa_sample_kernel, past=past, t_new=t_new),
        grid_spec=pltpu.PrefetchScalarGridSpec(
            num_scalar_prefetch=1, grid=(b, n_pages // PAGES_PER_STEP),
            in_specs=_page_specs(2 * LANES, 1) + [per_b(cmp_tok), per_b(q_rows), per_b(qr_rows), per_b(gate_rows),
                                                  per_b(new_sel), per_b(win_state), per_b(new_win),
                                                  pl.BlockSpec(overlap.shape, lambda i, c, pt: (0, 0))],
            out_specs=pl.BlockSpec((1, rows, LANES), lambda i, c, pt: (i, 0, 0)),
            scratch_shapes=[pltpu.VMEM((n_pages, 2 * LANES, PAGE_SIZE), F32), pltpu.VMEM((SAMPLE_ROWS, 1), F32),
                            pltpu.VMEM((SAMPLE_ROWS, 1), F32), pltpu.VMEM((SAMPLE_ROWS, LANES), F32)]),
        out_shape=jax.ShapeDtypeStruct((b, rows, LANES), F32),
        compiler_params=_params(("parallel", "arbitrary")),
    )(page_table, *([pool_t] * PAGES_PER_STEP), cmp_tok, q_rows, qr_rows, gate_rows, new_sel, win_state, new_win,
      overlap)


def _layer_weights(w_in, pe_cmp, w_ck1, w_ck2, w_cv1, w_cv2, p_a, p_b, w_o, w_cq, w_ckv, w_co, w_pq, peer_u, peer_v):
    w_proj = jnp.concatenate(
        [w_in[:, :N_MAIN], jnp.pad(w_in[:, N_MAIN:N_MAIN + N_GATES], ((0, 0), (0, LANES - N_GATES))),
         w_in[:, N_MAIN + N_GATES:]], axis=1).astype(BF16)
    uv = jnp.concatenate([peer_u, peer_v], axis=1)[:, None, :]
    return dict(w_proj=w_proj, cw=_compress_weights(pe_cmp, w_ck1, w_ck2, w_cv1, w_cv2),
                p_a=p_a.astype(BF16), p_b=p_b.astype(BF16), w_o=w_o.astype(BF16), w_cq=w_cq.astype(BF16),
                w_ckv=w_ckv.astype(BF16), w_co=w_co.astype(BF16), w_pq=w_pq.astype(BF16),
                uv=uv)


def _channel_and_norm(h2d, batch, mem_kv, w, g_cross, g_ffn, sub_k1, sub_k2, g_final):
    h3 = _cross(h2d.reshape(batch, -1, D_MODEL), mem_kv, g_cross, w["w_cq"], w["w_co"])
    h2 = h3.reshape(-1, D_MODEL)
    z, eid, gate = _peer_select(h2, g_ffn, w["w_pq"], sub_k1, sub_k2)
    return _peer_gather(eid, h2, z, gate, g_final, w["uv"])


def _prompt_group(x, mem, w, g_attn, g_cross, g_mem, g_ffn, sub_k1, sub_k2, g_final):
    b, seq, _ = x.shape
    x2d = x.reshape(b * seq, D_MODEL)
    aq, akv, bq, bqr, bkv, bwin, gates, ga, gb = _projection(x2d, jnp.arange(seq, dtype=jnp.int32), seq, g_attn,
                                                             w["w_proj"])
    r3 = lambda a: a.reshape(b, seq, a.shape[-1])
    oa = _moba_prompt(r3(aq), r3(akv))
    cmp_tok = _compress_prompt(r3(bkv), w["cw"])
    ob = _nsa_prompt(r3(bq), r3(bqr), r3(gates), r3(bkv), r3(bwin), cmp_tok)
    h = _merge(x2d, oa.reshape(-1, A_WIDTH), ob.reshape(-1, B_WIDTH), ga, gb, w["p_a"], w["p_b"], w["w_o"])
    mlen = mem.shape[1]
    mem_kv = _rms_matmul(mem.reshape(b * mlen, D_MODEL), g_mem, w["w_ckv"]).reshape(b, mlen, 2 * C_WIDTH)
    y = _channel_and_norm(h, b, mem_kv, w, g_cross, g_ffn, sub_k1, sub_k2, g_final)
    win = r3(bwin)[:, seq - min(WINDOW, seq):]
    return (y.reshape(b, seq, D_MODEL), r3(akv).reshape(b, seq, 2, A_HEADS, HEAD_DIM),
            r3(bkv).reshape(b, seq, 4, B_KV_HEADS, HEAD_DIM), win.reshape(b, -1, 2, B_KV_HEADS, HEAD_DIM),
            mem_kv.reshape(b, mlen, 2, C_HEADS, C_HEAD_DIM))


def _pad_rows(a, rows):
    return jnp.pad(a, ((0, 0), (0, rows - a.shape[1]), (0, 0)))


def _sample_group(x, moba_pool, nsa_pool, win_state, mem_kv, page_table, w, g_attn, g_cross, g_ffn, sub_k1, sub_k2,
                  g_final):
    b, t, _ = x.shape
    past = page_table.shape[1] * PAGE_SIZE
    assert t * B_KV_HEADS == SUBLANES and past % (PAGES_PER_STEP * PAGE_SIZE) == 0 and B_HEADS * t <= SAMPLE_ROWS
    x2d = x.reshape(b * t, D_MODEL)
    pos = past + jnp.arange(t, dtype=jnp.int32)
    aq, akv, bq, bqr, bkv, bwin, gates, ga, gb = _projection(x2d, pos, t, g_attn, w["w_proj"])
    r3 = lambda a: a.reshape(b, t, a.shape[-1])

    n_pool = moba_pool.shape[0]
    pool_a = moba_pool.transpose(0, 2, 3, 4, 1).reshape(n_pool, 2 * A_WIDTH, PAGE_SIZE)
    pool_b = nsa_pool.transpose(0, 2, 3, 4, 1).reshape(n_pool, 4 * B_KV_WIDTH, PAGE_SIZE)

    n_blocks = past // A_BLOCK
    kmean_t = _moba_kmean(pool_a, page_table, -(-n_blocks // LANES) * LANES)
    qa = aq.reshape(b, t, A_HEADS, HEAD_DIM).transpose(0, 2, 1, 3)
    qa_rows = (qa[:, :, :, None, :] * jnp.eye(A_HEADS, dtype=F32)[None, :, None, :, None]).reshape(b, A_HEADS * t, A_WIDTH)
    oa_rows = _moba_sample(pool_a, page_table, kmean_t, _pad_rows(qa_rows, SAMPLE_ROWS),
                           _pad_rows(r3(akv), SUBLANES), past, t)
    oa = oa_rows.reshape(b, A_HEADS, t, A_WIDTH).sum(axis=1)

    cmp_tok = _compress_sample(pool_b, page_table, w["cw"], past)
    eye_g = jnp.eye(B_KV_HEADS, dtype=F32)[None, None, :, None, :, None]

    def group_rows(a):
        a = a.reshape(b, t, B_KV_HEADS, B_GROUP, HEAD_DIM).transpose(0, 3, 2, 1, 4)
        return _pad_rows((a[:, :, :, :, None, :] * eye_g).reshape(b, B_HEADS * t, LANES), SAMPLE_ROWS)

    gate_rows = gates[:, :N_GATES].reshape(b, t, B_KV_HEADS, B_GROUP, 3).transpose(0, 3, 2, 1, 4)
    gate_rows = jnp.pad(gate_rows.reshape(b, B_HEADS * t, 3), ((0, 0), (0, SAMPLE_ROWS - B_HEADS * t), (0, LANES - 3)))
    win_rows = win_state.reshape(b, win_state.shape[1], 2 * B_KV_WIDTH)
    ob_rows = _nsa_sample(pool_b, page_table, cmp_tok, group_rows(bq), group_rows(bqr), gate_rows,
                          _pad_rows(r3(bkv)[:, :, 2 * B_KV_WIDTH:], SUBLANES), win_rows,
                          _pad_rows(r3(bwin), SUBLANES), past, t)
    ob = ob_rows.reshape(b, B_GROUP, B_KV_HEADS, t, B_KV_HEADS, HEAD_DIM).sum(axis=4)
    ob = ob.transpose(0, 3, 2, 1, 4).reshape(b * t, B_WIDTH)

    h = _merge(x2d, oa.reshape(-1, A_WIDTH), ob, ga, gb, w["p_a"], w["p_b"], w["w_o"])
    mem_rows = mem_kv.reshape(b, mem_kv.shape[1], 2 * C_WIDTH)
    y = _channel_and_norm(h, b, mem_rows, w, g_cross, g_ffn, sub_k1, sub_k2, g_final)
    win = jnp.concatenate([win_rows, r3(bwin)], axis=1)
    win = win[:, win.shape[1] - min(WINDOW, win.shape[1]):]
    return (y.reshape(b, t, D_MODEL), r3(akv).reshape(b, t, 2, A_HEADS, HEAD_DIM),
            r3(bkv).reshape(b, t, 4, B_KV_HEADS, HEAD_DIM), win.reshape(b, -1, 2, B_KV_HEADS, HEAD_DIM))


def kernel(x_prompt, x_sample, cache_moba_kv, cache_nsa_kv, state_nsa_win, cache_mem_kv, page_table, mem_prompt, g_attn, w_in, pe_cmp, w_ck1, w_ck2, w_cv1, w_cv2, p_a, p_b, w_o, g_cross, g_mem, w_cq, w_ckv, w_co, g_ffn, w_pq, sub_k1, sub_k2, peer_u, peer_v, g_final):
    assert g_attn.shape[0] == 1, "the final norm is fused into the last PEER step of a single layer"
    w = _layer_weights(w_in[0], pe_cmp[0], w_ck1[0], w_ck2[0], w_cv1[0], w_cv2[0], p_a[0], p_b[0], w_o[0], w_cq[0],
                       w_ckv[0], w_co[0], w_pq[0], peer_u[0], peer_v[0])
    y_p, moba_p, nsa_p, win_p, mem_p = _prompt_group(x_prompt, mem_prompt, w, g_attn[0], g_cross[0], g_mem[0],
                                                    g_ffn[0], sub_k1[0], sub_k2[0], g_final)
    y_s, moba_s, nsa_s, win_s = _sample_group(x_sample, cache_moba_kv[0], cache_nsa_kv[0], state_nsa_win[0],
                                              cache_mem_kv[0], page_table, w, g_attn[0], g_cross[0], g_ffn[0],
                                              sub_k1[0], sub_k2[0], g_final)
    return (y_p, y_s, moba_p[None], moba_s[None], nsa_p[None], nsa_s[None], win_p[None], win_s[None], mem_p[None])
```

```python
import functools

import numpy as np
import jax
import jax.numpy as jnp
from jax import lax
from jax.experimental import pallas as pl
from jax.experimental.pallas import tpu as pltpu

F32 = jnp.float32
BF16 = jnp.bfloat16
HIGHEST = lax.Precision.HIGHEST

LANES = 128
SUBLANES = 8
VMEM_LIMIT_BYTES = 56 * 1024 * 1024

D_MODEL = 1024
HEAD_DIM = 64
ROPE_THETA = 10000.0
NORM_EPS = 1e-6
NEG_BIG = -1e30
PAGE_SIZE = 128

A_HEADS = 8
A_BLOCK = 256
A_TOPK = 3
A_WIDTH = A_HEADS * HEAD_DIM

B_HEADS = 8
B_KV_HEADS = 2
B_GROUP = B_HEADS // B_KV_HEADS
B_WIDTH = B_HEADS * HEAD_DIM
B_KV_WIDTH = B_KV_HEADS * HEAD_DIM
CMP_LEN = 32
CMP_STRIDE = 16
CMP_HIDDEN = 64
SEL_BLOCK = 64
SEL_TOPN = 16
WINDOW = 512
FORCE_SCORE = 1e4

C_HEADS = 4
C_HEAD_DIM = 128
C_WIDTH = C_HEADS * C_HEAD_DIM

P_HEADS = 8
P_NKEYS = 128
P_QDIM = 256
P_TOPK = 16
P_PICKS = P_HEADS * P_TOPK

Q_BLOCK = 256
N_MAIN = 3 * A_WIDTH + B_WIDTH + 6 * B_KV_WIDTH
N_GATES = 3 * B_HEADS
N_PROJ = N_MAIN + LANES + 2 * D_MODEL


def _params(semantics):
    return pltpu.CompilerParams(dimension_semantics=semantics, vmem_limit_bytes=VMEM_LIMIT_BYTES)


def _lane_iota(shape, dtype=jnp.int32):
    return lax.broadcasted_iota(dtype, shape, len(shape) - 1)


def _row_iota(shape, dtype=jnp.int32):
    return lax.broadcasted_iota(dtype, shape, len(shape) - 2)


def _rms(x, g):
    return x * lax.rsqrt(jnp.mean(x * x, axis=-1, keepdims=True) + NORM_EPS) * g


def _sigmoid(x):
    return 1.0 / (1.0 + jnp.exp(-x))


def _dot_t(a, b, precision=None):
    return lax.dot_general(a, b, (((1,), (1,)), ((), ())), precision=precision, preferred_element_type=F32)


def _topk_mask(x, k, lane_f):
    sel = jnp.zeros(x.shape, F32)
    for _ in range(k):
        mx = jnp.max(x, axis=-1, keepdims=True)
        first = jnp.min(jnp.where(x == mx, lane_f, float(x.shape[-1])), axis=-1, keepdims=True)
        hit = lane_f == first
        sel = jnp.where(hit, 1.0, sel)
        x = jnp.where(hit, -jnp.inf, x)
    return sel


def _topk_mask_t(x, k, ids_f):
    sel = jnp.zeros(x.shape, F32)
    for _ in range(k):
        mx = jnp.max(x, axis=0, keepdims=True)
        first = jnp.min(jnp.where(x == mx, ids_f, float(2 ** 20)), axis=0, keepdims=True)
        hit = ids_f == first
        sel = jnp.where(hit, 1.0, sel)
        x = jnp.where(hit, -jnp.inf, x)
    return sel


def _rows_to_lanes(sel_t):
    pad = jnp.zeros((LANES - sel_t.shape[0], sel_t.shape[1]), F32)
    return jnp.transpose(jnp.concatenate([sel_t, pad], axis=0))


def _softmax_rows(s, mask):
    s = jnp.where(mask, s, NEG_BIG)
    m = jnp.max(s, axis=-1, keepdims=True)
    p = jnp.where(mask, jnp.exp(s - m), 0.0)
    return p, jnp.sum(p, axis=-1, keepdims=True)


def _softmax_biased(s, bias):
    s = s + bias
    p = jnp.exp(s - jnp.max(s, axis=-1, keepdims=True))
    return p, jnp.sum(p, axis=-1, keepdims=True)


def _mask_bias(mask):
    return jnp.where(mask, 0.0, NEG_BIG)


def _safe_inv(l):
    return jnp.where(l > 0.0, 1.0 / jnp.where(l > 0.0, l, 1.0), 0.0)


def _proj_kernel(x_ref, g_ref, w_ref, cs_ref, sn_ref, aq_ref, akv_ref, bq_ref, bqr_ref, bkv_ref, bwin_ref,
                 gates_ref, ga_ref, gb_ref):
    ub = _rms(x_ref[...], g_ref[...]).astype(BF16)
    cs = cs_ref[...]
    sn = sn_ref[...]
    first_half = (_lane_iota((1, LANES)) % HEAD_DIM) < (HEAD_DIM // 2)

    def cols(c0, n):
        return jnp.dot(ub, w_ref[:, c0:c0 + n], preferred_element_type=F32)

    def rot(p):
        swapped = jnp.where(first_half, pltpu.roll(p, LANES - HEAD_DIM // 2, 1), pltpu.roll(p, HEAD_DIM // 2, 1))
        return p * cs + swapped * sn

    def rot_wide(p):
        return jnp.concatenate([rot(p[:, c:c + LANES]) for c in range(0, p.shape[1], LANES)], axis=1)

    aq_ref[...] = rot_wide(cols(0, A_WIDTH))
    akv_ref[:, 0:A_WIDTH] = rot_wide(cols(A_WIDTH, A_WIDTH))
    akv_ref[:, A_WIDTH:2 * A_WIDTH] = cols(2 * A_WIDTH, A_WIDTH)
    bq = cols(3 * A_WIDTH, B_WIDTH)
    bq_ref[...] = bq
    bqr_ref[...] = rot_wide(bq)
    c0 = 3 * A_WIDTH + B_WIDTH
    bkv = cols(c0, 4 * B_KV_WIDTH)
    bkv_ref[:, 0:2 * LANES] = bkv[:, 0:2 * LANES]
    bkv_ref[:, 2 * LANES:3 * LANES] = rot(bkv[:, 2 * LANES:3 * LANES])
    bkv_ref[:, 3 * LANES:4 * LANES] = bkv[:, 3 * LANES:4 * LANES]
    bwin = cols(c0 + 4 * B_KV_WIDTH, 2 * B_KV_WIDTH)
    bwin_ref[:, 0:LANES] = rot(bwin[:, 0:LANES])
    bwin_ref[:, LANES:2 * LANES] = bwin[:, LANES:2 * LANES]
    gates_ref[...] = cols(N_MAIN, LANES)
    ga_ref[...] = cols(N_MAIN + LANES, D_MODEL)
    gb_ref[...] = cols(N_MAIN + LANES + D_MODEL, D_MODEL)


def _rope_tables(pos):
    half = HEAD_DIM // 2
    inv_freq = ROPE_THETA ** (-jnp.arange(half, dtype=F32) / half)
    ang = pos.astype(F32)[:, None] * inv_freq[None, :]
    cos, sin = jnp.cos(ang), jnp.sin(ang)
    reps = LANES // HEAD_DIM
    return jnp.tile(jnp.concatenate([cos, cos], axis=1), (1, reps)), jnp.tile(jnp.concatenate([-sin, sin], axis=1), (1, reps))


def _projection(x2d, pos, seq, g_attn, w_proj):
    n = x2d.shape[0]
    tm = min(Q_BLOCK, n)
    cs, sn = _rope_tables(pos)
    if seq >= tm:
        tab_map = lambda i: (i % (seq // tm), 0)
    else:
        cs, sn = jnp.tile(cs, (tm // seq, 1)), jnp.tile(sn, (tm // seq, 1))
        tab_map = lambda i: (0, 0)
    widths = (A_WIDTH, 2 * A_WIDTH, B_WIDTH, B_WIDTH, 4 * B_KV_WIDTH, 2 * B_KV_WIDTH, LANES, D_MODEL, D_MODEL)
    row = lambda i: (i, 0)
    fixed = lambda i: (0, 0)
    return pl.pallas_call(
        _proj_kernel,
        grid=(n // tm,),
        in_specs=[pl.BlockSpec((tm, D_MODEL), row), pl.BlockSpec((1, D_MODEL), fixed),
                  pl.BlockSpec((D_MODEL, N_PROJ), fixed), pl.BlockSpec((tm, LANES), tab_map),
                  pl.BlockSpec((tm, LANES), tab_map)],
        out_specs=[pl.BlockSpec((tm, w), row) for w in widths],
        out_shape=[jax.ShapeDtypeStruct((n, w), F32) for w in widths],
        compiler_params=_params(("parallel",)),
    )(x2d, g_attn.reshape(1, D_MODEL), w_proj, cs, sn)


def _rms_matmul_kernel(x_ref, g_ref, w_ref, o_ref):
    o_ref[...] = jnp.dot(_rms(x_ref[...], g_ref[...]).astype(BF16), w_ref[...], preferred_element_type=F32)


def _rms_matmul(x2d, g, w_bf16):
    n, d = x2d.shape
    m = w_bf16.shape[1]
    tm = min(Q_BLOCK, n)
    return pl.pallas_call(
        _rms_matmul_kernel,
        grid=(n // tm,),
        in_specs=[pl.BlockSpec((tm, d), lambda i: (i, 0)), pl.BlockSpec((1, d), lambda i: (0, 0)),
                  pl.BlockSpec((d, m), lambda i: (0, 0))],
        out_specs=pl.BlockSpec((tm, m), lambda i: (i, 0)),
        out_shape=jax.ShapeDtypeStruct((n, m), F32),
        compiler_params=_params(("parallel",)),
    )(x2d, g.reshape(1, d), w_bf16)


def _moba_prompt_kernel(q_ref, k_ref, v_ref, o_ref, *, seq):
    nb = seq // A_BLOCK
    k = k_ref[0]
    kb = k.astype(BF16)
    vb = v_ref[0].astype(BF16)
    lane = _lane_iota((1, LANES))
    nb_rows = -(-nb // SUBLANES) * SUBLANES
    kmean = jnp.concatenate(
        [jnp.mean(k[j * A_BLOCK:(j + 1) * A_BLOCK], axis=0, keepdims=True) for j in range(nb)]
        + [jnp.zeros((nb_rows - nb, LANES), F32)] * (nb_rows > nb), axis=0)
    blk_t = _row_iota((nb_rows, Q_BLOCK))
    expand = jnp.where(_row_iota((LANES, seq)) == _lane_iota((LANES, seq)) // A_BLOCK, 1.0, 0.0).astype(BF16)
    own_bias = _mask_bias(_lane_iota((Q_BLOCK, A_BLOCK)) <= _row_iota((Q_BLOCK, A_BLOCK)))
    scale = HEAD_DIM ** -0.5
    for i in range(nb):
        qi = q_ref[0, i * Q_BLOCK:(i + 1) * Q_BLOCK, :]
        heads = LANES // HEAD_DIM
        n_keys = (i + 1) * A_BLOCK
        qhs, biases = [], []
        for hh in range(heads):
            qh = jnp.where((lane // HEAD_DIM) == hh, qi, 0.0)
            bias = [own_bias]
            if i > A_TOPK:
                gate = _dot_t(kmean[0:nb_rows], qh, HIGHEST)
                gate = jnp.where(blk_t < i, gate, -jnp.inf)
                sel = _rows_to_lanes(jnp.where(blk_t < i, _topk_mask_t(gate, A_TOPK, blk_t.astype(F32)), 0.0))
                chosen = jnp.dot(sel.astype(BF16), expand[:, 0:i * A_BLOCK], preferred_element_type=F32) > 0.5
                bias = [_mask_bias(chosen)] + bias
            elif i > 0:
                bias = [jnp.zeros((Q_BLOCK, i * A_BLOCK), F32)] + bias
            qhs.append(qh * scale)
            biases.append(jnp.concatenate(bias, axis=1))
        s = _dot_t(jnp.concatenate(qhs, axis=0).astype(BF16), kb[0:n_keys]).reshape(heads, Q_BLOCK, n_keys)
        p, l = _softmax_biased(s, jnp.stack(biases))
        o = jnp.dot(p.reshape(heads * Q_BLOCK, n_keys).astype(BF16), vb[0:n_keys], preferred_element_type=F32)
        o = o.reshape(heads, Q_BLOCK, LANES) * _safe_inv(l)
        o_ref[0, i * Q_BLOCK:(i + 1) * Q_BLOCK, :] = jnp.where((lane // HEAD_DIM) == 0, o[0], o[1])


def _moba_prompt(aq, akv):
    b, seq, _ = aq.shape
    hp = A_WIDTH // LANES
    return pl.pallas_call(
        functools.partial(_moba_prompt_kernel, seq=seq),
        grid=(b, hp),
        in_specs=[pl.BlockSpec((1, seq, LANES), lambda i, j: (i, 0, j)),
                  pl.BlockSpec((1, seq, LANES), lambda i, j: (i, 0, j)),
                  pl.BlockSpec((1, seq, LANES), lambda i, j: (i, 0, hp + j))],
        out_specs=pl.BlockSpec((1, seq, LANES), lambda i, j: (i, 0, j)),
        out_shape=jax.ShapeDtypeStruct((b, seq, A_WIDTH), F32),
        compiler_params=_params(("parallel", "parallel")),
    )(aq, akv, akv)


def _compress_rows(xk_ref, xv_ref, n_rows, pe_ref, wa_ref, wb_ref, w2_ref):
    acc_a = jnp.zeros((n_rows, 2 * LANES), F32)
    acc_b = jnp.zeros((n_rows, 2 * LANES), F32)
    for tt in range(CMP_STRIDE):
        rows_tt = pl.ds(tt, n_rows, stride=CMP_STRIDE)
        xt = jnp.concatenate([xk_ref[rows_tt, :], xv_ref[rows_tt, :]], axis=1)
        acc_a += jnp.dot((xt + pe_ref[tt:tt + 1, :]).astype(BF16), wa_ref[tt], preferred_element_type=F32)
        acc_b += jnp.dot((xt + pe_ref[CMP_STRIDE + tt:CMP_STRIDE + tt + 1, :]).astype(BF16), wb_ref[tt],
                         preferred_element_type=F32)
    hidden = acc_a + pltpu.roll(acc_b, n_rows - 1, 0)
    out = jnp.dot(jax.nn.gelu(hidden).astype(BF16), w2_ref[...], preferred_element_type=F32)
    return jnp.where(_row_iota(out.shape) < n_rows - 1, out, 0.0)


def _compress_prompt_kernel(xk_ref, xv_ref, pe_ref, wa_ref, wb_ref, w2_ref, o_ref, *, seq):
    o_ref[0] = _compress_rows(xk_ref.at[0], xv_ref.at[0], seq // CMP_STRIDE, pe_ref, wa_ref, wb_ref, w2_ref)


def _compress_weights(pe_cmp, w_ck1, w_ck2, w_cv1, w_cv2):
    def diag(mk, mv):
        z = jnp.zeros_like(mk)
        rows = [[mk, z, z, z], [z, mk, z, z], [z, z, mv, z], [z, z, z, mv]]
        return jnp.concatenate([jnp.concatenate(r, axis=-1) for r in rows], axis=-2)

    k1 = w_ck1.reshape(CMP_LEN, HEAD_DIM, CMP_HIDDEN)
    v1 = w_cv1.reshape(CMP_LEN, HEAD_DIM, CMP_HIDDEN)
    w1 = diag(k1, v1).astype(BF16)
    return jnp.tile(pe_cmp, (1, 4)), w1[:CMP_STRIDE], w1[CMP_STRIDE:], diag(w_ck2, w_cv2).astype(BF16)


def _compress_prompt(bkv, cw):
    b, seq, _ = bkv.shape
    n_rows = seq // CMP_STRIDE
    pe, wa, wb, w2 = cw
    full = lambda *s: pl.BlockSpec(s, lambda i: (0,) * len(s))
    return pl.pallas_call(
        functools.partial(_compress_prompt_kernel, seq=seq),
        grid=(b,),
        in_specs=[pl.BlockSpec((1, seq, LANES), lambda i: (i, 0, 0)), pl.BlockSpec((1, seq, LANES), lambda i: (i, 0, 1)),
                  full(CMP_LEN, 2 * LANES), full(CMP_STRIDE, 2 * LANES, 2 * LANES),
                  full(CMP_STRIDE, 2 * LANES, 2 * LANES), full(2 * LANES, 2 * LANES)],
        out_specs=pl.BlockSpec((1, n_rows, 2 * LANES), lambda i: (i, 0, 0)),
        out_shape=jax.ShapeDtypeStruct((b, n_rows, 2 * LANES), F32),
        compiler_params=_params(("parallel",)),
    )(bkv, bkv, pe, wa, wb, w2)


def _overlap_matrix(n_cmp, n_sel, rows, cols):
    c_start = np.arange(n_cmp)[:, None] * CMP_STRIDE
    s_start = np.arange(n_sel)[None, :] * SEL_BLOCK
    ov = np.clip(np.minimum(c_start + CMP_LEN, s_start + SEL_BLOCK) - np.maximum(c_start, s_start), 0, None)
    out = np.zeros((rows, cols), np.float32)
    out[:n_cmp, :n_sel] = ov / CMP_STRIDE
    return jnp.asarray(out)


def _group_halves(x, g):
    xg = jnp.where((_lane_iota((1, LANES)) // HEAD_DIM) == g, x, 0.0)
    return xg + pltpu.roll(xg, HEAD_DIM, 1)


def _gate_column(gates, head, branch):
    lane = _lane_iota((1, LANES))
    return _sigmoid(jnp.sum(jnp.where(lane == head * 3 + branch, gates, 0.0), axis=-1, keepdims=True))


def _nsa_select_kernel(q_ref, gates_ref, cmp_ref, ovt_ref, ocmp_ref, sel_ref, *, seq):
    g = pl.program_id(1)
    i = pl.program_id(2)
    n_cmp = (seq - CMP_LEN) // CMP_STRIDE + 1
    rows = B_GROUP * Q_BLOCK
    scale = HEAD_DIM ** -0.5
    lane = _lane_iota((1, LANES))
    both_halves = lambda x: _group_halves(x, g)

    parts = []
    for j in range(B_GROUP):
        x = q_ref[0, :, (j // 2) * LANES:(j // 2 + 1) * LANES]
        parts.append(jnp.where((lane // HEAD_DIM) == (j % 2), x, 0.0) * scale)
    q4 = jnp.concatenate(parts, axis=0).astype(BF16)
    qpos = i * Q_BLOCK + _row_iota((Q_BLOCK, 1))

    kc = both_halves(cmp_ref[0, :, 0:LANES]).astype(BF16)
    vc = both_halves(cmp_ref[0, :, LANES:2 * LANES]).astype(BF16)
    n_tok = kc.shape[0]
    tok = _lane_iota((1, n_tok))
    cmask = (tok < n_cmp) & (tok * CMP_STRIDE + (CMP_LEN - 1) <= qpos)
    s = _dot_t(q4, kc).reshape(B_GROUP, Q_BLOCK, n_tok)
    p, l = _softmax_rows(s, cmask[None])
    p_cmp = p * _safe_inv(l)
    o_cmp = jnp.dot(p_cmp.reshape(rows, n_tok).astype(BF16), vc, preferred_element_type=F32)
    n_sel = -(-seq // SEL_BLOCK)
    sel_rows = -(-n_sel // SUBLANES) * SUBLANES
    imp = _dot_t(ovt_ref[0:sel_rows, :], jnp.sum(p_cmp, axis=0), HIGHEST)
    blk = _row_iota((sel_rows, Q_BLOCK))
    cur = (i * Q_BLOCK + _lane_iota((1, Q_BLOCK))) // SEL_BLOCK
    forced = (blk == 0) | (blk == cur) | (blk == cur - 1)
    score = jnp.where(blk <= cur, jnp.where(forced, FORCE_SCORE, imp), -jnp.inf)
    sel = _topk_mask_t(score, min(SEL_TOPN, n_sel), blk.astype(F32))
    sel_ref[0, 0] = _rows_to_lanes(jnp.where(blk <= cur, sel, 0.0))

    gates = gates_ref[0]
    outs = [_gate_column(gates, g * B_GROUP + j, 0) * o_cmp[j * Q_BLOCK:(j + 1) * Q_BLOCK] for j in range(B_GROUP)]
    left = (lane // HEAD_DIM) == 0
    ocmp_ref[0] = jnp.concatenate([jnp.where(left, outs[0], outs[1]), jnp.where(left, outs[2], outs[3])], axis=1)


def _nsa_attend_kernel(qr_ref, ksel_ref, vsel_ref, kwin_ref, vwin_ref, sel_ref, gates_ref, ocmp_ref, o_ref, *, seq):
    chunk = pl.program_id(1)
    g = chunk // (B_GROUP // 2)
    nq = seq // Q_BLOCK
    scale = HEAD_DIM ** -0.5
    lane = _lane_iota((1, LANES))
    ks = _group_halves(ksel_ref[0], g).astype(BF16)
    vs = _group_halves(vsel_ref[0], g).astype(BF16)
    kw = _group_halves(kwin_ref[0], g).astype(BF16)
    vw = _group_halves(vwin_ref[0], g).astype(BF16)
    expand = jnp.where(_row_iota((LANES, seq)) == _lane_iota((LANES, seq)) // SEL_BLOCK, 1.0, 0.0).astype(BF16)
    for i in range(nq):
        rows = slice(i * Q_BLOCK, (i + 1) * Q_BLOCK)
        n_keys = (i + 1) * Q_BLOCK
        w0 = max(i * Q_BLOCK - WINDOW, 0)
        qpos = i * Q_BLOCK + _row_iota((Q_BLOCK, 1))
        chosen = jnp.dot(sel_ref[0, 0, rows, :].astype(BF16), expand[:, 0:n_keys], preferred_element_type=F32) > 0.5
        sbias = _mask_bias(chosen & (_lane_iota((1, n_keys)) <= qpos))
        wpos = w0 + _lane_iota((1, n_keys - w0))
        wbias = _mask_bias((wpos <= qpos) & (wpos > qpos - WINDOW))
        qi = qr_ref[0, rows, :]
        gates = gates_ref[0, rows, :]
        heads = LANES // HEAD_DIM
        q2 = jnp.concatenate([jnp.where((lane // HEAD_DIM) == hh, qi, 0.0) * scale for hh in range(heads)],
                             axis=0).astype(BF16)

        def attend(k, v, bias):
            s = _dot_t(q2, k).reshape(heads, Q_BLOCK, k.shape[0])
            p, l = _softmax_biased(s, bias[None])
            o = jnp.dot(p.reshape(heads * Q_BLOCK, k.shape[0]).astype(BF16), v, preferred_element_type=F32)
            return o.reshape(heads, Q_BLOCK, LANES) * _safe_inv(l)

        o_sel = attend(ks[0:n_keys], vs[0:n_keys], sbias)
        o_win = attend(kw[w0:n_keys], vw[w0:n_keys], wbias)
        outs = []
        for hh in range(heads):
            head = chunk * heads + hh
            outs.append(_gate_column(gates, head, 1) * o_sel[hh] + _gate_column(gates, head, 2) * o_win[hh])
        o_ref[0, rows, :] = ocmp_ref[0, rows, :] + jnp.where((lane // HEAD_DIM) == 0, outs[0], outs[1])


def _nsa_prompt(bq, bqr, gates, bkv, bwin, cmp_tok):
    b, seq, _ = bq.shape
    nq = seq // Q_BLOCK
    n_tok = cmp_tok.shape[1]
    n_cmp = (seq - CMP_LEN) // CMP_STRIDE + 1
    n_sel = -(-seq // SEL_BLOCK)
    assert n_sel <= LANES
    overlap_t = _overlap_matrix(n_cmp, n_sel, n_tok, LANES).T
    ocmp, sel = pl.pallas_call(
        functools.partial(_nsa_select_kernel, seq=seq),
        grid=(b, B_KV_HEADS, nq),
        in_specs=[pl.BlockSpec((1, Q_BLOCK, 2 * LANES), lambda i, g, t: (i, t, g)),
                  pl.BlockSpec((1, Q_BLOCK, LANES), lambda i, g, t: (i, t, 0)),
                  pl.BlockSpec((1, n_tok, 2 * LANES), lambda i, g, t: (i, 0, 0)),
                  pl.BlockSpec((LANES, n_tok), lambda i, g, t: (0, 0))],
        out_specs=[pl.BlockSpec((1, Q_BLOCK, 2 * LANES), lambda i, g, t: (i, t, g)),
                   pl.BlockSpec((1, 1, Q_BLOCK, LANES), lambda i, g, t: (i, g, t, 0))],
        out_shape=[jax.ShapeDtypeStruct((b, seq, B_WIDTH), F32),
                   jax.ShapeDtypeStruct((b, B_KV_HEADS, seq, LANES), F32)],
        compiler_params=_params(("parallel", "parallel", "parallel")),
    )(bq, gates, cmp_tok, overlap_t)
    chunks = B_WIDTH // LANES
    per_group = chunks // B_KV_HEADS
    own = pl.BlockSpec((1, seq, LANES), lambda i, c: (i, 0, c))
    col = lambda k: pl.BlockSpec((1, seq, LANES), lambda i, c: (i, 0, k))
    return pl.pallas_call(
        functools.partial(_nsa_attend_kernel, seq=seq),
        grid=(b, chunks),
        in_specs=[own, col(2), col(3), col(0), col(1),
                  pl.BlockSpec((1, 1, seq, LANES), lambda i, c: (i, c // per_group, 0, 0)),
                  pl.BlockSpec((1, seq, LANES), lambda i, c: (i, 0, 0)), own],
        out_specs=own,
        out_shape=jax.ShapeDtypeStruct((b, seq, B_WIDTH), F32),
        compiler_params=_params(("parallel", "parallel")),
    )(bqr, bkv, bkv, bwin, bwin, sel, gates, ocmp)


def _merge_kernel(x_ref, oa_ref, ob_ref, ga_ref, gb_ref, pa_ref, pb_ref, wo_ref, h_ref):
    ya = jnp.dot(oa_ref[...].astype(BF16), pa_ref[...], preferred_element_type=F32)
    yb = jnp.dot(ob_ref[...].astype(BF16), pb_ref[...], preferred_element_type=F32)
    mixed = _sigmoid(ga_ref[...]) * ya + _sigmoid(gb_ref[...]) * yb
    h_ref[...] = x_ref[...] + jnp.dot(mixed.astype(BF16), wo_ref[...], preferred_element_type=F32)


def _merge(x2d, oa, ob, ga, gb, pa, pb, wo):
    n = x2d.shape[0]
    tm = min(Q_BLOCK, n)
    row = lambda w: pl.BlockSpec((tm, w), lambda i: (i, 0))
    full = lambda a: pl.BlockSpec(a.shape, lambda i: (0, 0))
    return pl.pallas_call(
        _merge_kernel,
        grid=(n // tm,),
        in_specs=[row(D_MODEL), row(A_WIDTH), row(B_WIDTH), row(D_MODEL), row(D_MODEL), full(pa), full(pb), full(wo)],
        out_specs=row(D_MODEL),
        out_shape=jax.ShapeDtypeStruct((n, D_MODEL), F32),
        compiler_params=_params(("parallel",)),
    )(x2d, oa, ob, ga, gb, pa, pb, wo)


def _cross_kernel(h_ref, mem_ref, g_ref, wq_ref, wo_ref, o_ref):
    h = h_ref[0]
    q = jnp.dot(_rms(h, g_ref[...]).astype(BF16), wq_ref[...], preferred_element_type=F32)
    scale = C_HEAD_DIM ** -0.5
    outs = []
    for hd in range(C_HEADS):
        c = slice(hd * C_HEAD_DIM, (hd + 1) * C_HEAD_DIM)
        kh = mem_ref[0, :, c].astype(BF16)
        vh = mem_ref[0, :, C_WIDTH + hd * C_HEAD_DIM:C_WIDTH + (hd + 1) * C_HEAD_DIM].astype(BF16)
        s = _dot_t(q[:, c].astype(BF16), kh) * scale
        m = jnp.max(s, axis=-1, keepdims=True)
        p = jnp.exp(s - m)
        o = jnp.dot(p.astype(BF16), vh, preferred_element_type=F32)
        outs.append(o / jnp.sum(p, axis=-1, keepdims=True))
    att = jnp.concatenate(outs, axis=1).astype(BF16)
    o_ref[0] = h + jnp.dot(att, wo_ref[...], preferred_element_type=F32)


def _cross(h3, mem_kv, g_cross, wq, wo):
    b, t, _ = h3.shape
    tq = min(Q_BLOCK, t)
    mlen = mem_kv.shape[1]
    return pl.pallas_call(
        _cross_kernel,
        grid=(b, t // tq),
        in_specs=[pl.BlockSpec((1, tq, D_MODEL), lambda i, j: (i, j, 0)),
                  pl.BlockSpec((1, mlen, 2 * C_WIDTH), lambda i, j: (i, 0, 0)),
                  pl.BlockSpec((1, D_MODEL), lambda i, j: (0, 0)),
                  pl.BlockSpec(wq.shape, lambda i, j: (0, 0)), pl.BlockSpec(wo.shape, lambda i, j: (0, 0))],
        out_specs=pl.BlockSpec((1, tq, D_MODEL), lambda i, j: (i, j, 0)),
        out_shape=jax.ShapeDtypeStruct((b, t, D_MODEL), F32),
        compiler_params=_params(("parallel", "parallel")),
    )(h3, mem_kv, g_cross.reshape(1, D_MODEL), wq, wo)


PEER_SELECT_ROWS = 512


def _peer_candidate_ids():
    ids = -np.ones((7 * SUBLANES,), np.float32)
    layout = [(0, 0, 16), (1, 16, 8), (2, 24, 5), (3, 32, 4), (4, 36, 3), (5, 40, 2), (6, 42, 2), (7, 44, 2)]
    layout += [(a, 40 + a, 1) for a in range(8, 16)]
    for a, row, n in layout:
        assert (a + 1) * n <= P_TOPK < (a + 1) * (n + 1)
        ids[row:row + n] = a * P_TOPK + np.arange(n)
    return jnp.asarray(np.tile(ids[:, None], (1, LANES)))


def _peer_select_kernel(h_ref, g_ref, wq_ref, k1_ref, k2_ref, cid_ref, z_ref, eid_ref, gate_ref, s_ref, e_scr, w_scr,
                        *, tm):
    z = _rms(h_ref[...], g_ref[...])
    z_ref[...] = z
    zb = z.astype(BF16)
    half = P_QDIM // 2
    k1 = k1_ref[...]
    k2 = k2_ref[...]
    for hd in range(P_HEADS):
        q = jnp.dot(zb, wq_ref[:, hd * P_QDIM:(hd + 1) * P_QDIM], preferred_element_type=F32)
        s1 = _dot_t(k1, q[:, 0:half], HIGHEST)
        s2 = _dot_t(k2, q[:, half:P_QDIM], HIGHEST)
        for tb in range(tm // LANES):
            s_ref[hd, tb, 0] = s1[:, tb * LANES:(tb + 1) * LANES]
            s_ref[hd, tb, 1] = s2[:, tb * LANES:(tb + 1) * LANES]

    key_f = _row_iota((P_NKEYS, LANES)).astype(F32)
    cand_valid = cid_ref[...] >= 0.0
    cand_f = jnp.where(cand_valid, cid_ref[...], float(P_TOPK * P_TOPK))
    r8 = _row_iota((SUBLANES, LANES))

    def pair_rows(x1, x2s, combine):
        x1s = jnp.concatenate(x1, axis=0)
        x2_8 = x2s[0:SUBLANES]
        shift = lambda k: pltpu.roll(x2_8, k, 0)
        return jnp.concatenate([
            combine(x1[0], x2s), combine(x1[1], x2_8), combine(x1[2], x2_8),
            jnp.where(r8 < 4, combine(x1[3], x2_8), combine(x1[4], shift(4))),
            jnp.where(r8 < 2, combine(x1[5], x2_8),
                      jnp.where(r8 < 4, combine(x1[6], shift(2)), combine(x1[7], shift(4)))),
            combine(x1s[SUBLANES:2 * SUBLANES], x2s[0:1])], axis=0)

    def take_top(x, ids_f, limit):
        mx = jnp.max(x, axis=0, keepdims=True)
        first = jnp.min(jnp.where(x == mx, ids_f, limit), axis=0, keepdims=True)
        hit = ids_f == first
        return mx, first, hit, jnp.where(hit, -jnp.inf, x)

    def select(x1, x2):
        v1, i1, v2, i2 = [], [], [], []
        for _ in range(P_TOPK):
            m, a, _, x1 = take_top(x1, key_f, float(P_NKEYS))
            v1.append(m)
            i1.append(a)
            m, a, _, x2 = take_top(x2, key_f, float(P_NKEYS))
            v2.append(m)
            i2.append(a)
        v2s = jnp.concatenate(v2, axis=0)
        i2s = jnp.concatenate(i2, axis=0)
        cand = jnp.where(cand_valid, pair_rows(v1, v2s, lambda a, b: a + b), -jnp.inf)
        cid = pair_rows(i1, i2s, lambda a, b: a * float(P_NKEYS) + b)
        es, ss = [], []
        for _ in range(P_TOPK):
            mx, _, hit, cand = take_top(cand, cand_f, float(P_TOPK * P_TOPK))
            es.append(jnp.max(jnp.where(hit, cid, -1.0), axis=0, keepdims=True))
            ss.append(mx)
        ex = jnp.exp(jnp.concatenate(ss, axis=0) - ss[0])
        return jnp.concatenate(es, axis=0), ex / jnp.sum(ex, axis=0, keepdims=True)

    def step(hd, carry):
        rows = pl.ds(pl.multiple_of(hd * P_TOPK, P_TOPK), P_TOPK)
        for tb in range(tm // LANES):
            e16, w16 = select(s_ref[hd, tb, 0], s_ref[hd, tb, 1])
            e_scr[tb, rows, :] = e16
            w_scr[tb, rows, :] = w16
        return carry

    lax.fori_loop(0, P_HEADS, step, 0)
    for tb in range(tm // LANES):
        eid_ref[tb * LANES:(tb + 1) * LANES, :] = jnp.transpose(e_scr[tb]).astype(jnp.int32)
        gate_ref[tb * LANES:(tb + 1) * LANES, :] = jnp.transpose(w_scr[tb])


def _peer_select(h2d, g_ffn, wq, k1, k2):
    n = h2d.shape[0]
    tm = min(PEER_SELECT_ROWS, n)
    assert tm % LANES == 0 and n % tm == 0
    cand_ids = _peer_candidate_ids()
    row = lambda: pl.BlockSpec((tm, D_MODEL), lambda i: (i, 0))
    pick = lambda: pl.BlockSpec((tm, LANES), lambda i: (i, 0))
    full = lambda a: pl.BlockSpec(a.shape, lambda i: (0, 0))
    return pl.pallas_call(
        functools.partial(_peer_select_kernel, tm=tm),
        grid=(n // tm,),
        in_specs=[row(), pl.BlockSpec((1, D_MODEL), lambda i: (0, 0)), full(wq), full(k1), full(k2), full(cand_ids)],
        out_specs=[row(), pick(), pick()],
        out_shape=[jax.ShapeDtypeStruct((n, D_MODEL), F32), jax.ShapeDtypeStruct((n, LANES), jnp.int32),
                   jax.ShapeDtypeStruct((n, LANES), F32)],
        scratch_shapes=[pltpu.VMEM((P_HEADS, tm // LANES, 2, P_NKEYS, LANES), F32),
                        pltpu.VMEM((tm // LANES, P_PICKS, LANES), F32), pltpu.VMEM((tm // LANES, P_PICKS, LANES), F32)],
        compiler_params=_params(("parallel",)),
    )(h2d, g_ffn.reshape(1, D_MODEL), wq, k1, k2, cand_ids)


PEER_TOKENS = 8
ROW_TILES = D_MODEL // LANES


def _peer_gather_kernel(eid_ref, h_ref, z_ref, gate_ref, gfin_ref, uv_hbm, y_ref, buf, sem):
    j = pl.program_id(0)
    n_tiles = pl.num_programs(0) - 1
    rows = PEER_TOKENS * P_PICKS

    for to_slot in range(2):
        @pl.when((j < n_tiles) & (j % 2 == to_slot))
        def _(to_slot=to_slot):
            for r in range(rows):
                e = eid_ref[r // P_PICKS, r % P_PICKS]
                pltpu.make_async_copy(uv_hbm.at[e], buf.at[to_slot, pl.ds(r, 1)], sem.at[to_slot]).start()

    @pl.when(j > 0)
    def _():
        slot = (j - 1) % 2
        pltpu.make_async_copy(uv_hbm.at[pl.ds(0, rows), 0], buf.at[slot], sem.at[slot]).wait()
        gate_t = jnp.transpose(jnp.concatenate(
            [gate_ref[...], jnp.zeros((LANES - PEER_TOKENS, P_PICKS), F32)], axis=0))
        for p in range(PEER_TOKENS):
            picks = pl.ds(p * P_PICKS, P_PICKS)
            acc = jnp.zeros((P_PICKS, LANES), F32)
            for s in range(ROW_TILES):
                acc += buf[slot, picks, s * LANES:(s + 1) * LANES] * z_ref[p:p + 1, s * LANES:(s + 1) * LANES]
            act = jax.nn.gelu(jnp.sum(acc, axis=-1, keepdims=True))
            coef = jnp.broadcast_to(gate_t[:, p:p + 1] * act, (P_PICKS, LANES))
            outs = []
            for s in range(ROW_TILES):
                v_s = buf[slot, picks, D_MODEL + s * LANES:D_MODEL + (s + 1) * LANES]
                outs.append(jnp.sum(coef * v_s, axis=0, keepdims=True))
            y_ref[p:p + 1, :] = h_ref[p:p + 1, :] + jnp.concatenate(outs, axis=1)
        y_ref[...] = _rms(y_ref[...], gfin_ref[...])


def _peer_gather(eid, h2d, z, gate, g_final, uv):
    n = h2d.shape[0]
    tiles = n // PEER_TOKENS
    ahead = lambda j: (jnp.minimum(j, tiles - 1), 0)
    behind = lambda j: (jnp.maximum(j - 1, 0), 0)
    row = lambda w: pl.BlockSpec((PEER_TOKENS, w), behind)
    return pl.pallas_call(
        _peer_gather_kernel,
        grid=(tiles + 1,),
        in_specs=[pl.BlockSpec((PEER_TOKENS, P_PICKS), ahead, memory_space=pltpu.SMEM),
                  row(D_MODEL), row(D_MODEL), row(P_PICKS), pl.BlockSpec((1, D_MODEL), lambda j: (0, 0)),
                  pl.BlockSpec(memory_space=pl.ANY)],
        out_specs=row(D_MODEL),
        out_shape=jax.ShapeDtypeStruct((n, D_MODEL), F32),
        scratch_shapes=[pltpu.VMEM((2, PEER_TOKENS * P_PICKS, 2 * D_MODEL), F32),
                        pltpu.SemaphoreType.DMA((2,))],
        compiler_params=_params(("arbitrary",)),
    )(eid, h2d, z, gate, g_final.reshape(1, D_MODEL), uv)


PAGES_PER_STEP = 16
SAMPLE_ROWS = LANES


def _page_specs(rows, row_block):
    def spec(k):
        return pl.BlockSpec((1, rows, PAGE_SIZE), lambda b, c, pt: (pt[b, c * PAGES_PER_STEP + k], row_block, 0))
    return [spec(k) for k in range(PAGES_PER_STEP)]


def _online_step(s, mask, pv_fn, m_ref, l_ref, acc_ref):
    s = jnp.where(mask, s, NEG_BIG)
    m_old = m_ref[...]
    m_new = jnp.maximum(m_old, jnp.max(s, axis=-1, keepdims=True))
    p = jnp.where(mask, jnp.exp(s - m_new), 0.0)
    alpha = jnp.exp(m_old - m_new)
    l_ref[...] = alpha * l_ref[...] + jnp.sum(p, axis=-1, keepdims=True)
    acc_ref[...] = alpha * acc_ref[...] + pv_fn(p.astype(BF16))
    m_ref[...] = m_new


def _online_update(s, mask, v, m_ref, l_ref, acc_ref):
    _online_step(s, mask, lambda p: jnp.dot(p, v, preferred_element_type=F32), m_ref, l_ref, acc_ref)


def _online_update_t(s, mask, v_t, m_ref, l_ref, acc_ref):
    _online_step(s, mask, lambda p: jnp.transpose(_dot_t(v_t, p)), m_ref, l_ref, acc_ref)


def _moba_kmean_kernel(pt_ref, *refs):
    pages, o_ref = refs[:PAGES_PER_STEP], refs[PAGES_PER_STEP]
    c = pl.program_id(1)

    @pl.when(c == 0)
    def _():
        o_ref[...] = jnp.zeros(o_ref.shape, F32)

    n_keys = PAGES_PER_STEP * PAGE_SIZE
    n_lanes = o_ref.shape[2]
    k_t = jnp.concatenate([p[0] for p in pages], axis=1)
    blk = c * (n_keys // A_BLOCK) + _row_iota((n_keys, n_lanes)) // A_BLOCK
    avg = jnp.where(_lane_iota((n_keys, n_lanes)) == blk, 1.0 / A_BLOCK, 0.0).astype(BF16)
    k_hi = k_t.astype(BF16)
    k_lo = (k_t - k_hi.astype(F32)).astype(BF16)
    o_ref[0] += (jnp.dot(k_hi, avg, preferred_element_type=F32) + jnp.dot(k_lo, avg, preferred_element_type=F32))


def _moba_kmean(pool_t, page_table, n_lanes):
    b, n_pages = page_table.shape
    return pl.pallas_call(
        _moba_kmean_kernel,
        grid_spec=pltpu.PrefetchScalarGridSpec(
            num_scalar_prefetch=1, grid=(b, n_pages // PAGES_PER_STEP),
            in_specs=_page_specs(A_WIDTH, 0),
            out_specs=pl.BlockSpec((1, A_WIDTH, n_lanes), lambda i, c, pt: (i, 0, 0))),
        out_shape=jax.ShapeDtypeStruct((b, A_WIDTH, n_lanes), F32),
        compiler_params=_params(("parallel", "arbitrary")),
    )(page_table, *([pool_t] * PAGES_PER_STEP))


def _moba_sample_kernel(pt_ref, *refs, past, t_new):
    pages = refs[:PAGES_PER_STEP]
    kmean_ref, q_ref, new_ref, o_ref, sel_ref, m_ref, l_ref, acc_ref = refs[PAGES_PER_STEP:]
    c = pl.program_id(1)
    rows = A_HEADS * t_new
    n_lanes = sel_ref.shape[1]
    cur = past // A_BLOCK
    q = q_ref[0]
    qb = (q * HEAD_DIM ** -0.5).astype(BF16)

    @pl.when(c == 0)
    def _():
        lane = _lane_iota((1, n_lanes))
        gate = jnp.dot(q, kmean_ref[0], precision=HIGHEST, preferred_element_type=F32)
        gate = jnp.where(lane < cur, gate, -jnp.inf)
        sel = _topk_mask(gate, min(A_TOPK, cur + 1), lane.astype(F32))
        sel_ref[...] = jnp.where(lane < cur, sel, 0.0)
        m_ref[...] = jnp.full(m_ref.shape, NEG_BIG, F32)
        l_ref[...] = jnp.zeros(l_ref.shape, F32)
        acc_ref[...] = jnp.zeros(acc_ref.shape, F32)

    n_keys = PAGES_PER_STEP * PAGE_SIZE
    k_t = jnp.concatenate([p[0, 0:A_WIDTH, :] for p in pages], axis=1).astype(BF16)
    v_t = jnp.concatenate([p[0, A_WIDTH:2 * A_WIDTH, :] for p in pages], axis=1).astype(BF16)
    blk = c * (n_keys // A_BLOCK) + _lane_iota((n_lanes, n_keys)) // A_BLOCK
    expand = jnp.where(_row_iota((n_lanes, n_keys)) == blk, 1.0, 0.0).astype(BF16)
    chosen = jnp.dot(sel_ref[...].astype(BF16), expand, preferred_element_type=F32) > 0.5
    _online_update_t(jnp.dot(qb, k_t, preferred_element_type=F32), chosen, v_t, m_ref, l_ref, acc_ref)

    @pl.when(c == pl.num_programs(1) - 1)
    def _():
        kn = new_ref[0, :, 0:A_WIDTH].astype(BF16)
        vn = new_ref[0, :, A_WIDTH:2 * A_WIDTH].astype(BF16)
        t_key = _lane_iota((SAMPLE_ROWS, kn.shape[0]))
        t_row = _row_iota((SAMPLE_ROWS, kn.shape[0])) % t_new
        _online_update(_dot_t(qb, kn), (t_key <= t_row) & (t_key < t_new), vn, m_ref, l_ref, acc_ref)
        own = (_lane_iota((SAMPLE_ROWS, A_WIDTH)) // HEAD_DIM) == (_row_iota((SAMPLE_ROWS, A_WIDTH)) // t_new)
        o_ref[0] = jnp.where(own, acc_ref[...] * _safe_inv(l_ref[...]), 0.0)[0:rows]


def _moba_sample(pool_t, page_table, kmean_t, q_rows, new_kv, past, t_new):
    b, n_pages = page_table.shape
    rows = A_HEADS * t_new
    n_lanes = kmean_t.shape[2]
    per_b = lambda s: pl.BlockSpec((1,) + s, lambda i, c, pt: (i, 0, 0))
    return pl.pallas_call(
        functools.partial(_moba_sample_kernel, past=past, t_new=t_new),
        grid_spec=pltpu.PrefetchScalarGridSpec(
            num_scalar_prefetch=1, grid=(b, n_pages // PAGES_PER_STEP),
            in_specs=_page_specs(2 * A_WIDTH, 0) + [per_b(kmean_t.shape[1:]), per_b(q_rows.shape[1:]),
                                                     per_b(new_kv.shape[1:])],
            out_specs=per_b((rows, A_WIDTH)),
            scratch_shapes=[pltpu.VMEM((SAMPLE_ROWS, n_lanes), F32), pltpu.VMEM((SAMPLE_ROWS, 1), F32),
                            pltpu.VMEM((SAMPLE_ROWS, 1), F32), pltpu.VMEM((SAMPLE_ROWS, A_WIDTH), F32)]),
        out_shape=jax.ShapeDtypeStruct((b, rows, A_WIDTH), F32),
        compiler_params=_params(("parallel", "arbitrary")),
    )(page_table, *([pool_t] * PAGES_PER_STEP), kmean_t, q_rows, new_kv)


def _compress_sample_kernel(pt_ref, *refs, past):
    pages = refs[:PAGES_PER_STEP]
    pe_ref, wa_ref, wb_ref, w2_ref, o_ref, xk_ref, xv_ref = refs[PAGES_PER_STEP:]
    c = pl.program_id(1)
    for k, page in enumerate(pages):
        start = pl.multiple_of((c * PAGES_PER_STEP + k) * PAGE_SIZE, PAGE_SIZE)
        xk_ref[pl.ds(start, PAGE_SIZE), :] = jnp.transpose(page[0, 0:LANES, :])
        xv_ref[pl.ds(start, PAGE_SIZE), :] = jnp.transpose(page[0, LANES:2 * LANES, :])

    @pl.when(c == pl.num_programs(1) - 1)
    def _():
        o_ref[0] = _compress_rows(xk_ref, xv_ref, past // CMP_STRIDE, pe_ref, wa_ref, wb_ref, w2_ref)


def _compress_sample(pool, page_table, cw, past):
    b, n_pages = page_table.shape
    n_rows = past // CMP_STRIDE
    pe, wa, wb, w2 = cw
    full = lambda a: pl.BlockSpec(a.shape, lambda i, c, pt: (0,) * a.ndim)
    return pl.pallas_call(
        functools.partial(_compress_sample_kernel, past=past),
        grid_spec=pltpu.PrefetchScalarGridSpec(
            num_scalar_prefetch=1, grid=(b, n_pages // PAGES_PER_STEP),
            in_specs=_page_specs(2 * LANES, 0) + [full(pe), full(wa), full(wb), full(w2)],
            out_specs=pl.BlockSpec((1, n_rows, 2 * LANES), lambda i, c, pt: (i, 0, 0)),
            scratch_shapes=[pltpu.VMEM((past, LANES), F32), pltpu.VMEM((past, LANES), F32)]),
        out_shape=jax.ShapeDtypeStruct((b, n_rows, 2 * LANES), F32),
        compiler_params=_params(("parallel", "arbitrary")),
    )(page_table, *([pool] * PAGES_PER_STEP), pe, wa, wb, w2)


def _nsa_sample_kernel(pt_ref, *refs, past, t_new):
    pages = refs[:PAGES_PER_STEP]
    (cmp_ref, q_ref, qr_ref, gates_ref, newsel_ref, win_ref, newwin_ref, ov_ref, o_ref,
     kv_ref, m_ref, l_ref, acc_ref) = refs[PAGES_PER_STEP:]
    c = pl.program_id(1)
    for k, page in enumerate(pages):
        kv_ref[c * PAGES_PER_STEP + k] = page[0]

    @pl.when(c == pl.num_programs(1) - 1)
    def _():
        rows = B_HEADS * t_new
        slab = B_KV_HEADS * t_new
        scale = HEAD_DIM ** -0.5
        qb = (q_ref[0] * scale).astype(BF16)
        qrb = (qr_ref[0] * scale).astype(BF16)
        pos = past + _row_iota((SAMPLE_ROWS, 1)) % t_new

        n_tok = cmp_ref.shape[1]
        n_cmp = n_tok - 1
        kc = cmp_ref[0, :, 0:LANES].astype(BF16)
        vc = cmp_ref[0, :, LANES:2 * LANES].astype(BF16)
        tok = _lane_iota((1, n_tok))
        p, l = _softmax_rows(_dot_t(qb, kc), (tok < n_cmp) & (tok * CMP_STRIDE + (CMP_LEN - 1) <= pos))
        p_cmp = p * _safe_inv(l)
        o_cmp = jnp.dot(p_cmp.astype(BF16), vc, preferred_element_type=F32)
        p_sum = sum(p_cmp[j * slab:(j + 1) * slab] for j in range(B_GROUP))
        imp = jnp.dot(p_sum, ov_ref[...], precision=HIGHEST, preferred_element_type=F32)
        n_lanes = ov_ref.shape[1]
        lane = _lane_iota((1, n_lanes))
        cur = past // SEL_BLOCK
        forced = (lane == 0) | (lane == cur) | (lane == cur - 1)
        score = jnp.where(lane <= cur, jnp.where(forced, FORCE_SCORE, imp), -jnp.inf)
        sel = _topk_mask(score, min(SEL_TOPN, cur + 1), lane.astype(F32))
        sel = jnp.where(lane <= cur, sel, 0.0)
        sel_b = sel.astype(BF16)

        m_ref[...] = jnp.full(m_ref.shape, NEG_BIG, F32)
        l_ref[...] = jnp.zeros(l_ref.shape, F32)
        acc_ref[...] = jnp.zeros(acc_ref.shape, F32)
        n_keys = PAGES_PER_STEP * PAGE_SIZE
        tile_rows = lambda a: jnp.concatenate([a] * (SAMPLE_ROWS // slab), axis=0)
        for ci in range(past // n_keys):
            pages_ci = range(ci * PAGES_PER_STEP, (ci + 1) * PAGES_PER_STEP)
            ks_t = jnp.concatenate([kv_ref[pg, 0:LANES, :] for pg in pages_ci], axis=1).astype(BF16)
            vs_t = jnp.concatenate([kv_ref[pg, LANES:2 * LANES, :] for pg in pages_ci], axis=1).astype(BF16)
            blk = (ci * n_keys + _lane_iota((n_lanes, n_keys))) // SEL_BLOCK
            expand = jnp.where(_row_iota((n_lanes, n_keys)) == blk, 1.0, 0.0).astype(BF16)
            chosen = jnp.dot(sel_b, expand, preferred_element_type=F32) > 0.5
            _online_update_t(jnp.dot(qrb, ks_t, preferred_element_type=F32), tile_rows(chosen), vs_t,
                             m_ref, l_ref, acc_ref)
        kn = newsel_ref[0, :, 0:LANES].astype(BF16)
        vn = newsel_ref[0, :, LANES:2 * LANES].astype(BF16)
        t_key = _lane_iota((SAMPLE_ROWS, kn.shape[0]))
        cur_chosen = tile_rows(jnp.sum(jnp.where(lane == cur, sel, 0.0), axis=-1, keepdims=True)) > 0.5
        _online_update(_dot_t(qrb, kn), cur_chosen & (past + t_key <= pos) & (t_key < t_new), vn,
                       m_ref, l_ref, acc_ref)
        o_sel = acc_ref[...] * _safe_inv(l_ref[...])

        n_win = win_ref.shape[1]
        kw = jnp.concatenate([win_ref[0, :, 0:LANES], newwin_ref[0, :, 0:LANES]], axis=0).astype(BF16)
        vw = jnp.concatenate([win_ref[0, :, LANES:2 * LANES], newwin_ref[0, :, LANES:2 * LANES]], axis=0).astype(BF16)
        wpos = past - n_win + _lane_iota((1, kw.shape[0]))
        p, l = _softmax_rows(_dot_t(qrb, kw), (wpos <= pos) & (wpos > pos - WINDOW))
        o_win = jnp.dot(p.astype(BF16), vw, preferred_element_type=F32) * _safe_inv(l)

        gt = _sigmoid(gates_ref[0])
        o = gt[:, 0:1] * o_cmp + gt[:, 1:2] * o_sel + gt[:, 2:3] * o_win
        shape = (SAMPLE_ROWS, LANES)
        own = (_lane_iota(shape) // HEAD_DIM) == ((_row_iota(shape) // t_new) % B_KV_HEADS)
        o_ref[0] = jnp.where(own, o, 0.0)[0:rows]


def _nsa_sample(pool_t, page_table, cmp_tok, q_rows, qr_rows, gate_rows, new_sel, win_state, new_win, past, t_new):
    b, n_pages = page_table.shape
    rows = B_HEADS * t_new
    n_tok = cmp_tok.shape[1]
    n_sel = past // SEL_BLOCK + 1
    n_lanes = -(-n_sel // LANES) * LANES
    overlap = _overlap_matrix(n_tok - 1, n_sel, n_tok, n_lanes)
    per_b = lambda a: pl.BlockSpec((1,) + a.shape[1:], lambda i, c, pt: (i, 0, 0))
    return pl.pallas_call(
        functools.partial(_nsa_sample_kernel, past=past, t_new=t_new),
        grid_spec=pltpu.PrefetchScalarGridSpec(
            num_scalar_prefetch=1, grid=(b, n_pages // PAGES_PER_STEP),
            in_specs=_page_specs(2 * LANES, 1) + [per_b(cmp_tok), per_b(q_rows), per_b(qr_rows), per_b(gate_rows),
                                                  per_b(new_sel), per_b(win_state), per_b(new_win),
                                                  pl.BlockSpec(overlap.shape, lambda i, c, pt: (0, 0))],
            out_specs=pl.BlockSpec((1, rows, LANES), lambda i, c, pt: (i, 0, 0)),
            scratch_shapes=[pltpu.VMEM((n_pages, 2 * LANES, PAGE_SIZE), F32), pltpu.VMEM((SAMPLE_ROWS, 1), F32),
                            pltpu.VMEM((SAMPLE_ROWS, 1), F32), pltpu.VMEM((SAMPLE_ROWS, LANES), F32)]),
        out_shape=jax.ShapeDtypeStruct((b, rows, LANES), F32),
        compiler_params=_params(("parallel", "arbitrary")),
    )(page_table, *([pool_t] * PAGES_PER_STEP), cmp_tok, q_rows, qr_rows, gate_rows, new_sel, win_state, new_win,
      overlap)


def _layer_weights(w_in, pe_cmp, w_ck1, w_ck2, w_cv1, w_cv2, p_a, p_b, w_o, w_cq, w_ckv, w_co, w_pq, peer_u, peer_v):
    w_proj = jnp.concatenate(
        [w_in[:, :N_MAIN], jnp.pad(w_in[:, N_MAIN:N_MAIN + N_GATES], ((0, 0), (0, LANES - N_GATES))),
         w_in[:, N_MAIN + N_GATES:]], axis=1).astype(BF16)
    uv = jnp.concatenate([peer_u, peer_v], axis=1)[:, None, :]
    return dict(w_proj=w_proj, cw=_compress_weights(pe_cmp, w_ck1, w_ck2, w_cv1, w_cv2),
                p_a=p_a.astype(BF16), p_b=p_b.astype(BF16), w_o=w_o.astype(BF16), w_cq=w_cq.astype(BF16),
                w_ckv=w_ckv.astype(BF16), w_co=w_co.astype(BF16), w_pq=w_pq.astype(BF16),
                uv=uv)


def _channel_and_norm(h2d, batch, mem_kv, w, g_cross, g_ffn, sub_k1, sub_k2, g_final):
    h3 = _cross(h2d.reshape(batch, -1, D_MODEL), mem_kv, g_cross, w["w_cq"], w["w_co"])
    h2 = h3.reshape(-1, D_MODEL)
    z, eid, gate = _peer_select(h2, g_ffn, w["w_pq"], sub_k1, sub_k2)
    return _peer_gather(eid, h2, z, gate, g_final, w["uv"])


def _prompt_group(x, mem, w, g_attn, g_cross, g_mem, g_ffn, sub_k1, sub_k2, g_final):
    b, seq, _ = x.shape
    x2d = x.reshape(b * seq, D_MODEL)
    aq, akv, bq, bqr, bkv, bwin, gates, ga, gb = _projection(x2d, jnp.arange(seq, dtype=jnp.int32), seq, g_attn,
                                                             w["w_proj"])
    r3 = lambda a: a.reshape(b, seq, a.shape[-1])
    oa = _moba_prompt(r3(aq), r3(akv))
    cmp_tok = _compress_prompt(r3(bkv), w["cw"])
    ob = _nsa_prompt(r3(bq), r3(bqr), r3(gates), r3(bkv), r3(bwin), cmp_tok)
    h = _merge(x2d, oa.reshape(-1, A_WIDTH), ob.reshape(-1, B_WIDTH), ga, gb, w["p_a"], w["p_b"], w["w_o"])
    mlen = mem.shape[1]
    mem_kv = _rms_matmul(mem.reshape(b * mlen, D_MODEL), g_mem, w["w_ckv"]).reshape(b, mlen, 2 * C_WIDTH)
    y = _channel_and_norm(h, b, mem_kv, w, g_cross, g_ffn, sub_k1, sub_k2, g_final)
    win = r3(bwin)[:, seq - min(WINDOW, seq):]
    return (y.reshape(b, seq, D_MODEL), r3(akv).reshape(b, seq, 2, A_HEADS, HEAD_DIM),
            r3(bkv).reshape(b, seq, 4, B_KV_HEADS, HEAD_DIM), win.reshape(b, -1, 2, B_KV_HEADS, HEAD_DIM),
            mem_kv.reshape(b, mlen, 2, C_HEADS, C_HEAD_DIM))


def _pad_rows(a, rows):
    return jnp.pad(a, ((0, 0), (0, rows - a.shape[1]), (0, 0)))


def _sample_group(x, moba_pool, nsa_pool, win_state, mem_kv, page_table, w, g_attn, g_cross, g_ffn, sub_k1, sub_k2,
                  g_final):
    b, t, _ = x.shape
    past = page_table.shape[1] * PAGE_SIZE
    assert t * B_KV_HEADS == SUBLANES and past % (PAGES_PER_STEP * PAGE_SIZE) == 0 and B_HEADS * t <= SAMPLE_ROWS
    x2d = x.reshape(b * t, D_MODEL)
    pos = past + jnp.arange(t, dtype=jnp.int32)
    aq, akv, bq, bqr, bkv, bwin, gates, ga, gb = _projection(x2d, pos, t, g_attn, w["w_proj"])
    r3 = lambda a: a.reshape(b, t, a.shape[-1])

    n_pool = moba_pool.shape[0]
    pool_a = moba_pool.transpose(0, 2, 3, 4, 1).reshape(n_pool, 2 * A_WIDTH, PAGE_SIZE)
    pool_b = nsa_pool.transpose(0, 2, 3, 4, 1).reshape(n_pool, 4 * B_KV_WIDTH, PAGE_SIZE)

    n_blocks = past // A_BLOCK
    kmean_t = _moba_kmean(pool_a, page_table, -(-n_blocks // LANES) * LANES)
    qa = aq.reshape(b, t, A_HEADS, HEAD_DIM).transpose(0, 2, 1, 3)
    qa_rows = (qa[:, :, :, None, :] * jnp.eye(A_HEADS, dtype=F32)[None, :, None, :, None]).reshape(b, A_HEADS * t, A_WIDTH)
    oa_rows = _moba_sample(pool_a, page_table, kmean_t, _pad_rows(qa_rows, SAMPLE_ROWS),
                           _pad_rows(r3(akv), SUBLANES), past, t)
    oa = oa_rows.reshape(b, A_HEADS, t, A_WIDTH).sum(axis=1)

    cmp_tok = _compress_sample(pool_b, page_table, w["cw"], past)
    eye_g = jnp.eye(B_KV_HEADS, dtype=F32)[None, None, :, None, :, None]

    def group_rows(a):
        a = a.reshape(b, t, B_KV_HEADS, B_GROUP, HEAD_DIM).transpose(0, 3, 2, 1, 4)
        return _pad_rows((a[:, :, :, :, None, :] * eye_g).reshape(b, B_HEADS * t, LANES), SAMPLE_ROWS)

    gate_rows = gates[:, :N_GATES].reshape(b, t, B_KV_HEADS, B_GROUP, 3).transpose(0, 3, 2, 1, 4)
    gate_rows = jnp.pad(gate_rows.reshape(b, B_HEADS * t, 3), ((0, 0), (0, SAMPLE_ROWS - B_HEADS * t), (0, LANES - 3)))
    win_rows = win_state.reshape(b, win_state.shape[1], 2 * B_KV_WIDTH)
    ob_rows = _nsa_sample(pool_b, page_table, cmp_tok, group_rows(bq), group_rows(bqr), gate_rows,
                          _pad_rows(r3(bkv)[:, :, 2 * B_KV_WIDTH:], SUBLANES), win_rows,
                          _pad_rows(r3(bwin), SUBLANES), past, t)
    ob = ob_rows.reshape(b, B_GROUP, B_KV_HEADS, t, B_KV_HEADS, HEAD_DIM).sum(axis=4)
    ob = ob.transpose(0, 3, 2, 1, 4).reshape(b * t, B_WIDTH)

    h = _merge(x2d, oa.reshape(-1, A_WIDTH), ob, ga, gb, w["p_a"], w["p_b"], w["w_o"])
    mem_rows = mem_kv.reshape(b, mem_kv.shape[1], 2 * C_WIDTH)
    y = _channel_and_norm(h, b, mem_rows, w, g_cross, g_ffn, sub_k1, sub_k2, g_final)
    win = jnp.concatenate([win_rows, r3(bwin)], axis=1)
    win = win[:, win.shape[1] - min(WINDOW, win.shape[1]):]
    return (y.reshape(b, t, D_MODEL), r3(akv).reshape(b, t, 2, A_HEADS, HEAD_DIM),
            r3(bkv).reshape(b, t, 4, B_KV_HEADS, HEAD_DIM), win.reshape(b, -1, 2, B_KV_HEADS, HEAD_DIM))


def kernel(x_prompt, x_sample, cache_moba_kv, cache_nsa_kv, state_nsa_win, cache_mem_kv, page_table, mem_prompt, g_attn, w_in, pe_cmp, w_ck1, w_ck2, w_cv1, w_cv2, p_a, p_b, w_o, g_cross, g_mem, w_cq, w_ckv, w_co, g_ffn, w_pq, sub_k1, sub_k2, peer_u, peer_v, g_final):
    assert g_attn.shape[0] == 1, "the final norm is fused into the last PEER step of a single layer"
    w = _layer_weights(w_in[0], pe_cmp[0], w_ck1[0], w_ck2[0], w_cv1[0], w_cv2[0], p_a[0], p_b[0], w_o[0], w_cq[0],
                       w_ckv[0], w_co[0], w_pq[0], peer_u[0], peer_v[0])
    y_p, moba_p, nsa_p, win_p, mem_p = _prompt_group(x_prompt, mem_prompt, w, g_attn[0], g_cross[0], g_mem[0],
                                                    g_ffn[0], sub_k1[0], sub_k2[0], g_final)
    y_s, moba_s, nsa_s, win_s = _sample_group(x_sample, cache_moba_kv[0], cache_nsa_kv[0], state_nsa_win[0],
                                              cache_mem_kv[0], page_table, w, g_attn[0], g_cross[0], g_ffn[0],
                                              sub_k1[0], sub_k2[0], g_final)
    return (y_p, y_s, moba_p[None], moba_s[None], nsa_p[None], nsa_s[None], win_p[None], win_s[None], mem_p[None])
```

```python
import functools

import numpy as np
import jax
import jax.numpy as jnp
from jax import lax
from jax.experimental import pallas as pl
from jax.experimental.pallas import tpu as pltpu

F32 = jnp.float32
BF16 = jnp.bfloat16
HIGHEST = lax.Precision.HIGHEST

LANES = 128
SUBLANES = 8
VMEM_LIMIT_BYTES = 56 * 1024 * 1024

D_MODEL = 1024
HEAD_DIM = 64
ROPE_THETA = 10000.0
NORM_EPS = 1e-6
NEG_BIG = -1e30
PAGE_SIZE = 128

A_HEADS = 8
A_BLOCK = 256
A_TOPK = 3
A_WIDTH = A_HEADS * HEAD_DIM

B_HEADS = 8
B_KV_HEADS = 2
B_GROUP = B_HEADS // B_KV_HEADS
B_WIDTH = B_HEADS * HEAD_DIM
B_KV_WIDTH = B_KV_HEADS * HEAD_DIM
CMP_LEN = 32
CMP_STRIDE = 16
CMP_HIDDEN = 64
SEL_BLOCK = 64
SEL_TOPN = 16
WINDOW = 512
FORCE_SCORE = 1e4

C_HEADS = 4
C_HEAD_DIM = 128
C_WIDTH = C_HEADS * C_HEAD_DIM

P_HEADS = 8
P_NKEYS = 128
P_QDIM = 256
P_TOPK = 16
P_PICKS = P_HEADS * P_TOPK

Q_BLOCK = 256
N_MAIN = 3 * A_WIDTH + B_WIDTH + 6 * B_KV_WIDTH
N_GATES = 3 * B_HEADS
N_PROJ = N_MAIN + LANES + 2 * D_MODEL


def _params(semantics):
    return pltpu.CompilerParams(dimension_semantics=semantics, vmem_limit_bytes=VMEM_LIMIT_BYTES)


def _lane_iota(shape, dtype=jnp.int32):
    return lax.broadcasted_iota(dtype, shape, len(shape) - 1)


def _row_iota(shape, dtype=jnp.int32):
    return lax.broadcasted_iota(dtype, shape, len(shape) - 2)


def _rms(x, g):
    return x * lax.rsqrt(jnp.mean(x * x, axis=-1, keepdims=True) + NORM_EPS) * g


def _sigmoid(x):
    return 1.0 / (1.0 + jnp.exp(-x))


def _dot_t(a, b, precision=None):
    return lax.dot_general(a, b, (((1,), (1,)), ((), ())), precision=precision, preferred_element_type=F32)


def _topk_mask(x, k, lane_f):
    sel = jnp.zeros(x.shape, F32)
    for _ in range(k):
        mx = jnp.max(x, axis=-1, keepdims=True)
        first = jnp.min(jnp.where(x == mx, lane_f, float(x.shape[-1])), axis=-1, keepdims=True)
        hit = lane_f == first
        sel = jnp.where(hit, 1.0, sel)
        x = jnp.where(hit, -jnp.inf, x)
    return sel


def _topk_mask_t(x, k, ids_f):
    sel = jnp.zeros(x.shape, F32)
    for _ in range(k):
        mx = jnp.max(x, axis=0, keepdims=True)
        first = jnp.min(jnp.where(x == mx, ids_f, float(2 ** 20)), axis=0, keepdims=True)
        hit = ids_f == first
        sel = jnp.where(hit, 1.0, sel)
        x = jnp.where(hit, -jnp.inf, x)
    return sel


def _rows_to_lanes(sel_t):
    pad = jnp.zeros((LANES - sel_t.shape[0], sel_t.shape[1]), F32)
    return jnp.transpose(jnp.concatenate([sel_t, pad], axis=0))


def _softmax_rows(s, mask):
    s = jnp.where(mask, s, NEG_BIG)
    m = jnp.max(s, axis=-1, keepdims=True)
    p = jnp.where(mask, jnp.exp(s - m), 0.0)
    return p, jnp.sum(p, axis=-1, keepdims=True)


def _softmax_biased(s, bias):
    s = s + bias
    p = jnp.exp(s - jnp.max(s, axis=-1, keepdims=True))
    return p, jnp.sum(p, axis=-1, keepdims=True)


def _mask_bias(mask):
    return jnp.where(mask, 0.0, NEG_BIG)


def _safe_inv(l):
    return jnp.where(l > 0.0, 1.0 / jnp.where(l > 0.0, l, 1.0), 0.0)


def _proj_kernel(x_ref, g_ref, w_ref, cs_ref, sn_ref, aq_ref, akv_ref, bq_ref, bqr_ref, bkv_ref, bwin_ref,
                 gates_ref, ga_ref, gb_ref):
    ub = _rms(x_ref[...], g_ref[...]).astype(BF16)
    cs = cs_ref[...]
    sn = sn_ref[...]
    first_half = (_lane_iota((1, LANES)) % HEAD_DIM) < (HEAD_DIM // 2)

    def cols(c0, n):
        return jnp.dot(ub, w_ref[:, c0:c0 + n], preferred_element_type=F32)

    def rot(p):
        swapped = jnp.where(first_half, pltpu.roll(p, LANES - HEAD_DIM // 2, 1), pltpu.roll(p, HEAD_DIM // 2, 1))
        return p * cs + swapped * sn

    def rot_wide(p):
        return jnp.concatenate([rot(p[:, c:c + LANES]) for c in range(0, p.shape[1], LANES)], axis=1)

    aq_ref[...] = rot_wide(cols(0, A_WIDTH))
    akv_ref[:, 0:A_WIDTH] = rot_wide(cols(A_WIDTH, A_WIDTH))
    akv_ref[:, A_WIDTH:2 * A_WIDTH] = cols(2 * A_WIDTH, A_WIDTH)
    bq = cols(3 * A_WIDTH, B_WIDTH)
    bq_ref[...] = bq
    bqr_ref[...] = rot_wide(bq)
    c0 = 3 * A_WIDTH + B_WIDTH
    bkv = cols(c0, 4 * B_KV_WIDTH)
    bkv_ref[:, 0:2 * LANES] = bkv[:, 0:2 * LANES]
    bkv_ref[:, 2 * LANES:3 * LANES] = rot(bkv[:, 2 * LANES:3 * LANES])
    bkv_ref[:, 3 * LANES:4 * LANES] = bkv[:, 3 * LANES:4 * LANES]
    bwin = cols(c0 + 4 * B_KV_WIDTH, 2 * B_KV_WIDTH)
    bwin_ref[:, 0:LANES] = rot(bwin[:, 0:LANES])
    bwin_ref[:, LANES:2 * LANES] = bwin[:, LANES:2 * LANES]
    gates_ref[...] = cols(N_MAIN, LANES)
    ga_ref[...] = cols(N_MAIN + LANES, D_MODEL)
    gb_ref[...] = cols(N_MAIN + LANES + D_MODEL, D_MODEL)


def _rope_tables(pos):
    half = HEAD_DIM // 2
    inv_freq = ROPE_THETA ** (-jnp.arange(half, dtype=F32) / half)
    ang = pos.astype(F32)[:, None] * inv_freq[None, :]
    cos, sin = jnp.cos(ang), jnp.sin(ang)
    reps = LANES // HEAD_DIM
    return jnp.tile(jnp.concatenate([cos, cos], axis=1), (1, reps)), jnp.tile(jnp.concatenate([-sin, sin], axis=1), (1, reps))


def _projection(x2d, pos, seq, g_attn, w_proj):
    n = x2d.shape[0]
    tm = min(Q_BLOCK, n)
    cs, sn = _rope_tables(pos)
    if seq >= tm:
        tab_map = lambda i: (i % (seq // tm), 0)
    else:
        cs, sn = jnp.tile(cs, (tm // seq, 1)), jnp.tile(sn, (tm // seq, 1))
        tab_map = lambda i: (0, 0)
    widths = (A_WIDTH, 2 * A_WIDTH, B_WIDTH, B_WIDTH, 4 * B_KV_WIDTH, 2 * B_KV_WIDTH, LANES, D_MODEL, D_MODEL)
    row = lambda i: (i, 0)
    fixed = lambda i: (0, 0)
    return pl.pallas_call(
        _proj_kernel,
        grid=(n // tm,),
        in_specs=[pl.BlockSpec((tm, D_MODEL), row), pl.BlockSpec((1, D_MODEL), fixed),
                  pl.BlockSpec((D_MODEL, N_PROJ), fixed), pl.BlockSpec((tm, LANES), tab_map),
                  pl.BlockSpec((tm, LANES), tab_map)],
        out_specs=[pl.BlockSpec((tm, w), row) for w in widths],
        out_shape=[jax.ShapeDtypeStruct((n, w), F32) for w in widths],
        compiler_params=_params(("parallel",)),
    )(x2d, g_attn.reshape(1, D_MODEL), w_proj, cs, sn)


def _rms_matmul_kernel(x_ref, g_ref, w_ref, o_ref):
    o_ref[...] = jnp.dot(_rms(x_ref[...], g_ref[...]).astype(BF16), w_ref[...], preferred_element_type=F32)


def _rms_matmul(x2d, g, w_bf16):
    n, d = x2d.shape
    m = w_bf16.shape[1]
    tm = min(Q_BLOCK, n)
    return pl.pallas_call(
        _rms_matmul_kernel,
        grid=(n // tm,),
        in_specs=[pl.BlockSpec((tm, d), lambda i: (i, 0)), pl.BlockSpec((1, d), lambda i: (0, 0)),
                  pl.BlockSpec((d, m), lambda i: (0, 0))],
        out_specs=pl.BlockSpec((tm, m), lambda i: (i, 0)),
        out_shape=jax.ShapeDtypeStruct((n, m), F32),
        compiler_params=_params(("parallel",)),
    )(x2d, g.reshape(1, d), w_bf16)


def _moba_prompt_kernel(q_ref, k_ref, v_ref, o_ref, *, seq):
    nb = seq // A_BLOCK
    k = k_ref[0]
    kb = k.astype(BF16)
    vb = v_ref[0].astype(BF16)
    lane = _lane_iota((1, LANES))
    nb_rows = -(-nb // SUBLANES) * SUBLANES
    kmean = jnp.concatenate(
        [jnp.mean(k[j * A_BLOCK:(j + 1) * A_BLOCK], axis=0, keepdims=True) for j in range(nb)]
        + [jnp.zeros((nb_rows - nb, LANES), F32)] * (nb_rows > nb), axis=0)
    blk_t = _row_iota((nb_rows, Q_BLOCK))
    expand = jnp.where(_row_iota((LANES, seq)) == _lane_iota((LANES, seq)) // A_BLOCK, 1.0, 0.0).astype(BF16)
    own_bias = _mask_bias(_lane_iota((Q_BLOCK, A_BLOCK)) <= _row_iota((Q_BLOCK, A_BLOCK)))
    scale = HEAD_DIM ** -0.5
    for i in range(nb):
        qi = q_ref[0, i * Q_BLOCK:(i + 1) * Q_BLOCK, :]
        heads = LANES // HEAD_DIM
        n_keys = (i + 1) * A_BLOCK
        qhs, biases = [], []
        for hh in range(heads):
            qh = jnp.where((lane // HEAD_DIM) == hh, qi, 0.0)
            bias = [own_bias]
            if i > A_TOPK:
                gate = _dot_t(kmean[0:nb_rows], qh, HIGHEST)
                gate = jnp.where(blk_t < i, gate, -jnp.inf)
                sel = _rows_to_lanes(jnp.where(blk_t < i, _topk_mask_t(gate, A_TOPK, blk_t.astype(F32)), 0.0))
                chosen = jnp.dot(sel.astype(BF16), expand[:, 0:i * A_BLOCK], preferred_element_type=F32) > 0.5
                bias = [_mask_bias(chosen)] + bias
            elif i > 0:
                bias = [jnp.zeros((Q_BLOCK, i * A_BLOCK), F32)] + bias
            qhs.append(qh * scale)
            biases.append(jnp.concatenate(bias, axis=1))
        s = _dot_t(jnp.concatenate(qhs, axis=0).astype(BF16), kb[0:n_keys]).reshape(heads, Q_BLOCK, n_keys)
        p, l = _softmax_biased(s, jnp.stack(biases))
        o = jnp.dot(p.reshape(heads * Q_BLOCK, n_keys).astype(BF16), vb[0:n_keys], preferred_element_type=F32)
        o = o.reshape(heads, Q_BLOCK, LANES) * _safe_inv(l)
        o_ref[0, i * Q_BLOCK:(i + 1) * Q_BLOCK, :] = jnp.where((lane // HEAD_DIM) == 0, o[0], o[1])


def _moba_prompt(aq, akv):
    b, seq, _ = aq.shape
    hp = A_WIDTH // LANES
    return pl.pallas_call(
        functools.partial(_moba_prompt_kernel, seq=seq),
        grid=(b, hp),
        in_specs=[pl.BlockSpec((1, seq, LANES), lambda i, j: (i, 0, j)),
                  pl.BlockSpec((1, seq, LANES), lambda i, j: (i, 0, j)),
                  pl.BlockSpec((1, seq, LANES), lambda i, j: (i, 0, hp + j))],
        out_specs=pl.BlockSpec((1, seq, LANES), lambda i, j: (i, 0, j)),
        out_shape=jax.ShapeDtypeStruct((b, seq, A_WIDTH), F32),
        compiler_params=_params(("parallel", "parallel")),
    )(aq, akv, akv)


def _compress_rows(xk_ref, xv_ref, n_rows, pe_ref, wa_ref, wb_ref, w2_ref):
    acc_a = jnp.zeros((n_rows, 2 * LANES), F32)
    acc_b = jnp.zeros((n_rows, 2 * LANES), F32)
    for tt in range(CMP_STRIDE):
        rows_tt = pl.ds(tt, n_rows, stride=CMP_STRIDE)
        xt = jnp.concatenate([xk_ref[rows_tt, :], xv_ref[rows_tt, :]], axis=1)
        acc_a += jnp.dot((xt + pe_ref[tt:tt + 1, :]).astype(BF16), wa_ref[tt], preferred_element_type=F32)
        acc_b += jnp.dot((xt + pe_ref[CMP_STRIDE + tt:CMP_STRIDE + tt + 1, :]).astype(BF16), wb_ref[tt],
                         preferred_element_type=F32)
    hidden = acc_a + pltpu.roll(acc_b, n_rows - 1, 0)
    out = jnp.dot(jax.nn.gelu(hidden).astype(BF16), w2_ref[...], preferred_element_type=F32)
    return jnp.where(_row_iota(out.shape) < n_rows - 1, out, 0.0)


def _compress_prompt_kernel(xk_ref, xv_ref, pe_ref, wa_ref, wb_ref, w2_ref, o_ref, *, seq):
    o_ref[0] = _compress_rows(xk_ref.at[0], xv_ref.at[0], seq // CMP_STRIDE, pe_ref, wa_ref, wb_ref, w2_ref)


def _compress_weights(pe_cmp, w_ck1, w_ck2, w_cv1, w_cv2):
    def diag(mk, mv):
        z = jnp.zeros_like(mk)
        rows = [[mk, z, z, z], [z, mk, z, z], [z, z, mv, z], [z, z, z, mv]]
        return jnp.concatenate([jnp.concatenate(r, axis=-1) for r in rows], axis=-2)

    k1 = w_ck1.reshape(CMP_LEN, HEAD_DIM, CMP_HIDDEN)
    v1 = w_cv1.reshape(CMP_LEN, HEAD_DIM, CMP_HIDDEN)
    w1 = diag(k1, v1).astype(BF16)
    return jnp.tile(pe_cmp, (1, 4)), w1[:CMP_STRIDE], w1[CMP_STRIDE:], diag(w_ck2, w_cv2).astype(BF16)


def _compress_prompt(bkv, cw):
    b, seq, _ = bkv.shape
    n_rows = seq // CMP_STRIDE
    pe, wa, wb, w2 = cw
    full = lambda *s: pl.BlockSpec(s, lambda i: (0,) * len(s))
    return pl.pallas_call(
        functools.partial(_compress_prompt_kernel, seq=seq),
        grid=(b,),
        in_specs=[pl.BlockSpec((1, seq, LANES), lambda i: (i, 0, 0)), pl.BlockSpec((1, seq, LANES), lambda i: (i, 0, 1)),
                  full(CMP_LEN, 2 * LANES), full(CMP_STRIDE, 2 * LANES, 2 * LANES),
                  full(CMP_STRIDE, 2 * LANES, 2 * LANES), full(2 * LANES, 2 * LANES)],
        out_specs=pl.BlockSpec((1, n_rows, 2 * LANES), lambda i: (i, 0, 0)),
        out_shape=jax.ShapeDtypeStruct((b, n_rows, 2 * LANES), F32),
        compiler_params=_params(("parallel",)),
    )(bkv, bkv, pe, wa, wb, w2)


def _overlap_matrix(n_cmp, n_sel, rows, cols):
    c_start = np.arange(n_cmp)[:, None] * CMP_STRIDE
    s_start = np.arange(n_sel)[None, :] * SEL_BLOCK
    ov = np.clip(np.minimum(c_start + CMP_LEN, s_start + SEL_BLOCK) - np.maximum(c_start, s_start), 0, None)
    out = np.zeros((rows, cols), np.float32)
    out[:n_cmp, :n_sel] = ov / CMP_STRIDE
    return jnp.asarray(out)


def _group_halves(x, g):
    xg = jnp.where((_lane_iota((1, LANES)) // HEAD_DIM) == g, x, 0.0)
    return xg + pltpu.roll(xg, HEAD_DIM, 1)


def _gate_column(gates, head, branch):
    lane = _lane_iota((1, LANES))
    return _sigmoid(jnp.sum(jnp.where(lane == head * 3 + branch, gates, 0.0), axis=-1, keepdims=True))


def _nsa_select_kernel(q_ref, gates_ref, cmp_ref, ovt_ref, ocmp_ref, sel_ref, *, seq):
    g = pl.program_id(1)
    i = pl.program_id(2)
    n_cmp = (seq - CMP_LEN) // CMP_STRIDE + 1
    rows = B_GROUP * Q_BLOCK
    scale = HEAD_DIM ** -0.5
    lane = _lane_iota((1, LANES))
    both_halves = lambda x: _group_halves(x, g)

    parts = []
    for j in range(B_GROUP):
        x = q_ref[0, :, (j // 2) * LANES:(j // 2 + 1) * LANES]
        parts.append(jnp.where((lane // HEAD_DIM) == (j % 2), x, 0.0) * scale)
    q4 = jnp.concatenate(parts, axis=0).astype(BF16)
    qpos = i * Q_BLOCK + _row_iota((Q_BLOCK, 1))

    kc = both_halves(cmp_ref[0, :, 0:LANES]).astype(BF16)
    vc = both_halves(cmp_ref[0, :, LANES:2 * LANES]).astype(BF16)
    n_tok = kc.shape[0]
    tok = _lane_iota((1, n_tok))
    cmask = (tok < n_cmp) & (tok * CMP_STRIDE + (CMP_LEN - 1) <= qpos)
    s = _dot_t(q4, kc).reshape(B_GROUP, Q_BLOCK, n_tok)
    p, l = _softmax_rows(s, cmask[None])
    p_cmp = p * _safe_inv(l)
    o_cmp = jnp.dot(p_cmp.reshape(rows, n_tok).astype(BF16), vc, preferred_element_type=F32)
    n_sel = -(-seq // SEL_BLOCK)
    sel_rows = -(-n_sel // SUBLANES) * SUBLANES
    imp = _dot_t(ovt_ref[0:sel_rows, :], jnp.sum(p_cmp, axis=0), HIGHEST)
    blk = _row_iota((sel_rows, Q_BLOCK))
    cur = (i * Q_BLOCK + _lane_iota((1, Q_BLOCK))) // SEL_BLOCK
    forced = (blk == 0) | (blk == cur) | (blk == cur - 1)
    score = jnp.where(blk <= cur, jnp.where(forced, FORCE_SCORE, imp), -jnp.inf)
    sel = _topk_mask_t(score, min(SEL_TOPN, n_sel), blk.astype(F32))
    sel_ref[0, 0] = _rows_to_lanes(jnp.where(blk <= cur, sel, 0.0))

    gates = gates_ref[0]
    outs = [_gate_column(gates, g * B_GROUP + j, 0) * o_cmp[j * Q_BLOCK:(j + 1) * Q_BLOCK] for j in range(B_GROUP)]
    left = (lane // HEAD_DIM) == 0
    ocmp_ref[0] = jnp.concatenate([jnp.where(left, outs[0], outs[1]), jnp.where(left, outs[2], outs[3])], axis=1)


def _nsa_attend_kernel(qr_ref, ksel_ref, vsel_ref, kwin_ref, vwin_ref, sel_ref, gates_ref, ocmp_ref, o_ref, *, seq):
    chunk = pl.program_id(1)
    g = chunk // (B_GROUP // 2)
    nq = seq // Q_BLOCK
    scale = HEAD_DIM ** -0.5
    lane = _lane_iota((1, LANES))
    ks = _group_halves(ksel_ref[0], g).astype(BF16)
    vs = _group_halves(vsel_ref[0], g).astype(BF16)
    kw = _group_halves(kwin_ref[0], g).astype(BF16)
    vw = _group_halves(vwin_ref[0], g).astype(BF16)
    expand = jnp.where(_row_iota((LANES, seq)) == _lane_iota((LANES, seq)) // SEL_BLOCK, 1.0, 0.0).astype(BF16)
    for i in range(nq):
        rows = slice(i * Q_BLOCK, (i + 1) * Q_BLOCK)
        n_keys = (i + 1) * Q_BLOCK
        w0 = max(i * Q_BLOCK - WINDOW, 0)
        qpos = i * Q_BLOCK + _row_iota((Q_BLOCK, 1))
        chosen = jnp.dot(sel_ref[0, 0, rows, :].astype(BF16), expand[:, 0:n_keys], preferred_element_type=F32) > 0.5
        sbias = _mask_bias(chosen & (_lane_iota((1, n_keys)) <= qpos))
        wpos = w0 + _lane_iota((1, n_keys - w0))
        wbias = _mask_bias((wpos <= qpos) & (wpos > qpos - WINDOW))
        qi = qr_ref[0, rows, :]
        gates = gates_ref[0, rows, :]
        heads = LANES // HEAD_DIM
        q2 = jnp.concatenate([jnp.where((lane // HEAD_DIM) == hh, qi, 0.0) * scale for hh in range(heads)],
                             axis=0).astype(BF16)

        def attend(k, v, bias):
            s = _dot_t(q2, k).reshape(heads, Q_BLOCK, k.shape[0])
            p, l = _softmax_biased(s, bias[None])
            o = jnp.dot(p.reshape(heads * Q_BLOCK, k.shape[0]).astype(BF16), v, preferred_element_type=F32)
            return o.reshape(heads, Q_BLOCK, LANES) * _safe_inv(l)

        o_sel = attend(ks[0:n_keys], vs[0:n_keys], sbias)
        o_win = attend(kw[w0:n_keys], vw[w0:n_keys], wbias)
        outs = []
        for hh in range(heads):
            head = chunk * heads + hh
            outs.append(_gate_column(gates, head, 1) * o_sel[hh] + _gate_column(gates, head, 2) * o_win[hh])
        o_ref[0, rows, :] = ocmp_ref[0, rows, :] + jnp.where((lane // HEAD_DIM) == 0, outs[0], outs[1])


def _nsa_prompt(bq, bqr, gates, bkv, bwin, cmp_tok):
    b, seq, _ = bq.shape
    nq = seq // Q_BLOCK
    n_tok = cmp_tok.shape[1]
    n_cmp = (seq - CMP_LEN) // CMP_STRIDE + 1
    n_sel = -(-seq // SEL_BLOCK)
    assert n_sel <= LANES
    overlap_t = _overlap_matrix(n_cmp, n_sel, n_tok, LANES).T
    ocmp, sel = pl.pallas_call(
        functools.partial(_nsa_select_kernel, seq=seq),
        grid=(b, B_KV_HEADS, nq),
        in_specs=[pl.BlockSpec((1, Q_BLOCK, 2 * LANES), lambda i, g, t: (i, t, g)),
                  pl.BlockSpec((1, Q_BLOCK, LANES), lambda i, g, t: (i, t, 0)),
                  pl.BlockSpec((1, n_tok, 2 * LANES), lambda i, g, t: (i, 0, 0)),
                  pl.BlockSpec((LANES, n_tok), lambda i, g, t: (0, 0))],
        out_specs=[pl.BlockSpec((1, Q_BLOCK, 2 * LANES), lambda i, g, t: (i, t, g)),
                   pl.BlockSpec((1, 1, Q_BLOCK, LANES), lambda i, g, t: (i, g, t, 0))],
        out_shape=[jax.ShapeDtypeStruct((b, seq, B_WIDTH), F32),
                   jax.ShapeDtypeStruct((b, B_KV_HEADS, seq, LANES), F32)],
        compiler_params=_params(("parallel", "parallel", "parallel")),
    )(bq, gates, cmp_tok, overlap_t)
    chunks = B_WIDTH // LANES
    per_group = chunks // B_KV_HEADS
    own = pl.BlockSpec((1, seq, LANES), lambda i, c: (i, 0, c))
    col = lambda k: pl.BlockSpec((1, seq, LANES), lambda i, c: (i, 0, k))
    return pl.pallas_call(
        functools.partial(_nsa_attend_kernel, seq=seq),
        grid=(b, chunks),
        in_specs=[own, col(2), col(3), col(0), col(1),
                  pl.BlockSpec((1, 1, seq, LANES), lambda i, c: (i, c // per_group, 0, 0)),
                  pl.BlockSpec((1, seq, LANES), lambda i, c: (i, 0, 0)), own],
        out_specs=own,
        out_shape=jax.ShapeDtypeStruct((b, seq, B_WIDTH), F32),
        compiler_params=_params(("parallel", "parallel")),
    )(bqr, bkv, bkv, bwin, bwin, sel, gates, ocmp)


def _merge_kernel(x_ref, oa_ref, ob_ref, ga_ref, gb_ref, pa_ref, pb_ref, wo_ref, h_ref):
    ya = jnp.dot(oa_ref[...].astype(BF16), pa_ref[...], preferred_element_type=F32)
    yb = jnp.dot(ob_ref[...].astype(BF16), pb_ref[...], preferred_element_type=F32)
    mixed = _sigmoid(ga_ref[...]) * ya + _sigmoid(gb_ref[...]) * yb
    h_ref[...] = x_ref[...] + jnp.dot(mixed.astype(BF16), wo_ref[...], preferred_element_type=F32)


def _merge(x2d, oa, ob, ga, gb, pa, pb, wo):
    n = x2d.shape[0]
    tm = min(Q_BLOCK, n)
    row = lambda w: pl.BlockSpec((tm, w), lambda i: (i, 0))
    full = lambda a: pl.BlockSpec(a.shape, lambda i: (0, 0))
    return pl.pallas_call(
        _merge_kernel,
        grid=(n // tm,),
        in_specs=[row(D_MODEL), row(A_WIDTH), row(B_WIDTH), row(D_MODEL), row(D_MODEL), full(pa), full(pb), full(wo)],
        out_specs=row(D_MODEL),
        out_shape=jax.ShapeDtypeStruct((n, D_MODEL), F32),
        compiler_params=_params(("parallel",)),
    )(x2d, oa, ob, ga, gb, pa, pb, wo)


def _cross_kernel(h_ref, mem_ref, g_ref, wq_ref, wo_ref, o_ref):
    h = h_ref[0]
    q = jnp.dot(_rms(h, g_ref[...]).astype(BF16), wq_ref[...], preferred_element_type=F32)
    scale = C_HEAD_DIM ** -0.5
    outs = []
    for hd in range(C_HEADS):
        c = slice(hd * C_HEAD_DIM, (hd + 1) * C_HEAD_DIM)
        kh = mem_ref[0, :, c].astype(BF16)
        vh = mem_ref[0, :, C_WIDTH + hd * C_HEAD_DIM:C_WIDTH + (hd + 1) * C_HEAD_DIM].astype(BF16)
        s = _dot_t(q[:, c].astype(BF16), kh) * scale
        m = jnp.max(s, axis=-1, keepdims=True)
        p = jnp.exp(s - m)
        o = jnp.dot(p.astype(BF16), vh, preferred_element_type=F32)
        outs.append(o / jnp.sum(p, axis=-1, keepdims=True))
    att = jnp.concatenate(outs, axis=1).astype(BF16)
    o_ref[0] = h + jnp.dot(att, wo_ref[...], preferred_element_type=F32)


def _cross(h3, mem_kv, g_cross, wq, wo):
    b, t, _ = h3.shape
    tq = min(Q_BLOCK, t)
    mlen = mem_kv.shape[1]
    return pl.pallas_call(
        _cross_kernel,
        grid=(b, t // tq),
        in_specs=[pl.BlockSpec((1, tq, D_MODEL), lambda i, j: (i, j, 0)),
                  pl.BlockSpec((1, mlen, 2 * C_WIDTH), lambda i, j: (i, 0, 0)),
                  pl.BlockSpec((1, D_MODEL), lambda i, j: (0, 0)),
                  pl.BlockSpec(wq.shape, lambda i, j: (0, 0)), pl.BlockSpec(wo.shape, lambda i, j: (0, 0))],
        out_specs=pl.BlockSpec((1, tq, D_MODEL), lambda i, j: (i, j, 0)),
        out_shape=jax.ShapeDtypeStruct((b, t, D_MODEL), F32),
        compiler_params=_params(("parallel", "parallel")),
    )(h3, mem_kv, g_cross.reshape(1, D_MODEL), wq, wo)


PEER_SELECT_ROWS = 1024


def _peer_candidate_ids():
    ids = -np.ones((7 * SUBLANES,), np.float32)
    layout = [(0, 0, 16), (1, 16, 8), (2, 24, 5), (3, 32, 4), (4, 36, 3), (5, 40, 2), (6, 42, 2), (7, 44, 2)]
    layout += [(a, 40 + a, 1) for a in range(8, 16)]
    for a, row, n in layout:
        assert (a + 1) * n <= P_TOPK < (a + 1) * (n + 1)
        ids[row:row + n] = a * P_TOPK + np.arange(n)
    return jnp.asarray(np.tile(ids[:, None], (1, LANES)))


def _peer_select_kernel(h_ref, g_ref, wq_ref, k1_ref, k2_ref, cid_ref, z_ref, eid_ref, gate_ref, s_ref, e_scr, w_scr,
                        *, tm):
    z = _rms(h_ref[...], g_ref[...])
    z_ref[...] = z
    zb = z.astype(BF16)
    half = P_QDIM // 2
    k1 = k1_ref[...]
    k2 = k2_ref[...]
    for hd in range(P_HEADS):
        q = jnp.dot(zb, wq_ref[:, hd * P_QDIM:(hd + 1) * P_QDIM], preferred_element_type=F32)
        s1 = _dot_t(k1, q[:, 0:half], HIGHEST)
        s2 = _dot_t(k2, q[:, half:P_QDIM], HIGHEST)
        for tb in range(tm // LANES):
            s_ref[hd, tb, 0] = s1[:, tb * LANES:(tb + 1) * LANES]
            s_ref[hd, tb, 1] = s2[:, tb * LANES:(tb + 1) * LANES]

    key_f = _row_iota((P_NKEYS, LANES)).astype(F32)
    cand_valid = cid_ref[...] >= 0.0
    cand_f = jnp.where(cand_valid, cid_ref[...], float(P_TOPK * P_TOPK))
    r8 = _row_iota((SUBLANES, LANES))

    def pair_rows(x1, x2s, combine):
        x1s = jnp.concatenate(x1, axis=0)
        x2_8 = x2s[0:SUBLANES]
        shift = lambda k: pltpu.roll(x2_8, k, 0)
        return jnp.concatenate([
            combine(x1[0], x2s), combine(x1[1], x2_8), combine(x1[2], x2_8),
            jnp.where(r8 < 4, combine(x1[3], x2_8), combine(x1[4], shift(4))),
            jnp.where(r8 < 2, combine(x1[5], x2_8),
                      jnp.where(r8 < 4, combine(x1[6], shift(2)), combine(x1[7], shift(4)))),
            combine(x1s[SUBLANES:2 * SUBLANES], x2s[0:1])], axis=0)

    def take_top(x, ids_f, limit):
        mx = jnp.max(x, axis=0, keepdims=True)
        first = jnp.min(jnp.where(x == mx, ids_f, limit), axis=0, keepdims=True)
        hit = ids_f == first
        return mx, first, hit, jnp.where(hit, -jnp.inf, x)

    def select(x1, x2):
        v1, i1, v2, i2 = [], [], [], []
        for _ in range(P_TOPK):
            m, a, _, x1 = take_top(x1, key_f, float(P_NKEYS))
            v1.append(m)
            i1.append(a)
            m, a, _, x2 = take_top(x2, key_f, float(P_NKEYS))
            v2.append(m)
            i2.append(a)
        v2s = jnp.concatenate(v2, axis=0)
        i2s = jnp.concatenate(i2, axis=0)
        cand = jnp.where(cand_valid, pair_rows(v1, v2s, lambda a, b: a + b), -jnp.inf)
        cid = pair_rows(i1, i2s, lambda a, b: a * float(P_NKEYS) + b)
        es, ss = [], []
        for _ in range(P_TOPK):
            mx, _, hit, cand = take_top(cand, cand_f, float(P_TOPK * P_TOPK))
            es.append(jnp.max(jnp.where(hit, cid, -1.0), axis=0, keepdims=True))
            ss.append(mx)
        ex = jnp.exp(jnp.concatenate(ss, axis=0) - ss[0])
        return jnp.concatenate(es, axis=0), ex / jnp.sum(ex, axis=0, keepdims=True)

    def step(hd, carry):
        rows = pl.ds(pl.multiple_of(hd * P_TOPK, P_TOPK), P_TOPK)
        for tb in range(tm // LANES):
            e16, w16 = select(s_ref[hd, tb, 0], s_ref[hd, tb, 1])
            e_scr[tb, rows, :] = e16
            w_scr[tb, rows, :] = w16
        return carry

    lax.fori_loop(0, P_HEADS, step, 0)
    for tb in range(tm // LANES):
        eid_ref[tb * LANES:(tb + 1) * LANES, :] = jnp.transpose(e_scr[tb]).astype(jnp.int32)
        gate_ref[tb * LANES:(tb + 1) * LANES, :] = jnp.transpose(w_scr[tb])


def _peer_select(h2d, g_ffn, wq, k1, k2):
    n = h2d.shape[0]
    tm = min(PEER_SELECT_ROWS, n)
    assert tm % LANES == 0 and n % tm == 0
    cand_ids = _peer_candidate_ids()
    row = lambda: pl.BlockSpec((tm, D_MODEL), lambda i: (i, 0))
    pick = lambda: pl.BlockSpec((tm, LANES), lambda i: (i, 0))
    full = lambda a: pl.BlockSpec(a.shape, lambda i: (0, 0))
    return pl.pallas_call(
        functools.partial(_peer_select_kernel, tm=tm),
        grid=(n // tm,),
        in_specs=[row(), pl.BlockSpec((1, D_MODEL), lambda i: (0, 0)), full(wq), full(k1), full(k2), full(cand_ids)],
        out_specs=[row(), pick(), pick()],
        out_shape=[jax.ShapeDtypeStruct((n, D_MODEL), F32), jax.ShapeDtypeStruct((n, LANES), jnp.int32),
                   jax.ShapeDtypeStruct((n, LANES), F32)],
        scratch_shapes=[pltpu.VMEM((P_HEADS, tm // LANES, 2, P_NKEYS, LANES), F32),
                        pltpu.VMEM((tm // LANES, P_PICKS, LANES), F32), pltpu.VMEM((tm // LANES, P_PICKS, LANES), F32)],
        compiler_params=_params(("parallel",)),
    )(h2d, g_ffn.reshape(1, D_MODEL), wq, k1, k2, cand_ids)


PEER_TOKENS = 8
ROW_TILES = D_MODEL // LANES


def _peer_gather_kernel(eid_ref, h_ref, z_ref, gate_ref, gfin_ref, uv_hbm, y_ref, buf, sem):
    j = pl.program_id(0)
    n_tiles = pl.num_programs(0) - 1
    rows = PEER_TOKENS * P_PICKS

    for to_slot in range(2):
        @pl.when((j < n_tiles) & (j % 2 == to_slot))
        def _(to_slot=to_slot):
            for r in range(rows):
                e = eid_ref[r // P_PICKS, r % P_PICKS]
                pltpu.make_async_copy(uv_hbm.at[e], buf.at[to_slot, pl.ds(r, 1)], sem.at[to_slot]).start()

    @pl.when(j > 0)
    def _():
        slot = (j - 1) % 2
        pltpu.make_async_copy(uv_hbm.at[pl.ds(0, rows), 0], buf.at[slot], sem.at[slot]).wait()
        gate_t = jnp.transpose(jnp.concatenate(
            [gate_ref[...], jnp.zeros((LANES - PEER_TOKENS, P_PICKS), F32)], axis=0))
        for p in range(PEER_TOKENS):
            picks = pl.ds(p * P_PICKS, P_PICKS)
            acc = jnp.zeros((P_PICKS, LANES), F32)
            for s in range(ROW_TILES):
                acc += buf[slot, picks, s * LANES:(s + 1) * LANES] * z_ref[p:p + 1, s * LANES:(s + 1) * LANES]
            act = jax.nn.gelu(jnp.sum(acc, axis=-1, keepdims=True))
            coef = jnp.broadcast_to(gate_t[:, p:p + 1] * act, (P_PICKS, LANES))
            outs = []
            for s in range(ROW_TILES):
                v_s = buf[slot, picks, D_MODEL + s * LANES:D_MODEL + (s + 1) * LANES]
                outs.append(jnp.sum(coef * v_s, axis=0, keepdims=True))
            y_ref[p:p + 1, :] = h_ref[p:p + 1, :] + jnp.concatenate(outs, axis=1)
        y_ref[...] = _rms(y_ref[...], gfin_ref[...])


def _peer_gather(eid, h2d, z, gate, g_final, uv):
    n = h2d.shape[0]
    tiles = n // PEER_TOKENS
    ahead = lambda j: (jnp.minimum(j, tiles - 1), 0)
    behind = lambda j: (jnp.maximum(j - 1, 0), 0)
    row = lambda w: pl.BlockSpec((PEER_TOKENS, w), behind)
    return pl.pallas_call(
        _peer_gather_kernel,
        grid=(tiles + 1,),
        in_specs=[pl.BlockSpec((PEER_TOKENS, P_PICKS), ahead, memory_space=pltpu.SMEM),
                  row(D_MODEL), row(D_MODEL), row(P_PICKS), pl.BlockSpec((1, D_MODEL), lambda j: (0, 0)),
                  pl.BlockSpec(memory_space=pl.ANY)],
        out_specs=row(D_MODEL),
        out_shape=jax.ShapeDtypeStruct((n, D_MODEL), F32),
        scratch_shapes=[pltpu.VMEM((2, PEER_TOKENS * P_PICKS, 2 * D_MODEL), F32),
                        pltpu.SemaphoreType.DMA((2,))],
        compiler_params=_params(("arbitrary",)),
    )(eid, h2d, z, gate, g_final.reshape(1, D_MODEL), uv)


PAGES_PER_STEP = 16
SAMPLE_ROWS = LANES


def _page_specs(rows, row_block):
    def spec(k):
        return pl.BlockSpec((1, rows, PAGE_SIZE), lambda b, c, pt: (pt[b, c * PAGES_PER_STEP + k], row_block, 0))
    return [spec(k) for k in range(PAGES_PER_STEP)]


def _online_step(s, mask, pv_fn, m_ref, l_ref, acc_ref):
    s = jnp.where(mask, s, NEG_BIG)
    m_old = m_ref[...]
    m_new = jnp.maximum(m_old, jnp.max(s, axis=-1, keepdims=True))
    p = jnp.where(mask, jnp.exp(s - m_new), 0.0)
    alpha = jnp.exp(m_old - m_new)
    l_ref[...] = alpha * l_ref[...] + jnp.sum(p, axis=-1, keepdims=True)
    acc_ref[...] = alpha * acc_ref[...] + pv_fn(p.astype(BF16))
    m_ref[...] = m_new


def _online_update(s, mask, v, m_ref, l_ref, acc_ref):
    _online_step(s, mask, lambda p: jnp.dot(p, v, preferred_element_type=F32), m_ref, l_ref, acc_ref)


def _online_update_t(s, mask, v_t, m_ref, l_ref, acc_ref):
    _online_step(s, mask, lambda p: jnp.transpose(_dot_t(v_t, p)), m_ref, l_ref, acc_ref)


def _moba_kmean_kernel(pt_ref, *refs):
    pages, o_ref = refs[:PAGES_PER_STEP], refs[PAGES_PER_STEP]
    c = pl.program_id(1)

    @pl.when(c == 0)
    def _():
        o_ref[...] = jnp.zeros(o_ref.shape, F32)

    n_keys = PAGES_PER_STEP * PAGE_SIZE
    n_lanes = o_ref.shape[2]
    k_t = jnp.concatenate([p[0] for p in pages], axis=1)
    blk = c * (n_keys // A_BLOCK) + _row_iota((n_keys, n_lanes)) // A_BLOCK
    avg = jnp.where(_lane_iota((n_keys, n_lanes)) == blk, 1.0 / A_BLOCK, 0.0).astype(BF16)
    k_hi = k_t.astype(BF16)
    k_lo = (k_t - k_hi.astype(F32)).astype(BF16)
    o_ref[0] += (jnp.dot(k_hi, avg, preferred_element_type=F32) + jnp.dot(k_lo, avg, preferred_element_type=F32))


def _moba_kmean(pool_t, page_table, n_lanes):
    b, n_pages = page_table.shape
    return pl.pallas_call(
        _moba_kmean_kernel,
        grid_spec=pltpu.PrefetchScalarGridSpec(
            num_scalar_prefetch=1, grid=(b, n_pages // PAGES_PER_STEP),
            in_specs=_page_specs(A_WIDTH, 0),
            out_specs=pl.BlockSpec((1, A_WIDTH, n_lanes), lambda i, c, pt: (i, 0, 0))),
        out_shape=jax.ShapeDtypeStruct((b, A_WIDTH, n_lanes), F32),
        compiler_params=_params(("parallel", "arbitrary")),
    )(page_table, *([pool_t] * PAGES_PER_STEP))


def _moba_sample_kernel(pt_ref, *refs, past, t_new):
    pages = refs[:PAGES_PER_STEP]
    kmean_ref, q_ref, new_ref, o_ref, sel_ref, m_ref, l_ref, acc_ref = refs[PAGES_PER_STEP:]
    c = pl.program_id(1)
    rows = A_HEADS * t_new
    n_lanes = sel_ref.shape[1]
    cur = past // A_BLOCK
    q = q_ref[0]
    qb = (q * HEAD_DIM ** -0.5).astype(BF16)

    @pl.when(c == 0)
    def _():
        lane = _lane_iota((1, n_lanes))
        gate = jnp.dot(q, kmean_ref[0], precision=HIGHEST, preferred_element_type=F32)
        gate = jnp.where(lane < cur, gate, -jnp.inf)
        sel = _topk_mask(gate, min(A_TOPK, cur + 1), lane.astype(F32))
        sel_ref[...] = jnp.where(lane < cur, sel, 0.0)
        m_ref[...] = jnp.full(m_ref.shape, NEG_BIG, F32)
        l_ref[...] = jnp.zeros(l_ref.shape, F32)
        acc_ref[...] = jnp.zeros(acc_ref.shape, F32)

    n_keys = PAGES_PER_STEP * PAGE_SIZE
    k_t = jnp.concatenate([p[0, 0:A_WIDTH, :] for p in pages], axis=1).astype(BF16)
    v_t = jnp.concatenate([p[0, A_WIDTH:2 * A_WIDTH, :] for p in pages], axis=1).astype(BF16)
    blk = c * (n_keys // A_BLOCK) + _lane_iota((n_lanes, n_keys)) // A_BLOCK
    expand = jnp.where(_row_iota((n_lanes, n_keys)) == blk, 1.0, 0.0).astype(BF16)
    chosen = jnp.dot(sel_ref[...].astype(BF16), expand, preferred_element_type=F32) > 0.5
    _online_update_t(jnp.dot(qb, k_t, preferred_element_type=F32), chosen, v_t, m_ref, l_ref, acc_ref)

    @pl.when(c == pl.num_programs(1) - 1)
    def _():
        kn = new_ref[0, :, 0:A_WIDTH].astype(BF16)
        vn = new_ref[0, :, A_WIDTH:2 * A_WIDTH].astype(BF16)
        t_key = _lane_iota((SAMPLE_ROWS, kn.shape[0]))
        t_row = _row_iota((SAMPLE_ROWS, kn.shape[0])) % t_new
        _online_update(_dot_t(qb, kn), (t_key <= t_row) & (t_key < t_new), vn, m_ref, l_ref, acc_ref)
        own = (_lane_iota((SAMPLE_ROWS, A_WIDTH)) // HEAD_DIM) == (_row_iota((SAMPLE_ROWS, A_WIDTH)) // t_new)
        o_ref[0] = jnp.where(own, acc_ref[...] * _safe_inv(l_ref[...]), 0.0)[0:rows]


def _moba_sample(pool_t, page_table, kmean_t, q_rows, new_kv, past, t_new):
    b, n_pages = page_table.shape
    rows = A_HEADS * t_new
    n_lanes = kmean_t.shape[2]
    per_b = lambda s: pl.BlockSpec((1,) + s, lambda i, c, pt: (i, 0, 0))
    return pl.pallas_call(
        functools.partial(_moba_sample_kernel, past=past, t_new=t_new),
        grid_spec=pltpu.PrefetchScalarGridSpec(
            num_scalar_prefetch=1, grid=(b, n_pages // PAGES_PER_STEP),
            in_specs=_page_specs(2 * A_WIDTH, 0) + [per_b(kmean_t.shape[1:]), per_b(q_rows.shape[1:]),
                                                     per_b(new_kv.shape[1:])],
            out_specs=per_b((rows, A_WIDTH)),
            scratch_shapes=[pltpu.VMEM((SAMPLE_ROWS, n_lanes), F32), pltpu.VMEM((SAMPLE_ROWS, 1), F32),
                            pltpu.VMEM((SAMPLE_ROWS, 1), F32), pltpu.VMEM((SAMPLE_ROWS, A_WIDTH), F32)]),
        out_shape=jax.ShapeDtypeStruct((b, rows, A_WIDTH), F32),
        compiler_params=_params(("parallel", "arbitrary")),
    )(page_table, *([pool_t] * PAGES_PER_STEP), kmean_t, q_rows, new_kv)


def _compress_sample_kernel(pt_ref, *refs, past):
    pages = refs[:PAGES_PER_STEP]
    pe_ref, wa_ref, wb_ref, w2_ref, o_ref, xk_ref, xv_ref = refs[PAGES_PER_STEP:]
    c = pl.program_id(1)
    for k, page in enumerate(pages):
        start = pl.multiple_of((c * PAGES_PER_STEP + k) * PAGE_SIZE, PAGE_SIZE)
        xk_ref[pl.ds(start, PAGE_SIZE), :] = jnp.transpose(page[0, 0:LANES, :])
        xv_ref[pl.ds(start, PAGE_SIZE), :] = jnp.transpose(page[0, LANES:2 * LANES, :])

    @pl.when(c == pl.num_programs(1) - 1)
    def _():
        o_ref[0] = _compress_rows(xk_ref, xv_ref, past // CMP_STRIDE, pe_ref, wa_ref, wb_ref, w2_ref)


def _compress_sample(pool, page_table, cw, past):
    b, n_pages = page_table.shape
    n_rows = past // CMP_STRIDE
    pe, wa, wb, w2 = cw
    full = lambda a: pl.BlockSpec(a.shape, lambda i, c, pt: (0,) * a.ndim)
    return pl.pallas_call(
        functools.partial(_compress_sample_kernel, past=past),
        grid_spec=pltpu.PrefetchScalarGridSpec(
            num_scalar_prefetch=1, grid=(b, n_pages // PAGES_PER_STEP),
            in_specs=_page_specs(2 * LANES, 0) + [full(pe), full(wa), full(wb), full(w2)],
            out_specs=pl.BlockSpec((1, n_rows, 2 * LANES), lambda i, c, pt: (i, 0, 0)),
            scratch_shapes=[pltpu.VMEM((past, LANES), F32), pltpu.VMEM((past, LANES), F32)]),
        out_shape=jax.ShapeDtypeStruct((b, n_rows, 2 * LANES), F32),
        compiler_params=_params(("parallel", "arbitrary")),
    )(page_table, *([pool] * PAGES_PER_STEP), pe, wa, wb, w2)


def _nsa_sample_kernel(pt_ref, *refs, past, t_new):
    pages = refs[:PAGES_PER_STEP]
    (cmp_ref, q_ref, qr_ref, gates_ref, newsel_ref, win_ref, newwin_ref, ov_ref, o_ref,
     kv_ref, m_ref, l_ref, acc_ref) = refs[PAGES_PER_STEP:]
    c = pl.program_id(1)
    for k, page in enumerate(pages):
        kv_ref[c * PAGES_PER_STEP + k] = page[0]

    @pl.when(c == pl.num_programs(1) - 1)
    def _():
        rows = B_HEADS * t_new
        slab = B_KV_HEADS * t_new
        scale = HEAD_DIM ** -0.5
        qb = (q_ref[0] * scale).astype(BF16)
        qrb = (qr_ref[0] * scale).astype(BF16)
        pos = past + _row_iota((SAMPLE_ROWS, 1)) % t_new

        n_tok = cmp_ref.shape[1]
        n_cmp = n_tok - 1
        kc = cmp_ref[0, :, 0:LANES].astype(BF16)
        vc = cmp_ref[0, :, LANES:2 * LANES].astype(BF16)
        tok = _lane_iota((1, n_tok))
        p, l = _softmax_rows(_dot_t(qb, kc), (tok < n_cmp) & (tok * CMP_STRIDE + (CMP_LEN - 1) <= pos))
        p_cmp = p * _safe_inv(l)
        o_cmp = jnp.dot(p_cmp.astype(BF16), vc, preferred_element_type=F32)
        p_sum = sum(p_cmp[j * slab:(j + 1) * slab] for j in range(B_GROUP))
        imp = jnp.dot(p_sum, ov_ref[...], precision=HIGHEST, preferred_element_type=F32)
        n_lanes = ov_ref.shape[1]
        lane = _lane_iota((1, n_lanes))
        cur = past // SEL_BLOCK
        forced = (lane == 0) | (lane == cur) | (lane == cur - 1)
        score = jnp.where(lane <= cur, jnp.where(forced, FORCE_SCORE, imp), -jnp.inf)
        sel = _topk_mask(score, min(SEL_TOPN, cur + 1), lane.astype(F32))
        sel = jnp.where(lane <= cur, sel, 0.0)
        sel_b = sel.astype(BF16)

        m_ref[...] = jnp.full(m_ref.shape, NEG_BIG, F32)
        l_ref[...] = jnp.zeros(l_ref.shape, F32)
        acc_ref[...] = jnp.zeros(acc_ref.shape, F32)
        n_keys = PAGES_PER_STEP * PAGE_SIZE
        tile_rows = lambda a: jnp.concatenate([a] * (SAMPLE_ROWS // slab), axis=0)
        for ci in range(past // n_keys):
            pages_ci = range(ci * PAGES_PER_STEP, (ci + 1) * PAGES_PER_STEP)
            ks_t = jnp.concatenate([kv_ref[pg, 0:LANES, :] for pg in pages_ci], axis=1).astype(BF16)
            vs_t = jnp.concatenate([kv_ref[pg, LANES:2 * LANES, :] for pg in pages_ci], axis=1).astype(BF16)
            blk = (ci * n_keys + _lane_iota((n_lanes, n_keys))) // SEL_BLOCK
            expand = jnp.where(_row_iota((n_lanes, n_keys)) == blk, 1.0, 0.0).astype(BF16)
            chosen = jnp.dot(sel_b, expand, preferred_element_type=F32) > 0.5
            _online_update_t(jnp.dot(qrb, ks_t, preferred_element_type=F32), tile_rows(chosen), vs_t,
                             m_ref, l_ref, acc_ref)
        kn = newsel_ref[0, :, 0:LANES].astype(BF16)
        vn = newsel_ref[0, :, LANES:2 * LANES].astype(BF16)
        t_key = _lane_iota((SAMPLE_ROWS, kn.shape[0]))
        cur_chosen = tile_rows(jnp.sum(jnp.where(lane == cur, sel, 0.0), axis=-1, keepdims=True)) > 0.5
        _online_update(_dot_t(qrb, kn), cur_chosen & (past + t_key <= pos) & (t_key < t_new), vn,
                       m_ref, l_ref, acc_ref)
        o_sel = acc_ref[...] * _safe_inv(l_ref[...])

        n_win = win_ref.shape[1]
        kw = jnp.concatenate([win_ref[0, :, 0:LANES], newwin_ref[0, :, 0:LANES]], axis=0).astype(BF16)
        vw = jnp.concatenate([win_ref[0, :, LANES:2 * LANES], newwin_ref[0, :, LANES:2 * LANES]], axis=0).astype(BF16)
        wpos = past - n_win + _lane_iota((1, kw.shape[0]))
        p, l = _softmax_rows(_dot_t(qrb, kw), (wpos <= pos) & (wpos > pos - WINDOW))
        o_win = jnp.dot(p.astype(BF16), vw, preferred_element_type=F32) * _safe_inv(l)

        gt = _sigmoid(gates_ref[0])
        o = gt[:, 0:1] * o_cmp + gt[:, 1:2] * o_sel + gt[:, 2:3] * o_win
        shape = (SAMPLE_ROWS, LANES)
        own = (_lane_iota(shape) // HEAD_DIM) == ((_row_iota(shape) // t_new) % B_KV_HEADS)
        o_ref[0] = jnp.where(own, o, 0.0)[0:rows]


def _nsa_sample(pool_t, page_table, cmp_tok, q_rows, qr_rows, gate_rows, new_sel, win_state, new_win, past, t_new):
    b, n_pages = page_table.shape
    rows = B_HEADS * t_new
    n_tok = cmp_tok.shape[1]
    n_sel = past // SEL_BLOCK + 1
    n_lanes = -(-n_sel // LANES) * LANES
    overlap = _overlap_matrix(n_tok - 1, n_sel, n_tok, n_lanes)
    per_b = lambda a: pl.BlockSpec((1,) + a.shape[1:], lambda i, c, pt: (i, 0, 0))
    return pl.pallas_call(
        functools.partial(_nsa_sample_kernel, past=past, t_new=t_new),
        grid_spec=pltpu.PrefetchScalarGridSpec(
            num_scalar_prefetch=1, grid=(b, n_pages // PAGES_PER_STEP),
            in_specs=_page_specs(2 * LANES, 1) + [per_b(cmp_tok), per_b(q_rows), per_b(qr_rows), per_b(gate_rows),
                                                  per_b(new_sel), per_b(win_state), per_b(new_win),
                                                  pl.BlockSpec(overlap.shape, lambda i, c, pt: (0, 0))],
            out_specs=pl.BlockSpec((1, rows, LANES), lambda i, c, pt: (i, 0, 0)),
            scratch_shapes=[pltpu.VMEM((n_pages, 2 * LANES, PAGE_SIZE), F32), pltpu.VMEM((SAMPLE_ROWS, 1), F32),
                            pltpu.VMEM((SAMPLE_ROWS, 1), F32), pltpu.VMEM((SAMPLE_ROWS, LANES), F32)]),
        out_shape=jax.ShapeDtypeStruct((b, rows, LANES), F32),
        compiler_params=_params(("parallel", "arbitrary")),
    )(page_table, *([pool_t] * PAGES_PER_STEP), cmp_tok, q_rows, qr_rows, gate_rows, new_sel, win_state, new_win,
      overlap)


def _layer_weights(w_in, pe_cmp, w_ck1, w_ck2, w_cv1, w_cv2, p_a, p_b, w_o, w_cq, w_ckv, w_co, w_pq, peer_u, peer_v):
    w_proj = jnp.concatenate(
        [w_in[:, :N_MAIN], jnp.pad(w_in[:, N_MAIN:N_MAIN + N_GATES], ((0, 0), (0, LANES - N_GATES))),
         w_in[:, N_MAIN + N_GATES:]], axis=1).astype(BF16)
    uv = jnp.concatenate([peer_u, peer_v], axis=1)[:, None, :]
    return dict(w_proj=w_proj, cw=_compress_weights(pe_cmp, w_ck1, w_ck2, w_cv1, w_cv2),
                p_a=p_a.astype(BF16), p_b=p_b.astype(BF16), w_o=w_o.astype(BF16), w_cq=w_cq.astype(BF16),
                w_ckv=w_ckv.astype(BF16), w_co=w_co.astype(BF16), w_pq=w_pq.astype(BF16),
                uv=uv)


def _channel_and_norm(h2d, batch, mem_kv, w, g_cross, g_ffn, sub_k1, sub_k2, g_final):
    h3 = _cross(h2d.reshape(batch, -1, D_MODEL), mem_kv, g_cross, w["w_cq"], w["w_co"])
    h2 = h3.reshape(-1, D_MODEL)
    z, eid, gate = _peer_select(h2, g_ffn, w["w_pq"], sub_k1, sub_k2)
    return _peer_gather(eid, h2, z, gate, g_final, w["uv"])


def _prompt_group(x, mem, w, g_attn, g_cross, g_mem, g_ffn, sub_k1, sub_k2, g_final):
    b, seq, _ = x.shape
    x2d = x.reshape(b * seq, D_MODEL)
    aq, akv, bq, bqr, bkv, bwin, gates, ga, gb = _projection(x2d, jnp.arange(seq, dtype=jnp.int32), seq, g_attn,
                                                             w["w_proj"])
    r3 = lambda a: a.reshape(b, seq, a.shape[-1])
    oa = _moba_prompt(r3(aq), r3(akv))
    cmp_tok = _compress_prompt(r3(bkv), w["cw"])
    ob = _nsa_prompt(r3(bq), r3(bqr), r3(gates), r3(bkv), r3(bwin), cmp_tok)
    h = _merge(x2d, oa.reshape(-1, A_WIDTH), ob.reshape(-1, B_WIDTH), ga, gb, w["p_a"], w["p_b"], w["w_o"])
    mlen = mem.shape[1]
    mem_kv = _rms_matmul(mem.reshape(b * mlen, D_MODEL), g_mem, w["w_ckv"]).reshape(b, mlen, 2 * C_WIDTH)
    y = _channel_and_norm(h, b, mem_kv, w, g_cross, g_ffn, sub_k1, sub_k2, g_final)
    win = r3(bwin)[:, seq - min(WINDOW, seq):]
    return (y.reshape(b, seq, D_MODEL), r3(akv).reshape(b, seq, 2, A_HEADS, HEAD_DIM),
            r3(bkv).reshape(b, seq, 4, B_KV_HEADS, HEAD_DIM), win.reshape(b, -1, 2, B_KV_HEADS, HEAD_DIM),
            mem_kv.reshape(b, mlen, 2, C_HEADS, C_HEAD_DIM))


def _pad_rows(a, rows):
    return jnp.pad(a, ((0, 0), (0, rows - a.shape[1]), (0, 0)))


def _sample_group(x, moba_pool, nsa_pool, win_state, mem_kv, page_table, w, g_attn, g_cross, g_ffn, sub_k1, sub_k2,
                  g_final):
    b, t, _ = x.shape
    past = page_table.shape[1] * PAGE_SIZE
    assert t * B_KV_HEADS == SUBLANES and past % (PAGES_PER_STEP * PAGE_SIZE) == 0 and B_HEADS * t <= SAMPLE_ROWS
    x2d = x.reshape(b * t, D_MODEL)
    pos = past + jnp.arange(t, dtype=jnp.int32)
    aq, akv, bq, bqr, bkv, bwin, gates, ga, gb = _projection(x2d, pos, t, g_attn, w["w_proj"])
    r3 = lambda a: a.reshape(b, t, a.shape[-1])

    n_pool = moba_pool.shape[0]
    pool_a = moba_pool.transpose(0, 2, 3, 4, 1).reshape(n_pool, 2 * A_WIDTH, PAGE_SIZE)
    pool_b = nsa_pool.transpose(0, 2, 3, 4, 1).reshape(n_pool, 4 * B_KV_WIDTH, PAGE_SIZE)

    n_blocks = past // A_BLOCK
    kmean_t = _moba_kmean(pool_a, page_table, -(-n_blocks // LANES) * LANES)
    qa = aq.reshape(b, t, A_HEADS, HEAD_DIM).transpose(0, 2, 1, 3)
    qa_rows = (qa[:, :, :, None, :] * jnp.eye(A_HEADS, dtype=F32)[None, :, None, :, None]).reshape(b, A_HEADS * t, A_WIDTH)
    oa_rows = _moba_sample(pool_a, page_table, kmean_t, _pad_rows(qa_rows, SAMPLE_ROWS),
                           _pad_rows(r3(akv), SUBLANES), past, t)
    oa = oa_rows.reshape(b, A_HEADS, t, A_WIDTH).sum(axis=1)

    cmp_tok = _compress_sample(pool_b, page_table, w["cw"], past)
    eye_g = jnp.eye(B_KV_HEADS, dtype=F32)[None, None, :, None, :, None]

    def group_rows(a):
        a = a.reshape(b, t, B_KV_HEADS, B_GROUP, HEAD_DIM).transpose(0, 3, 2, 1, 4)
        return _pad_rows((a[:, :, :, :, None, :] * eye_g).reshape(b, B_HEADS * t, LANES), SAMPLE_ROWS)

    gate_rows = gates[:, :N_GATES].reshape(b, t, B_KV_HEADS, B_GROUP, 3).transpose(0, 3, 2, 1, 4)
    gate_rows = jnp.pad(gate_rows.reshape(b, B_HEADS * t, 3), ((0, 0), (0, SAMPLE_ROWS - B_HEADS * t), (0, LANES - 3)))
    win_rows = win_state.reshape(b, win_state.shape[1], 2 * B_KV_WIDTH)
    ob_rows = _nsa_sample(pool_b, page_table, cmp_tok, group_rows(bq), group_rows(bqr), gate_rows,
                          _pad_rows(r3(bkv)[:, :, 2 * B_KV_WIDTH:], SUBLANES), win_rows,
                          _pad_rows(r3(bwin), SUBLANES), past, t)
    ob = ob_rows.reshape(b, B_GROUP, B_KV_HEADS, t, B_KV_HEADS, HEAD_DIM).sum(axis=4)
    ob = ob.transpose(0, 3, 2, 1, 4).reshape(b * t, B_WIDTH)

    h = _merge(x2d, oa.reshape(-1, A_WIDTH), ob, ga, gb, w["p_a"], w["p_b"], w["w_o"])
    mem_rows = mem_kv.reshape(b, mem_kv.shape[1], 2 * C_WIDTH)
    y = _channel_and_norm(h, b, mem_rows, w, g_cross, g_ffn, sub_k1, sub_k2, g_final)
    win = jnp.concatenate([win_rows, r3(bwin)], axis=1)
    win = win[:, win.shape[1] - min(WINDOW, win.shape[1]):]
    return (y.reshape(b, t, D_MODEL), r3(akv).reshape(b, t, 2, A_HEADS, HEAD_DIM),
            r3(bkv).reshape(b, t, 4, B_KV_HEADS, HEAD_DIM), win.reshape(b, -1, 2, B_KV_HEADS, HEAD_DIM))


def kernel(x_prompt, x_sample, cache_moba_kv, cache_nsa_kv, state_nsa_win, cache_mem_kv, page_table, mem_prompt, g_attn, w_in, pe_cmp, w_ck1, w_ck2, w_cv1, w_cv2, p_a, p_b, w_o, g_cross, g_mem, w_cq, w_ckv, w_co, g_ffn, w_pq, sub_k1, sub_k2, peer_u, peer_v, g_final):
    assert g_attn.shape[0] == 1, "the final norm is fused into the last PEER step of a single layer"
    w = _layer_weights(w_in[0], pe_cmp[0], w_ck1[0], w_ck2[0], w_cv1[0], w_cv2[0], p_a[0], p_b[0], w_o[0], w_cq[0],
                       w_ckv[0], w_co[0], w_pq[0], peer_u[0], peer_v[0])
    y_p, moba_p, nsa_p, win_p, mem_p = _prompt_group(x_prompt, mem_prompt, w, g_attn[0], g_cross[0], g_mem[0],
                                                    g_ffn[0], sub_k1[0], sub_k2[0], g_final)
    y_s, moba_s, nsa_s, win_s = _sample_group(x_sample, cache_moba_kv[0], cache_nsa_kv[0], state_nsa_win[0],
                                              cache_mem_kv[0], page_table, w, g_attn[0], g_cross[0], g_ffn[0],
                                              sub_k1[0], sub_k2[0], g_final)
    return (y_p, y_s, moba_p[None], moba_s[None], nsa_p[None], nsa_s[None], win_p[None], win_s[None], mem_p[None])
```
